```python
import math
import jax, jax.numpy as jnp
from jax import lax
import numpy as np

D_MODEL = 2048
BATCH = 16
SEQ = 256
DEPTH = 2
DEC_BATCH = 2
DEC_SEQ = 1024
PAST_LEN = 512

GRID_W = 64
N_EVEN = (DEPTH + 1) // 2
N_ODD = DEPTH // 2
N_DIR = 2
N_MOD = 9
EPS = 1e-6
D_FF = 5632

S5_WIDTH = D_MODEL // 2
S5_GROUP = 16
S5_GROUPS = S5_WIDTH // S5_GROUP
S5_STATE = 64
S5_DT_MIN = 1e-3
S5_DT_MAX = 1e-1

GLA_HEADS = 4
GLA_V = D_MODEL // 2
GLA_QK = GLA_V // 2
GLA_HEAD_K = GLA_QK // GLA_HEADS
GLA_HEAD_V = GLA_V // GLA_HEADS
GLA_RANK = 16
GLA_TAU = 16.0
GLA_CHUNK = 64

EV_IN = S5_WIDTH + 2 * GLA_QK + 2 * GLA_V + N_DIR * GLA_RANK
EV_MIX = S5_WIDTH + GLA_V

LRU_WIDTH = D_MODEL
LRU_HEADS = 8
LRU_BLOCK = LRU_WIDTH // LRU_HEADS
LRU_C = 8.0
CONV_W = 4
CONV_LEFT = 2

kernel_name = 'hybrid_s5_gla_rglru_diffusion_step'


def rmsnorm(x, g):
    xf = x.astype(jnp.float32)
    y = xf * lax.rsqrt(jnp.mean(xf * xf, axis=-1, keepdims=True) + EPS)
    return (y * g.astype(jnp.float32)).astype(x.dtype)


def swiglu(h, w_in, w_out):
    a, b = jnp.split(h @ w_in, 2, axis=-1)
    return (jax.nn.silu(a) * b) @ w_out


def flip(t):
    return t[:, ::-1]


def _cplx_combine(e1, e2):
    a1r, a1i, b1r, b1i = e1
    a2r, a2i, b2r, b2i = e2
    return (a2r * a1r - a2i * a1i,
            a2r * a1i + a2i * a1r,
            a2r * b1r - a2i * b1i + b2r,
            a2r * b1i + a2i * b1r + b2i)


def _real_combine(e1, e2):
    a1, b1 = e1
    a2, b2 = e2
    return a1 * a2, a2 * b1 + b2


def s5_direction(u, lam_re, lam_im, log_step, b_re, b_im, c_re, c_im, h0_re, h0_im):
    dt = jnp.exp(log_step)[:, None]
    z_re, z_im = lam_re * dt, lam_im * dt
    mag = jnp.exp(z_re)
    ab_re, ab_im = mag * jnp.cos(z_im), mag * jnp.sin(z_im)
    den = lam_re * lam_re + lam_im * lam_im
    n_re = ab_re - 1.0
    f_re = (n_re * lam_re + ab_im * lam_im) / den
    f_im = (ab_im * lam_re - n_re * lam_im) / den
    bb_re = f_re[..., None] * b_re - f_im[..., None] * b_im
    bb_im = f_re[..., None] * b_im + f_im[..., None] * b_re
    bu_re = jnp.einsum('gpn,blgn->blgp', bb_re, u)
    bu_im = jnp.einsum('gpn,blgn->blgp', bb_im, u)
    bu_re = bu_re.at[:, 0].add(ab_re * h0_re - ab_im * h0_im)
    bu_im = bu_im.at[:, 0].add(ab_re * h0_im + ab_im * h0_re)
    a_re = jnp.broadcast_to(ab_re, bu_re.shape)
    a_im = jnp.broadcast_to(ab_im, bu_im.shape)
    _, _, h_re, h_im = lax.associative_scan(_cplx_combine, (a_re, a_im, bu_re, bu_im), axis=1)
    y = jnp.einsum('gnp,blgp->blgn', c_re, h_re) - jnp.einsum('gnp,blgp->blgn', c_im, h_im)
    return y, h_re[:, -1], h_im[:, -1]


def gla_direction(q, k, v, log_a, s0):
    bsz, L, H, _ = q.shape
    n = L // GLA_CHUNK

    def chunks(t):
        return t.reshape(bsz, n, GLA_CHUNK, H, t.shape[-1]).transpose(1, 0, 3, 2, 4)

    qc, kc, vc, gc = chunks(q), chunks(k), chunks(v), chunks(log_a)
    bcum = jnp.cumsum(gc, axis=3)
    blast = bcum[:, :, :, -1:, :]
    q_t = qc * jnp.exp(bcum)
    k_t = kc * jnp.exp(-bcum)
    k_end = kc * jnp.exp(blast - bcum)
    mask = jnp.tril(jnp.ones((GLA_CHUNK, GLA_CHUNK), dtype=bool))
    att = jnp.where(mask, jnp.einsum('nbhid,nbhjd->nbhij', q_t, k_t), 0.0)
    o_intra = jnp.einsum('nbhij,nbhjv->nbhiv', att, vc)
    kv_chunk = jnp.einsum('nbhjd,nbhjv->nbhdv', k_end, vc)
    decay_chunk = jnp.exp(blast[:, :, :, 0, :])

    def step(s, inp):
        qt, kv, dec = inp
        o_inter = jnp.einsum('bhid,bhdv->bhiv', qt, s)
        return dec[..., None] * s + kv, o_inter

    s_fin, o_inter = lax.scan(step, s0, (q_t, kv_chunk, decay_chunk))
    o = (o_intra + o_inter).transpose(1, 0, 3, 2, 4).reshape(bsz, L, H, v.shape[-1])
    return o, s_fin


def even_mixer(h, p, e, s5_h0_re, s5_h0_im, gla_s0):
    f32 = jnp.float32
    bsz, L, _ = h.shape
    cuts = [S5_WIDTH, S5_WIDTH + GLA_QK, S5_WIDTH + 2 * GLA_QK,
            S5_WIDTH + 2 * GLA_QK + GLA_V, S5_WIDTH + 2 * GLA_QK + 2 * GLA_V]
    u, q, k, v, g, glr = jnp.split(h @ p['ev_w_in'][e], cuts, axis=-1)

    uf = u.astype(f32)
    ug = uf.reshape(bsz, L, S5_GROUPS, S5_GROUP)
    y_s5 = 0.0
    s5_re, s5_im = [], []
    for d in range(N_DIR):
        orient = (lambda t: t) if d == 0 else flip
        y, hr, hi = s5_direction(orient(ug),
                                 p['s5_lam_re'][e, d].astype(f32), p['s5_lam_im'][e, d].astype(f32),
                                 p['s5_log_step'][e, d].astype(f32),
                                 p['s5_b_re'][e, d].astype(f32), p['s5_b_im'][e, d].astype(f32),
                                 p['s5_c_re'][e, d].astype(f32), p['s5_c_im'][e, d].astype(f32),
                                 s5_h0_re[:, d].astype(f32), s5_h0_im[:, d].astype(f32))
        y_s5 = y_s5 + orient(y)
        s5_re.append(hr)
        s5_im.append(hi)
    y_s5 = y_s5.reshape(bsz, L, S5_WIDTH) + p['s5_d'][e].astype(f32) * uf
    y_s5 = jax.nn.gelu(y_s5)
    y_s5 = y_s5 * jax.nn.sigmoid(y_s5 @ p['s5_glu_w'][e].astype(f32) + p['s5_glu_b'][e].astype(f32))

    qh = q.astype(f32).reshape(bsz, L, GLA_HEADS, GLA_HEAD_K) * (GLA_HEAD_K ** -0.5)
    kh = k.astype(f32).reshape(bsz, L, GLA_HEADS, GLA_HEAD_K)
    vh = v.astype(f32).reshape(bsz, L, GLA_HEADS, GLA_HEAD_V)
    glr = glr.astype(f32).reshape(bsz, L, N_DIR, GLA_RANK)
    o_sum = 0.0
    gla_s = []
    for d in range(N_DIR):
        orient = (lambda t: t) if d == 0 else flip
        log_a = jax.nn.log_sigmoid(glr[:, :, d] @ p['gla_gate_w2'][e, d].astype(f32)
                                   + p['gla_gate_b'][e, d].astype(f32)) / GLA_TAU
        log_a = log_a.reshape(bsz, L, GLA_HEADS, GLA_HEAD_K)
        o, s = gla_direction(orient(qh), orient(kh), orient(vh), orient(log_a), gla_s0[:, d].astype(f32))
        o_sum = o_sum + orient(o)
        gla_s.append(s)
    o = o_sum * lax.rsqrt(jnp.mean(o_sum * o_sum, axis=-1, keepdims=True) + EPS) * p['gla_norm_g'][e].astype(f32)
    o = o.reshape(bsz, L, GLA_V) * jax.nn.silu(g.astype(f32))

    y = jnp.concatenate([y_s5, o], axis=-1).astype(h.dtype) @ p['ev_w_out'][e]
    return (y, jnp.stack(s5_re, 1).astype(h.dtype), jnp.stack(s5_im, 1).astype(h.dtype),
            jnp.stack(gla_s, 1).astype(h.dtype))


def depthwise_conv(x, w, b):
    y = lax.conv_general_dilated(x, w[:, None, :], window_strides=(1,),
                                 padding=[(CONV_LEFT, CONV_W - 1 - CONV_LEFT)],
                                 dimension_numbers=('NWC', 'WIO', 'NWC'),
                                 feature_group_count=x.shape[-1])
    return y + b


def rglru_direction(x, wa, ba, wx, bx, lam, h0):
    bsz, L, W = x.shape
    xb = x.reshape(bsz, L, LRU_HEADS, LRU_BLOCK)
    r = jax.nn.sigmoid(jnp.einsum('blhi,hij->blhj', xb, wa).reshape(bsz, L, W) + ba)
    i = jax.nn.sigmoid(jnp.einsum('blhi,hij->blhj', xb, wx).reshape(bsz, L, W) + bx)
    log_a = -LRU_C * r * jax.nn.softplus(-lam)
    a = jnp.exp(log_a)
    b = jnp.sqrt(-jnp.expm1(2.0 * log_a)) * (i * x)
    b = b.at[:, 0].add(a[:, 0] * h0)
    _, hs = lax.associative_scan(_real_combine, (a, b), axis=1)
    return hs, hs[:, -1]


def odd_mixer(h, p, o, lru_h0, grid):
    f32 = jnp.float32
    bsz, L, _ = h.shape
    gate_br, x_br = jnp.split(h @ p['od_w_in'][o], 2, axis=-1)
    cw = p['lru_conv_w'][o].astype(f32)
    cb = p['lru_conv_b'][o].astype(f32)
    xf = x_br.astype(f32)
    if grid:
        rows = L // GRID_W
        xc = depthwise_conv(xf.reshape(bsz * rows, GRID_W, LRU_WIDTH), cw, cb).reshape(bsz, L, LRU_WIDTH)
    else:
        xc = depthwise_conv(xf, cw, cb)
    hsum = 0.0
    states = []
    for d in range(N_DIR):
        orient = (lambda t: t) if d == 0 else flip
        hs, s = rglru_direction(orient(xc),
                                p['lru_wa'][o, d].astype(f32), p['lru_ba'][o, d].astype(f32),
                                p['lru_wx'][o, d].astype(f32), p['lru_bx'][o, d].astype(f32),
                                p['lru_lam'][o, d].astype(f32), lru_h0[:, d].astype(f32))
        hsum = hsum + orient(hs)
        states.append(s)
    y = (hsum * jax.nn.gelu(gate_br.astype(f32))).astype(h.dtype) @ p['od_w_out'][o]
    return y, jnp.stack(states, 1).astype(h.dtype)


def trunk(x, cond, s5_re0, s5_im0, gla0, lru0, p, grid):
    cond_act = jax.nn.silu(cond)
    s5_re_out, s5_im_out, gla_out, lru_out = [], [], [], []
    for l in range(DEPTH):
        mod = (cond_act @ p['ada_w'][l] + p['ada_b'][l]).reshape(cond.shape[0], 1, N_MOD, D_MODEL)
        sh1, sc1, g1, sh2, sc2, g2, sh3, sc3, g3 = [mod[:, :, j] for j in range(N_MOD)]
        hm = rmsnorm(x, p['norm_g'][l, 0]) * (1.0 + sc1) + sh1
        x = x + 0.5 * g1 * swiglu(hm, p['ffn_w_in'][l, 0], p['ffn_w_out'][l, 0])
        hm = rmsnorm(x, p['norm_g'][l, 1]) * (1.0 + sc2) + sh2
        if l % 2 == 0:
            e = l // 2
            y, sr, si, sg = even_mixer(hm, p, e, s5_re0[:, e], s5_im0[:, e], gla0[:, e])
            s5_re_out.append(sr)
            s5_im_out.append(si)
            gla_out.append(sg)
        else:
            o = l // 2
            y, sl = odd_mixer(hm, p, o, lru0[:, o], grid)
            lru_out.append(sl)
        x = x + g2 * y
        hm = rmsnorm(x, p['norm_g'][l, 2]) * (1.0 + sc3) + sh3
        x = x + 0.5 * g3 * swiglu(hm, p['ffn_w_in'][l, 1], p['ffn_w_out'][l, 1])
    return (rmsnorm(x, p['final_norm_g']), jnp.stack(s5_re_out, 1), jnp.stack(s5_im_out, 1),
            jnp.stack(gla_out, 1), jnp.stack(lru_out, 1))


def setup_inputs(seed: int = 0) -> dict:
    key = jax.random.key(seed)
    ks = iter(jax.random.split(key, 48))
    f32 = jnp.float32

    def nrm(shape, scale):
        return scale * jax.random.normal(next(ks), shape, f32)

    def unif(shape, lo, hi):
        return jax.random.uniform(next(ks), shape, f32, lo, hi)

    lam_im_base = jnp.pi * jnp.arange(S5_STATE, dtype=f32)
    lru_s = unif((N_ODD, N_DIR, LRU_WIDTH), 0.9, 0.999) ** (1.0 / LRU_C)
    return {
        'x_prompt': nrm((BATCH, SEQ, D_MODEL), 1.0),
        'x_sample': nrm((DEC_BATCH, DEC_SEQ, D_MODEL), 1.0),
        'state_s5_re': nrm((DEC_BATCH, N_EVEN, N_DIR, S5_GROUPS, S5_STATE), 0.5),
        'state_s5_im': nrm((DEC_BATCH, N_EVEN, N_DIR, S5_GROUPS, S5_STATE), 0.5),
        'state_gla': nrm((DEC_BATCH, N_EVEN, N_DIR, GLA_HEADS, GLA_HEAD_K, GLA_HEAD_V), 1.0),
        'state_lru': nrm((DEC_BATCH, N_ODD, N_DIR, LRU_WIDTH), 0.5),
        'c': nrm((DEC_BATCH, D_MODEL), 1.0),
        'c_ctx': nrm((D_MODEL,), 1.0),
        'norm_g': 1.0 + nrm((DEPTH, 3, D_MODEL), 0.02),
        'ada_w': nrm((DEPTH, D_MODEL, N_MOD * D_MODEL), 0.5 * D_MODEL ** -0.5),
        'ada_b': nrm((DEPTH, N_MOD * D_MODEL), 0.02),
        'ffn_w_in': nrm((DEPTH, 2, D_MODEL, 2 * D_FF), D_MODEL ** -0.5),
        'ffn_w_out': nrm((DEPTH, 2, D_FF, D_MODEL), D_FF ** -0.5),
        'final_norm_g': 1.0 + nrm((D_MODEL,), 0.02),
        'ev_w_in': nrm((N_EVEN, D_MODEL, EV_IN), D_MODEL ** -0.5),
        'ev_w_out': nrm((N_EVEN, EV_MIX, D_MODEL), EV_MIX ** -0.5),
        's5_lam_re': -0.5 + nrm((N_EVEN, N_DIR, S5_GROUPS, S5_STATE), 0.01),
        's5_lam_im': lam_im_base + nrm((N_EVEN, N_DIR, S5_GROUPS, S5_STATE), 0.01),
        's5_log_step': unif((N_EVEN, N_DIR, S5_GROUPS), math.log(S5_DT_MIN), math.log(S5_DT_MAX)),
        's5_b_re': nrm((N_EVEN, N_DIR, S5_GROUPS, S5_STATE, S5_GROUP), (2 * S5_GROUP) ** -0.5),
        's5_b_im': nrm((N_EVEN, N_DIR, S5_GROUPS, S5_STATE, S5_GROUP), (2 * S5_GROUP) ** -0.5),
        's5_c_re': nrm((N_EVEN, N_DIR, S5_GROUPS, S5_GROUP, S5_STATE), S5_STATE ** -0.5),
        's5_c_im': nrm((N_EVEN, N_DIR, S5_GROUPS, S5_GROUP, S5_STATE), S5_STATE ** -0.5),
        's5_d': nrm((N_EVEN, S5_WIDTH), 0.5),
        's5_glu_w': nrm((N_EVEN, S5_WIDTH, S5_WIDTH), S5_WIDTH ** -0.5),
        's5_glu_b': nrm((N_EVEN, S5_WIDTH), 0.02),
        'gla_gate_w2': nrm((N_EVEN, N_DIR, GLA_RANK, GLA_QK), GLA_RANK ** -0.5),
        'gla_gate_b': nrm((N_EVEN, N_DIR, GLA_QK), 0.1),
        'gla_norm_g': 1.0 + nrm((N_EVEN, GLA_HEAD_V), 0.02),
        'od_w_in': nrm((N_ODD, D_MODEL, 2 * LRU_WIDTH), D_MODEL ** -0.5),
        'od_w_out': nrm((N_ODD, LRU_WIDTH, D_MODEL), LRU_WIDTH ** -0.5),
        'lru_conv_w': nrm((N_ODD, CONV_W, LRU_WIDTH), CONV_W ** -0.5),
        'lru_conv_b': nrm((N_ODD, LRU_WIDTH), 0.02),
        'lru_wa': nrm((N_ODD, N_DIR, LRU_HEADS, LRU_BLOCK, LRU_BLOCK), LRU_BLOCK ** -0.5),
        'lru_ba': nrm((N_ODD, N_DIR, LRU_WIDTH), 0.02),
        'lru_wx': nrm((N_ODD, N_DIR, LRU_HEADS, LRU_BLOCK, LRU_BLOCK), LRU_BLOCK ** -0.5),
        'lru_bx': nrm((N_ODD, N_DIR, LRU_WIDTH), 0.02),
        'lru_lam': jnp.log(lru_s) - jnp.log1p(-lru_s),
    }


def reference(x_prompt, x_sample, state_s5_re, state_s5_im, state_gla, state_lru, c, c_ctx,
              norm_g, ada_w, ada_b, ffn_w_in, ffn_w_out, final_norm_g,
              ev_w_in, ev_w_out, s5_lam_re, s5_lam_im, s5_log_step, s5_b_re, s5_b_im, s5_c_re, s5_c_im,
              s5_d, s5_glu_w, s5_glu_b, gla_gate_w2, gla_gate_b, gla_norm_g,
              od_w_in, od_w_out, lru_conv_w, lru_conv_b, lru_wa, lru_ba, lru_wx, lru_bx, lru_lam):
    p = dict(norm_g=norm_g, ada_w=ada_w, ada_b=ada_b, ffn_w_in=ffn_w_in, ffn_w_out=ffn_w_out,
             final_norm_g=final_norm_g, ev_w_in=ev_w_in, ev_w_out=ev_w_out,
             s5_lam_re=s5_lam_re, s5_lam_im=s5_lam_im, s5_log_step=s5_log_step,
             s5_b_re=s5_b_re, s5_b_im=s5_b_im, s5_c_re=s5_c_re, s5_c_im=s5_c_im,
             s5_d=s5_d, s5_glu_w=s5_glu_w, s5_glu_b=s5_glu_b,
             gla_gate_w2=gla_gate_w2, gla_gate_b=gla_gate_b, gla_norm_g=gla_norm_g,
             od_w_in=od_w_in, od_w_out=od_w_out, lru_conv_w=lru_conv_w, lru_conv_b=lru_conv_b,
             lru_wa=lru_wa, lru_ba=lru_ba, lru_wx=lru_wx, lru_bx=lru_bx, lru_lam=lru_lam)
    bsz = x_prompt.shape[0]
    dt = x_prompt.dtype
    y_prompt, new_s5_re, new_s5_im, new_gla, new_lru = trunk(
        x_prompt, c_ctx[None, :],
        jnp.zeros((bsz, N_EVEN, N_DIR, S5_GROUPS, S5_STATE), dt),
        jnp.zeros((bsz, N_EVEN, N_DIR, S5_GROUPS, S5_STATE), dt),
        jnp.zeros((bsz, N_EVEN, N_DIR, GLA_HEADS, GLA_HEAD_K, GLA_HEAD_V), dt),
        jnp.zeros((bsz, N_ODD, N_DIR, LRU_WIDTH), dt),
        p, False)
    y_sample, _, _, _, _ = trunk(x_sample, c, state_s5_re, state_s5_im, state_gla, state_lru, p, True)
    return (y_prompt, y_sample, new_s5_re, new_s5_im, new_gla, new_lru)
```

```python
import functools
import math

import jax
import jax.numpy as jnp
from jax import lax
from jax.experimental import pallas as pl
from jax.experimental.pallas import tpu as pltpu

F32 = jnp.float32
BF16 = jnp.bfloat16
EPS = 1e-6
NPS = 16
N_MOD = 9
GLA_CHUNK = 64
GLA_TAU = 16.0
LRU_C = 8.0
S5_GROUP = 16
S5_STATE = 64
S5_GPC = 16
VMEM_LIMIT = 56 * 1024 * 1024


def _cparams(*sem):
    return pltpu.CompilerParams(dimension_semantics=sem, vmem_limit_bytes=VMEM_LIMIT)


def _dot(a, b):
    return jnp.dot(a, b, preferred_element_type=F32)


def _dot_nt(a, b):
    return lax.dot_general(a, b, (((1,), (1,)), ((), ())), preferred_element_type=F32)


def _dot_tn(a, b):
    return lax.dot_general(a, b, (((0,), (0,)), ((), ())), preferred_element_type=F32)


def _silu(x):
    return x * jax.nn.sigmoid(x)


def _gelu(x):
    return 0.5 * x * (1.0 + jnp.tanh(math.sqrt(2.0 / math.pi) * (x + 0.044715 * (x * x * x))))


def _softplus(x):
    return jnp.maximum(x, 0.0) + jnp.log1p(jnp.exp(-jnp.abs(x)))


def _rms_mod(x, g, scale, shift):
    y = x * lax.rsqrt(jnp.mean(x * x, axis=-1, keepdims=True) + EPS) * g
    return y * (1.0 + scale) + shift


def _ada_kernel(c_ref, w_ref, b_ref, o_ref):
    ca = _silu(c_ref[...])
    o_ref[...] = _dot(ca.astype(BF16), w_ref[...].astype(BF16)) + b_ref[...]


def _ada(cond8, ada_w, ada_b):
    depth, d, n = ada_w.shape
    tn = 1024
    return pl.pallas_call(
        _ada_kernel, grid=(depth, n // tn),
        in_specs=[pl.BlockSpec((8, d), lambda l, j: (0, 0)),
                  pl.BlockSpec((None, d, tn), lambda l, j: (l, 0, j)),
                  pl.BlockSpec((None, 1, tn), lambda l, j: (l, 0, j))],
        out_specs=pl.BlockSpec((None, 8, tn), lambda l, j: (l, 0, j)),
        out_shape=jax.ShapeDtypeStruct((depth, 8, n), F32),
        compiler_params=_cparams("arbitrary", "arbitrary"), name="ada",
    )(cond8, ada_w, ada_b.reshape(depth, 1, n))


ROW_CHUNK = 64


def _ffn_kernel(x_ref, mod_ref, g_ref, fg_ref, wa_ref, wb_ref, wo_ref, o_ref, h_sc, acc_sc, *, mi, nf, final):
    j = pl.program_id(1)
    tm = x_ref.shape[0]

    @pl.when(j == 0)
    def _():
        def body(r, c):
            rows = pl.ds(pl.multiple_of(r * ROW_CHUNK, ROW_CHUNK), ROW_CHUNK)
            h = _rms_mod(x_ref[rows, :], g_ref[...], mod_ref[mi + 1:mi + 2, :], mod_ref[mi:mi + 1, :])
            h_sc[rows, :] = h.astype(BF16)
            return c
        lax.fori_loop(0, tm // ROW_CHUNK, body, 0)

    h = h_sc[...]
    a = _dot(h, wa_ref[...])
    b = _dot(h, wb_ref[...])
    part = _dot((_silu(a) * b).astype(BF16), wo_ref[...])

    @pl.when(j == 0)
    def _():
        acc_sc[...] = part

    @pl.when(j > 0)
    def _():
        acc_sc[...] += part

    @pl.when(j == nf - 1)
    def _():
        def body(r, c):
            rows = pl.ds(pl.multiple_of(r * ROW_CHUNK, ROW_CHUNK), ROW_CHUNK)
            y = x_ref[rows, :] + 0.5 * mod_ref[mi + 2:mi + 3, :] * acc_sc[rows, :]
            if final:
                y = y * lax.rsqrt(jnp.mean(y * y, axis=-1, keepdims=True) + EPS) * fg_ref[...]
            o_ref[rows, :] = y
            return c
        lax.fori_loop(0, tm // ROW_CHUNK, body, 0)


def _ffn(x, mod_l, norm_g, final_g, w_in, w_out, l, k, *, mi, row_base, rows_per_cond, final):
    m, d = x.shape
    f = w_out.shape[2]
    tm = min(512, rows_per_cond)
    tf = 512
    nf = f // tf
    row = lambda i, j: (row_base + (i * tm) // rows_per_cond, 0, 0)
    return pl.pallas_call(
        functools.partial(_ffn_kernel, mi=mi, nf=nf, final=final),
        grid=(m // tm, nf),
        in_specs=[pl.BlockSpec((tm, d), lambda i, j: (i, 0)),
                  pl.BlockSpec((None, N_MOD, d), row),
                  pl.BlockSpec((1, d), lambda i, j: (0, 0)),
                  pl.BlockSpec((1, d), lambda i, j: (0, 0)),
                  pl.BlockSpec((None, None, d, tf), lambda i, j: (l, k, 0, j)),
                  pl.BlockSpec((None, None, d, tf), lambda i, j: (l, k, 0, j + nf)),
                  pl.BlockSpec((None, None, tf, d), lambda i, j: (l, k, j, 0))],
        out_specs=pl.BlockSpec((tm, d), lambda i, j: (i, 0)),
        out_shape=jax.ShapeDtypeStruct((m, d), F32),
        scratch_shapes=[pltpu.VMEM((tm, d), BF16), pltpu.VMEM((tm, d), F32)],
        compiler_params=_cparams("arbitrary", "arbitrary"), name="ffn",
    )(x, mod_l, norm_g, final_g, w_in, w_in, w_out)


def _normmod_kernel(x_ref, mod_ref, g_ref, o_ref, *, mi):
    def body(r, c):
        rows = pl.ds(pl.multiple_of(r * ROW_CHUNK, ROW_CHUNK), ROW_CHUNK)
        h = _rms_mod(x_ref[rows, :], g_ref[...], mod_ref[mi + 1:mi + 2, :], mod_ref[mi:mi + 1, :])
        o_ref[rows, :] = h.astype(BF16)
        return c
    lax.fori_loop(0, x_ref.shape[0] // ROW_CHUNK, body, 0)


def _normmod(x, mod_l, norm_g, *, mi, row_base, rows_per_cond):
    m, d = x.shape
    tm = min(512, rows_per_cond)
    return pl.pallas_call(
        functools.partial(_normmod_kernel, mi=mi), grid=(m // tm,),
        in_specs=[pl.BlockSpec((tm, d), lambda i: (i, 0)),
                  pl.BlockSpec((None, N_MOD, d), lambda i: (row_base + (i * tm) // rows_per_cond, 0, 0)),
                  pl.BlockSpec((1, d), lambda i: (0, 0))],
        out_specs=pl.BlockSpec((tm, d), lambda i: (i, 0)),
        out_shape=jax.ShapeDtypeStruct((m, d), BF16),
        compiler_params=_cparams("arbitrary"), name="normmod",
    )(x, mod_l, norm_g)


def _proj_kernel(h_ref, w_ref, o_ref):
    o_ref[...] = _dot(h_ref[...], w_ref[...]).astype(o_ref.dtype)


def _proj_bm(h, w, tn):
    m, kd = h.shape
    n = w.shape[1]
    tm = 512
    return pl.pallas_call(
        _proj_kernel, grid=(n // tn, m // tm),
        in_specs=[pl.BlockSpec((tm, kd), lambda j, i: (i, 0)),
                  pl.BlockSpec((kd, tn), lambda j, i: (0, j))],
        out_specs=pl.BlockSpec((tm, tn), lambda j, i: (i, j)),
        out_shape=jax.ShapeDtypeStruct((m, n), F32),
        compiler_params=_cparams("arbitrary", "arbitrary"), name="proj_bm",
    )(h, w)


def _proj_tm(h, w, t):
    kd = h.shape[1]
    f = w.shape[1]
    tn = min(f, 1024)
    nj = f // tn
    out = pl.pallas_call(
        _proj_kernel, grid=(nj, NPS),
        in_specs=[pl.BlockSpec((t, kd), lambda j, i: (i, 0)),
                  pl.BlockSpec((kd, tn), lambda j, i: (0, j))],
        out_specs=pl.BlockSpec((t, tn), lambda j, i: (0, i * nj + j)),
        out_shape=jax.ShapeDtypeStruct((t, NPS * f), F32),
        compiler_params=_cparams("arbitrary", "arbitrary"), name="proj_tm",
    )(h, w)
    return out.reshape(t * NPS, f)


S5_TT = 16
S5_LANES = 256


def _s5_kernel(u_ref, bw_ref, cw_ref, a_ref, at_ref, h0_ref, y_ref, *rest, nblk, nseg, segmented, width):
    if segmented:
        bu_sc, st_sc = rest
    else:
        stout_ref, bu_sc, st_sc = rest
    d = pl.program_id(0)
    ph = pl.program_id(1)
    blk = pl.program_id(2)
    nph = 2 if segmented else 1
    nchunk = width // (S5_GPC * S5_GROUP)
    cin = S5_GPC * S5_GROUP
    cst = S5_GPC * S5_STATE
    half = cst

    @pl.when(blk == 0)
    def _():
        if segmented:
            @pl.when(ph == 0)
            def _():
                st_sc[...] = jnp.zeros_like(st_sc)
        else:
            st_sc[...] = h0_ref[...]

    ub = u_ref[...].astype(BF16)
    for c in range(nchunk):
        bu_sc[:, c * 2 * cst:(c + 1) * 2 * cst] = _dot(ub[:, c * cin:(c + 1) * cin], bw_ref[c])

    def step(s, carry):
        t = jnp.where(d == 0, s, S5_TT - 1 - s)
        rows = pl.ds(pl.multiple_of(t * NPS, NPS), NPS)
        for c in range(nchunk):
            for q in range(half // S5_LANES):
                cr = slice(c * 2 * cst + q * S5_LANES, c * 2 * cst + (q + 1) * S5_LANES)
                ci = slice(c * 2 * cst + half + q * S5_LANES, c * 2 * cst + half + (q + 1) * S5_LANES)
                ar, ai = a_ref[:, cr], a_ref[:, ci]
                hr, hi = st_sc[:, cr], st_sc[:, ci]
                nr = ar * hr - ai * hi + bu_sc[rows, cr]
                ni = ar * hi + ai * hr + bu_sc[rows, ci]
                st_sc[:, cr] = nr
                st_sc[:, ci] = ni
                bu_sc[rows, cr] = nr
                bu_sc[rows, ci] = ni
        return carry

    lax.fori_loop(0, S5_TT, step, 0)

    @pl.when(ph == nph - 1)
    def _():
        for c in range(nchunk):
            hb = bu_sc[:, c * 2 * cst:(c + 1) * 2 * cst].astype(BF16)
            y_ref[:, c * cin:(c + 1) * cin] = _dot(hb, cw_ref[c])

    if segmented:
        nb = NPS // nseg

        def chain(order):
            for b in range(nb):
                for c in range(nchunk):
                    cr = slice(c * 2 * cst, c * 2 * cst + half)
                    ci = slice(c * 2 * cst + half, (c + 1) * 2 * cst)
                    pr, pi = h0_ref[b:b + 1, cr], h0_ref[b:b + 1, ci]
                    ar, ai = at_ref[:, cr], at_ref[:, ci]
                    for k in order:
                        j = b * nseg + k
                        er, ei = st_sc[j:j + 1, cr], st_sc[j:j + 1, ci]
                        st_sc[j:j + 1, cr] = pr
                        st_sc[j:j + 1, ci] = pi
                        pr, pi = ar * pr - ai * pi + er, ar * pi + ai * pr + ei

        @pl.when((ph == 0) & (blk == nblk - 1) & (d == 0))
        def _():
            chain(range(nseg))

        @pl.when((ph == 0) & (blk == nblk - 1) & (d == 1))
        def _():
            chain(range(nseg - 1, -1, -1))
    else:
        @pl.when(blk == nblk - 1)
        def _():
            stout_ref[...] = st_sc[...]


def _s5(u_tm, bw, cw, a_bc, a_t, h0, *, t, nseg):
    segmented = nseg > 1
    width = u_tm.shape[1]
    sw = a_bc.shape[-1]
    nblk = t // S5_TT
    r = S5_TT * NPS
    nph = 2 if segmented else 1
    tb = lambda d, blk: jnp.where(d == 0, blk, nblk - 1 - blk)
    out_shape = [jax.ShapeDtypeStruct((2, t * NPS, width), F32)]
    out_specs = [pl.BlockSpec((None, r, width),
                              lambda d, ph, blk: (d, jnp.where(ph == nph - 1, tb(d, blk), tb(d, 0)), 0))]
    if not segmented:
        out_shape.append(jax.ShapeDtypeStruct((2, NPS, sw), F32))
        out_specs.append(pl.BlockSpec((None, NPS, sw), lambda d, ph, blk: (d, 0, 0)))
    res = pl.pallas_call(
        functools.partial(_s5_kernel, nblk=nblk, nseg=nseg, segmented=segmented, width=width),
        grid=(2, nph, nblk),
        in_specs=[pl.BlockSpec((r, width), lambda d, ph, blk: (tb(d, blk), 0)),
                  pl.BlockSpec((None,) + bw.shape[1:], lambda d, ph, blk: (d, 0, 0, 0)),
                  pl.BlockSpec((None,) + cw.shape[1:], lambda d, ph, blk: (d, 0, 0, 0)),
                  pl.BlockSpec((None, NPS, sw), lambda d, ph, blk: (d, 0, 0)),
                  pl.BlockSpec((None, 1, sw), lambda d, ph, blk: (d, 0, 0)),
                  pl.BlockSpec((None,) + h0.shape[1:], lambda d, ph, blk: (d, 0, 0))],
        out_specs=out_specs, out_shape=out_shape,
        scratch_shapes=[pltpu.VMEM((r, sw), F32), pltpu.VMEM((NPS, sw), F32)],
        compiler_params=_cparams("arbitrary", "arbitrary", "arbitrary"), name="s5_scan",
    )(u_tm, bw, cw, a_bc, a_t, h0)
    return res if not segmented else (res[0], None)


def _s5_params(lam_re, lam_im, log_step, b_re, b_im, c_re, c_im, t_seg):
    nd, g, p = lam_re.shape
    n = b_re.shape[-1]
    nc = g // S5_GPC
    dt = jnp.exp(log_step)[..., None]
    z_re, z_im = lam_re * dt, lam_im * dt
    mag = jnp.exp(z_re)
    ab_re, ab_im = mag * jnp.cos(z_im), mag * jnp.sin(z_im)
    den = lam_re * lam_re + lam_im * lam_im
    n_re = ab_re - 1.0
    f_re = (n_re * lam_re + ab_im * lam_im) / den
    f_im = (ab_im * lam_re - n_re * lam_im) / den
    bb_re = f_re[..., None] * b_re - f_im[..., None] * b_im
    bb_im = f_re[..., None] * b_im + f_im[..., None] * b_re
    magt = jnp.exp(z_re * t_seg)
    at_re, at_im = magt * jnp.cos(z_im * t_seg), magt * jnp.sin(z_im * t_seg)
    eye = jnp.eye(S5_GPC, dtype=F32)

    def pack_b(bb):
        bbc = bb.reshape(nd, nc, S5_GPC, p, n)
        return jnp.einsum('dcgpn,gh->dcgnhp', bbc, eye).reshape(nd, nc, S5_GPC * n, S5_GPC * p)

    def pack_c(cc):
        ccc = cc.reshape(nd, nc, S5_GPC, n, p)
        return jnp.einsum('dcgnp,gh->dcgphn', ccc, eye).reshape(nd, nc, S5_GPC * p, S5_GPC * n)

    bw = jnp.concatenate([pack_b(bb_re), pack_b(bb_im)], axis=-1).astype(BF16)
    cw = jnp.concatenate([pack_c(c_re), pack_c(-c_im)], axis=-2).astype(BF16)

    def cols(re, im):
        return jnp.stack([re.reshape(nd, nc, S5_GPC * p), im.reshape(nd, nc, S5_GPC * p)], axis=2).reshape(nd, -1)

    a = cols(ab_re, ab_im)
    a_bc = jnp.broadcast_to(a[:, None, :], (nd, NPS, a.shape[-1]))
    a_t = cols(at_re, at_im)[:, None, :]
    return bw, cw, a_bc, a_t


def _s5_state_to_cols(s_re, s_im):
    b, nd, g, p = s_re.shape
    nc = g // S5_GPC
    st = jnp.stack([s_re.reshape(b, nd, nc, S5_GPC * p), s_im.reshape(b, nd, nc, S5_GPC * p)], axis=3)
    return st.reshape(b, nd, -1).transpose(1, 0, 2)


def _s5_cols_to_state(st, g, p):
    nd, b, _ = st.shape
    nc = g // S5_GPC
    st = st.reshape(nd, b, nc, 2, S5_GPC, p).transpose(3, 1, 0, 2, 4, 5).reshape(2, b, nd, g, p)
    return st[0], st[1]


GLA_RB = 256


def _gla_kernel(q_ref, k_ref, v_ref, g_ref, glr_ref, w2_ref, gb_ref, ng_ref, *rest, seq, zero_init, want_state):
    rest = list(rest)
    s0_ref = None if zero_init else rest.pop(0)
    o_ref = rest.pop(0)
    sfin_ref = rest.pop(0) if want_state else None
    qt_sc, kt_sc, ke_sc, dec_sc, osum_sc = rest
    dk = q_ref.shape[1]
    dv = v_ref.shape[1]
    n = seq // GLA_CHUNK
    scale = dk ** -0.5
    ri = lax.broadcasted_iota(jnp.int32, (GLA_RB, GLA_RB), 0)
    ci = lax.broadcasted_iota(jnp.int32, (GLA_RB, GLA_RB), 1)
    same = (ri // GLA_CHUNK) == (ci // GLA_CHUNK)
    ones_blk = jnp.where(same, 1.0, 0.0).astype(BF16)

    for d in range(2):
        mask = same & ((ci <= ri) if d == 0 else (ci >= ri))
        tri = jnp.where(mask, 1.0, 0.0).astype(BF16)
        for rb in range(seq // GLA_RB):
            rows = slice(rb * GLA_RB, (rb + 1) * GLA_RB)
            x = _dot(glr_ref[rows, :].astype(BF16), w2_ref[d]) + gb_ref[d]
            la = (jnp.minimum(x, 0.0) - jnp.log1p(jnp.exp(-jnp.abs(x)))) * (1.0 / GLA_TAU)
            la_hi = la.astype(BF16)
            la_lo = (la - la_hi.astype(F32)).astype(BF16)
            bc = _dot(tri, la_hi) + _dot(tri, la_lo)
            tot = _dot(ones_blk, la_hi) + _dot(ones_blk, la_lo)
            qf = q_ref[rows, :] * scale
            kf = k_ref[rows, :]
            q_t = (qf * jnp.exp(bc)).astype(BF16)
            k_t = (kf * jnp.exp(-bc)).astype(BF16)
            qt_sc[d, rows, :] = q_t
            kt_sc[d, rows, :] = k_t
            ke_sc[d, rows, :] = (kf * jnp.exp(tot - bc)).astype(BF16)
            dec_sc[d, rows, :] = jnp.exp(tot)
            att = jnp.where(mask, _dot_nt(q_t, k_t), 0.0).astype(BF16)
            o_intra = _dot(att, v_ref[rows, :].astype(BF16))
            if d == 0:
                osum_sc[rows, :] = o_intra
            else:
                osum_sc[rows, :] += o_intra

        st = jnp.zeros((dv, dk), F32) if zero_init else s0_ref[d].T
        for c in (range(n) if d == 0 else range(n - 1, -1, -1)):
            rows = slice(c * GLA_CHUNK, (c + 1) * GLA_CHUNK)
            osum_sc[rows, :] += _dot_nt(qt_sc[d, rows, :], st.astype(BF16))
            kv = _dot_tn(v_ref[rows, :].astype(BF16), ke_sc[d, rows, :])
            st = st * dec_sc[d, c * GLA_CHUNK:c * GLA_CHUNK + 1, :] + kv
        if want_state:
            sfin_ref[d] = st.T

    for rb in range(seq // GLA_RB):
        rows = slice(rb * GLA_RB, (rb + 1) * GLA_RB)
        o = osum_sc[rows, :]
        o = o * lax.rsqrt(jnp.mean(o * o, axis=-1, keepdims=True) + EPS) * ng_ref[...]
        o_ref[rows, :] = (o * _silu(g_ref[rows, :])).astype(BF16)


def _gla(qkvg, glr, w2p, gate_b, norm_g, s0, *, nb, seq, heads, want_state):
    qk = w2p.shape[-1]
    dk = qk // heads
    vdim = (qkvg.shape[1] - 2 * qk) // 2
    dv = vdim // heads
    zero_init = s0 is None
    in_specs = [pl.BlockSpec((seq, dk), lambda b, h: (b, h)),
                pl.BlockSpec((seq, dk), lambda b, h: (b, heads + h)),
                pl.BlockSpec((seq, dv), lambda b, h: (b, 2 * qk // dv + h)),
                pl.BlockSpec((seq, dv), lambda b, h: (b, (2 * qk + vdim) // dv + h)),
                pl.BlockSpec((seq, glr.shape[1]), lambda b, h: (b, 0)),
                pl.BlockSpec((2, w2p.shape[1], dk), lambda b, h: (0, 0, h)),
                pl.BlockSpec((2, 1, dk), lambda b, h: (0, 0, h)),
                pl.BlockSpec((1, dv), lambda b, h: (0, 0))]
    args = [qkvg, qkvg, qkvg, qkvg, glr, w2p, gate_b, norm_g]
    if not zero_init:
        in_specs.append(pl.BlockSpec((None, 2, None, dk, dv), lambda b, h: (b, 0, h, 0, 0)))
        args.append(s0)
    out_shape = [jax.ShapeDtypeStruct((nb * seq, vdim), BF16)]
    out_specs = [pl.BlockSpec((seq, dv), lambda b, h: (b, h))]
    if want_state:
        out_shape.append(jax.ShapeDtypeStruct((nb, 2, heads, dk, dv), F32))
        out_specs.append(pl.BlockSpec((None, 2, None, dk, dv), lambda b, h: (b, 0, h, 0, 0)))
    res = pl.pallas_call(
        functools.partial(_gla_kernel, seq=seq, zero_init=zero_init, want_state=want_state),
        grid=(nb, heads), in_specs=in_specs, out_specs=out_specs, out_shape=out_shape,
        scratch_shapes=[pltpu.VMEM((2, seq, dk), BF16), pltpu.VMEM((2, seq, dk), BF16),
                        pltpu.VMEM((2, seq, dk), BF16), pltpu.VMEM((2, seq, dk), F32),
                        pltpu.VMEM((seq, dv), F32)],
        compiler_params=_cparams("arbitrary", "arbitrary"), name="gla",
    )(*args)
    return (res[0], res[1]) if want_state else (res[0], None)


def _even_out_kernel(yf_ref, yb_ref, u_ref, o_ref, x_ref, mod_ref, sd_ref, gw_ref, gb_ref, wo_ref, out_ref):
    sw = u_ref.shape[1]
    ys = _gelu(yf_ref[...] + yb_ref[...] + sd_ref[...] * u_ref[...])
    ys = ys * jax.nn.sigmoid(_dot(ys.astype(BF16), gw_ref[...]) + gb_ref[...])
    y = _dot(ys.astype(BF16), wo_ref[:sw, :]) + _dot(o_ref[...], wo_ref[sw:, :])
    out_ref[...] = x_ref[...] + mod_ref[5:6, :] * y


def _even_out(y_dir, u_tm, o_gla, x, mod_l, s5_d, glu_w, glu_b, w_out, *, t, row_base, pseq_per_cond):
    m, d = x.shape
    sw = u_tm.shape[1]
    y3 = y_dir.reshape(2, t, NPS * sw)
    u2 = u_tm.reshape(t, NPS * sw)
    return pl.pallas_call(
        _even_out_kernel, grid=(NPS,),
        in_specs=[pl.BlockSpec((None, t, sw), lambda i: (0, 0, i)),
                  pl.BlockSpec((None, t, sw), lambda i: (1, 0, i)),
                  pl.BlockSpec((t, sw), lambda i: (0, i)),
                  pl.BlockSpec((t, o_gla.shape[1]), lambda i: (i, 0)),
                  pl.BlockSpec((t, d), lambda i: (i, 0)),
                  pl.BlockSpec((None, N_MOD, d), lambda i: (row_base + i // pseq_per_cond, 0, 0)),
                  pl.BlockSpec((1, sw), lambda i: (0, 0)),
                  pl.BlockSpec((sw, sw), lambda i: (0, 0)),
                  pl.BlockSpec((1, sw), lambda i: (0, 0)),
                  pl.BlockSpec(w_out.shape, lambda i: (0, 0))],
        out_specs=pl.BlockSpec((t, d), lambda i: (i, 0)),
        out_shape=jax.ShapeDtypeStruct((m, d), F32),
        compiler_params=_cparams("arbitrary"), name="even_out",
    )(y3, y3, u2, o_gla, x, mod_l, s5_d, glu_w, glu_b, w_out)


LRU_TB = 8
CONV_W = 4
CONV_LEFT = 2


def _lru_kernel(x_ref, cw_ref, cb_ref, wa_ref, ba_ref, wx_ref, bx_ref, lam_ref, h0_ref, hs_ref, *rest,
                t, glen, nseg, segmented):
    if segmented:
        xp_sc, xc_sc, a_sc, b_sc, e_sc, p_sc, s_sc = rest
    else:
        stout_ref, xp_sc, xc_sc, a_sc, b_sc = rest
    r = t * NPS
    rb = LRU_TB * NPS
    pad = CONV_LEFT * NPS
    w = x_ref.shape[1]

    xp_sc[0:pad, :] = jnp.zeros((pad, w), F32)
    xp_sc[pad + r:pad + r + pad, :] = jnp.zeros((pad, w), F32)

    def cp(i, c):
        rows = pl.ds(pl.multiple_of(i * rb, rb), rb)
        xp_sc[pl.ds(pl.multiple_of(pad + i * rb, NPS), rb), :] = x_ref[rows, :]
        return c
    lax.fori_loop(0, r // rb, cp, 0)

    tloc = lax.broadcasted_iota(jnp.int32, (rb, w), 0) // NPS

    def conv(i, c):
        tg = (i * LRU_TB + tloc) % glen
        acc = jnp.zeros((rb, w), F32) + cb_ref[...]
        for kk in range(CONV_W):
            off = kk - CONV_LEFT
            xs = xp_sc[pl.ds(pl.multiple_of(pad + i * rb + off * NPS, NPS), rb), :]
            ok = (tg + off >= 0) & (tg + off < glen)
            acc = acc + jnp.where(ok, xs, 0.0) * cw_ref[kk:kk + 1, :]
        xc_sc[pl.ds(pl.multiple_of(i * rb, rb), rb), :] = acc
        return c
    lax.fori_loop(0, r // rb, conv, 0)

    for d in range(2):
        sp = _softplus(-lam_ref[d])

        def gates(i, c):
            rows = pl.ds(pl.multiple_of(i * rb, rb), rb)
            xc = xc_sc[rows, :]
            xb = xc.astype(BF16)
            rg = jax.nn.sigmoid(_dot(xb, wa_ref[d]) + ba_ref[d])
            ig = jax.nn.sigmoid(_dot(xb, wx_ref[d]) + bx_ref[d])
            la = -LRU_C * rg * sp
            a_sc[rows, :] = jnp.exp(la)
            th = jnp.tanh(la)
            b_sc[rows, :] = jnp.sqrt(-2.0 * th / (1.0 - th)) * (ig * xc)
            return c
        lax.fori_loop(0, r // rb, gates, 0)

        def trow(s):
            tt = s if d == 0 else t - 1 - s
            return pl.ds(pl.multiple_of(tt * NPS, NPS), NPS)

        if segmented:
            def sweep(s, carry):
                h, p = carry
                rows = trow(s)
                a = a_sc[rows, :]
                return a * h + b_sc[rows, :], a * p
            e, p = lax.fori_loop(0, t, sweep, (jnp.zeros((NPS, w), F32), jnp.ones((NPS, w), F32)))
            e_sc[...] = e
            p_sc[...] = p
            nb = NPS // nseg
            for b in range(nb):
                prev = h0_ref[d, b:b + 1, :]
                for k in (range(nseg) if d == 0 else range(nseg - 1, -1, -1)):
                    j = b * nseg + k
                    s_sc[j:j + 1, :] = prev
                    prev = p_sc[j:j + 1, :] * prev + e_sc[j:j + 1, :]
            h_init = s_sc[...]
        else:
            h_init = h0_ref[d]

        def scan(s, h):
            rows = trow(s)
            h = a_sc[rows, :] * h + b_sc[rows, :]
            if d == 0:
                hs_ref[rows, :] = h
            else:
                hs_ref[rows, :] += h
            return h
        h_fin = lax.fori_loop(0, t, scan, h_init)
        if not segmented:
            stout_ref[d] = h_fin


def _lru(x_tm, conv_w, conv_b, wa, ba, wx, bx, lam, h0, *, t, glen, nseg):
    segmented = nseg > 1
    r, w = x_tm.shape
    heads, blk = wa.shape[1], wa.shape[2]
    col = lambda hd: (0, hd)
    col3 = lambda hd: (0, 0, hd)
    out_shape = [jax.ShapeDtypeStruct((r, w), F32)]
    out_specs = [pl.BlockSpec((r, blk), col)]
    scratch = [pltpu.VMEM((r + 2 * CONV_LEFT * NPS, blk), F32)] + [pltpu.VMEM((r, blk), F32)] * 3
    if segmented:
        scratch += [pltpu.VMEM((NPS, blk), F32)] * 3
    else:
        out_shape.append(jax.ShapeDtypeStruct((2, NPS, w), F32))
        out_specs.append(pl.BlockSpec((2, NPS, blk), col3))
    res = pl.pallas_call(
        functools.partial(_lru_kernel, t=t, glen=glen, nseg=nseg, segmented=segmented),
        grid=(heads,),
        in_specs=[pl.BlockSpec((r, blk), col),
                  pl.BlockSpec((CONV_W, blk), col),
                  pl.BlockSpec((1, blk), col),
                  pl.BlockSpec((2, None, blk, blk), lambda hd: (0, hd, 0, 0)),
                  pl.BlockSpec((2, 1, blk), col3),
                  pl.BlockSpec((2, None, blk, blk), lambda hd: (0, hd, 0, 0)),
                  pl.BlockSpec((2, 1, blk), col3),
                  pl.BlockSpec((2, 1, blk), col3),
                  pl.BlockSpec((2, h0.shape[1], blk), col3)],
        out_specs=out_specs, out_shape=out_shape, scratch_shapes=scratch,
        compiler_params=_cparams("arbitrary"), name="lru",
    )(x_tm, conv_w, conv_b, wa, ba, wx, bx, lam, h0)
    return (res[0], res[1]) if not segmented else (res[0], None)


def _odd_out_kernel(hs_ref, gate_ref, x_ref, mod_ref, wo_ref, out_ref):
    y = _dot((hs_ref[...] * _gelu(gate_ref[...])).astype(BF16), wo_ref[...])
    out_ref[...] = x_ref[...] + mod_ref[5:6, :] * y


def _odd_out(hs_tm, gate, x, mod_l, w_out, *, t, row_base, pseq_per_cond):
    m, d = x.shape
    w = hs_tm.shape[1]
    return pl.pallas_call(
        _odd_out_kernel, grid=(NPS,),
        in_specs=[pl.BlockSpec((t, w), lambda i: (0, i)),
                  pl.BlockSpec((t, w), lambda i: (i, 0)),
                  pl.BlockSpec((t, d), lambda i: (i, 0)),
                  pl.BlockSpec((None, N_MOD, d), lambda i: (row_base + i // pseq_per_cond, 0, 0)),
                  pl.BlockSpec(w_out.shape, lambda i: (0, 0))],
        out_specs=pl.BlockSpec((t, d), lambda i: (i, 0)),
        out_shape=jax.ShapeDtypeStruct((m, d), F32),
        compiler_params=_cparams("arbitrary"), name="odd_out",
    )(hs_tm.reshape(t, NPS * w), gate, x, mod_l, w_out)


def kernel(x_prompt, x_sample, state_s5_re, state_s5_im, state_gla, state_lru, c, c_ctx, norm_g, ada_w, ada_b, ffn_w_in, ffn_w_out, final_norm_g, ev_w_in, ev_w_out, s5_lam_re, s5_lam_im, s5_log_step, s5_b_re, s5_b_im, s5_c_re, s5_c_im, s5_d, s5_glu_w, s5_glu_b, gla_gate_w2, gla_gate_b, gla_norm_g, od_w_in, od_w_out, lru_conv_w, lru_conv_b, lru_wa, lru_ba, lru_wx, lru_bx, lru_lam):
    nbc, seq, d = x_prompt.shape
    nbl, dseq, _ = x_sample.shape
    depth = norm_g.shape[0]
    assert nbc == NPS and NPS % nbl == 0
    nseg = NPS // nbl
    tl = dseq // nseg
    grid_w = 64
    assert tl % grid_w == 0 and tl % GLA_CHUNK == 0 and seq % GLA_CHUNK == 0
    sw = s5_d.shape[1]
    qk = gla_gate_w2.shape[-1]
    heads = state_gla.shape[3]
    rank = gla_gate_w2.shape[2]
    g5, p5 = s5_lam_re.shape[2], s5_lam_re.shape[3]
    main = ev_w_in.shape[2] - 2 * rank

    passes = [dict(x=x_prompt.reshape(nbc * seq, d), base=0, rpc=nbc * seq, ppc=NPS, t=seq, nseg=1,
                   glen=seq, nb=nbc, seq=seq),
              dict(x=x_sample.reshape(nbl * dseq, d), base=1, rpc=dseq, ppc=nseg, t=tl, nseg=nseg,
                   glen=grid_w, nb=nbl, seq=dseq)]

    cond8 = jnp.concatenate([c_ctx[None, :], c, jnp.zeros((8 - 1 - nbl, d), F32)], axis=0)
    mod = _ada(cond8, ada_w, ada_b).reshape(depth, 8, N_MOD, d)

    w_in16 = ffn_w_in.astype(BF16)
    w_out16 = ffn_w_out.astype(BF16)
    fg = final_norm_g.reshape(1, d)

    new_s5_re, new_s5_im, new_gla, new_lru = [], [], [], []
    for l in range(depth):
        mod_l = mod[l]
        ng = lambda s: norm_g[l, s].reshape(1, d)
        for ps in passes:
            ps['x'] = _ffn(ps['x'], mod_l, ng(0), fg, w_in16, w_out16, l, 0, mi=0,
                           row_base=ps['base'], rows_per_cond=ps['rpc'], final=False)
        if l % 2 == 0:
            e = l // 2
            w_e = ev_w_in[e]
            w_u = w_e[:, :sw].astype(BF16)
            w_qkvg = w_e[:, sw:main].astype(BF16)
            w_glr = jnp.pad(w_e[:, main:], ((0, 0), (0, 128 - 2 * rank))).astype(BF16)
            w_o = ev_w_out[e].astype(BF16)
            glu_w = s5_glu_w[e].astype(BF16)
            w2p = jnp.stack([jnp.pad(gla_gate_w2[e, dd], ((dd * rank, 128 - (dd + 1) * rank), (0, 0)))
                             for dd in range(2)]).astype(BF16)
            gate_b = gla_gate_b[e].reshape(2, 1, qk)
            for pi, ps in enumerate(passes):
                t = ps['t']
                bw, cw, a_bc, a_t = _s5_params(s5_lam_re[e], s5_lam_im[e], s5_log_step[e], s5_b_re[e],
                                               s5_b_im[e], s5_c_re[e], s5_c_im[e], float(t))
                h = _normmod(ps['x'], mod_l, ng(1), mi=3, row_base=ps['base'], rows_per_cond=ps['rpc'])
                u_tm = _proj_tm(h, w_u, t)
                qkvg = _proj_bm(h, w_qkvg, 1024)
                glr = _proj_bm(h, w_glr, 128)
                if pi == 0:
                    h0 = jnp.zeros((2, NPS, a_bc.shape[-1]), F32)
                    s0 = None
                else:
                    h0 = _s5_state_to_cols(state_s5_re[:, e], state_s5_im[:, e])
                    s0 = state_gla[:, e]
                y_dir, s5_fin = _s5(u_tm, bw, cw, a_bc, a_t, h0, t=t, nseg=ps['nseg'])
                o_gla, gla_fin = _gla(qkvg, glr, w2p, gate_b, gla_norm_g[e].reshape(1, -1), s0,
                                      nb=ps['nb'], seq=ps['seq'], heads=heads, want_state=(pi == 0))
                if pi == 0:
                    sr, si = _s5_cols_to_state(s5_fin, g5, p5)
                    new_s5_re.append(sr)
                    new_s5_im.append(si)
                    new_gla.append(gla_fin)
                ps['x'] = _even_out(y_dir, u_tm, o_gla, ps['x'], mod_l, s5_d[e].reshape(1, sw), glu_w,
                                    s5_glu_b[e].reshape(1, sw), w_o, t=t, row_base=ps['base'],
                                    pseq_per_cond=ps['ppc'])
        else:
            o = l // 2
            lw = od_w_in.shape[2] // 2
            w_gate = od_w_in[o][:, :lw].astype(BF16)
            w_x = od_w_in[o][:, lw:].astype(BF16)
            w_o = od_w_out[o].astype(BF16)
            wa16 = lru_wa[o].astype(BF16)
            wx16 = lru_wx[o].astype(BF16)
            for pi, ps in enumerate(passes):
                t = ps['t']
                h = _normmod(ps['x'], mod_l, ng(1), mi=3, row_base=ps['base'], rows_per_cond=ps['rpc'])
                gate = _proj_bm(h, w_gate, 1024)
                x_tm = _proj_tm(h, w_x, t)
                if pi == 0:
                    h0 = jnp.zeros((2, NPS, lw), F32)
                else:
                    h0 = state_lru[:, o].transpose(1, 0, 2)
                hs, lru_fin = _lru(x_tm, lru_conv_w[o], lru_conv_b[o].reshape(1, lw), wa16,
                                   lru_ba[o].reshape(2, 1, lw), wx16, lru_bx[o].reshape(2, 1, lw),
                                   lru_lam[o].reshape(2, 1, lw), h0, t=t, glen=ps['glen'], nseg=ps['nseg'])
                if pi == 0:
                    new_lru.append(lru_fin.transpose(1, 0, 2))
                ps['x'] = _odd_out(hs, gate, ps['x'], mod_l, w_o, t=t, row_base=ps['base'],
                                   pseq_per_cond=ps['ppc'])
        for ps in passes:
            ps['x'] = _ffn(ps['x'], mod_l, ng(2), fg, w_in16, w_out16, l, 1, mi=6,
                           row_base=ps['base'], rows_per_cond=ps['rpc'], final=(l == depth - 1))

    y_prompt = passes[0]['x'].reshape(nbc, seq, d)
    y_sample = passes[1]['x'].reshape(nbl, dseq, d)
    return (y_prompt, y_sample, jnp.stack(new_s5_re, 1), jnp.stack(new_s5_im, 1),
            jnp.stack(new_gla, 1), jnp.stack(new_lru, 1))
```

```python
import functools
import math

import jax
import jax.numpy as jnp
from jax import lax
from jax.experimental import pallas as pl
from jax.experimental.pallas import tpu as pltpu

F32 = jnp.float32
BF16 = jnp.bfloat16
EPS = 1e-6
NPS = 16
N_MOD = 9
GLA_CHUNK = 64
GLA_TAU = 16.0
LRU_C = 8.0
VMEM_LIMIT = 58 * 1024 * 1024
ROW_CHUNK = 64


def _cparams(*sem):
    return pltpu.CompilerParams(dimension_semantics=sem, vmem_limit_bytes=VMEM_LIMIT)


def _dot(a, b):
    return jnp.dot(a, b, preferred_element_type=F32)


def _dot_nt(a, b):
    return lax.dot_general(a, b, (((1,), (1,)), ((), ())), preferred_element_type=F32)


def _dot_tn(a, b):
    return lax.dot_general(a, b, (((0,), (0,)), ((), ())), preferred_element_type=F32)


def _sigmoid(x):
    return 0.5 * (1.0 + jnp.tanh(0.5 * x))


def _silu(x):
    return x * _sigmoid(x)


def _gelu(x):
    return 0.5 * x * (1.0 + jnp.tanh(math.sqrt(2.0 / math.pi) * (x + 0.044715 * (x * x * x))))


def _softplus(x):
    return jnp.maximum(x, 0.0) + jnp.log1p(jnp.exp(-jnp.abs(x)))


def _rms_mod(x, g, scale, shift):
    y = x * lax.rsqrt(jnp.mean(x * x, axis=-1, keepdims=True) + EPS) * g
    return y * (1.0 + scale) + shift


def _perm_matrix(n_outer, n_inner):
    n = n_outer * n_inner
    ro = lax.broadcasted_iota(jnp.int32, (n, n), 0)
    ci = lax.broadcasted_iota(jnp.int32, (n, n), 1)
    return jnp.where(ci == (ro % n_outer) * n_inner + ro // n_outer, 1.0, 0.0).astype(BF16)


def _permute_rows_f32(perm, x):
    hi = x.astype(BF16)
    r1 = x - hi.astype(F32)
    mid = r1.astype(BF16)
    lo = (r1 - mid.astype(F32)).astype(BF16)
    return _dot(perm, hi) + _dot(perm, mid) + _dot(perm, lo)


def _norm_rows(x_ref, mod_ref, g_ref, h_sc, mi):
    def body(r, c):
        rows = pl.ds(pl.multiple_of(r * ROW_CHUNK, ROW_CHUNK), ROW_CHUNK)
        h = _rms_mod(x_ref[rows, :], g_ref[...], mod_ref[mi + 1:mi + 2, :], mod_ref[mi:mi + 1, :])
        h_sc[rows, :] = h.astype(BF16)
        return c
    lax.fori_loop(0, x_ref.shape[0] // ROW_CHUNK, body, 0)


def _ada_kernel(c_ref, w_ref, b_ref, o_ref):
    ca = _silu(c_ref[...])
    o_ref[...] = _dot(ca.astype(BF16), w_ref[...].astype(BF16)) + b_ref[...]


def _ada(cond8, ada_w, ada_b):
    depth, d, n = ada_w.shape
    tn = 1024
    return pl.pallas_call(
        _ada_kernel, grid=(depth, n // tn),
        in_specs=[pl.BlockSpec((8, d), lambda l, j: (0, 0)),
                  pl.BlockSpec((None, d, tn), lambda l, j: (l, 0, j)),
                  pl.BlockSpec((None, 1, tn), lambda l, j: (l, 0, j))],
        out_specs=pl.BlockSpec((None, 8, tn), lambda l, j: (l, 0, j)),
        out_shape=jax.ShapeDtypeStruct((depth, 8, n), F32),
        compiler_params=_cparams("arbitrary", "arbitrary"), name="ada",
    )(cond8, ada_w, ada_b.reshape(depth, 1, n))


FFN_TM = 1024
FFN_TF = 256
FFN_NC = 512


def _ffn_kernel(x_ref, mod_ref, g_ref, fg_ref, wa_ref, wb_ref, wo_ref, o_ref, h_sc, *, mi, nf, final):
    j = pl.program_id(1)
    tm, d = x_ref.shape

    @pl.when(j == 0)
    def _():
        _norm_rows(x_ref, mod_ref, g_ref, h_sc, mi)
        o_ref[...] = jnp.zeros_like(o_ref)

    h = h_sc[...]
    a = _dot(h, wa_ref[...].astype(BF16))
    b = _dot(h, wb_ref[...].astype(BF16))
    act = (_silu(a) * b).astype(BF16)
    for n in range(d // FFN_NC):
        cols = slice(n * FFN_NC, (n + 1) * FFN_NC)
        o_ref[:, cols] += _dot(act, wo_ref[:, cols].astype(BF16))

    @pl.when(j == nf - 1)
    def _():
        def body(r, c):
            rows = pl.ds(pl.multiple_of(r * ROW_CHUNK, ROW_CHUNK), ROW_CHUNK)
            y = x_ref[rows, :] + 0.5 * mod_ref[mi + 2:mi + 3, :] * o_ref[rows, :]
            if final:
                y = y * lax.rsqrt(jnp.mean(y * y, axis=-1, keepdims=True) + EPS) * fg_ref[...]
            o_ref[rows, :] = y
            return c
        lax.fori_loop(0, tm // ROW_CHUNK, body, 0)


def _ffn(x, mod_l, norm_g, final_g, w_in, w_out, l, k, *, mi, row_base, rows_per_cond, final):
    m, d = x.shape
    f = w_out.shape[2]
    tm = min(FFN_TM, rows_per_cond)
    tf = FFN_TF
    nf = f // tf
    row = lambda i, j: (row_base + (i * tm) // rows_per_cond, 0, 0)
    return pl.pallas_call(
        functools.partial(_ffn_kernel, mi=mi, nf=nf, final=final),
        grid=(m // tm, nf),
        in_specs=[pl.BlockSpec((tm, d), lambda i, j: (i, 0), pipeline_mode=pl.Buffered(1)),
                  pl.BlockSpec((None, N_MOD, d), row),
                  pl.BlockSpec((1, d), lambda i, j: (0, 0)),
                  pl.BlockSpec((1, d), lambda i, j: (0, 0)),
                  pl.BlockSpec((None, None, d, tf), lambda i, j: (l, k, 0, j)),
                  pl.BlockSpec((None, None, d, tf), lambda i, j: (l, k, 0, j + nf)),
                  pl.BlockSpec((None, None, tf, d), lambda i, j: (l, k, j, 0))],
        out_specs=pl.BlockSpec((tm, d), lambda i, j: (i, 0)),
        out_shape=jax.ShapeDtypeStruct((m, d), F32),
        scratch_shapes=[pltpu.VMEM((tm, d), BF16)],
        compiler_params=_cparams("arbitrary", "arbitrary"), name="ffn",
    )(x, mod_l, norm_g, final_g, w_in, w_in, w_out)


PROJ_TM = 1024
PROJ_TN = 1024


def _norm_proj_kernel(x_ref, mod_ref, g_ref, w_ref, *rest, mi, nmain):
    if len(rest) == 4:
        wx_ref, o_ref, ox_ref, h_sc = rest
    else:
        (o_ref, h_sc), wx_ref, ox_ref = rest, None, None
    j = pl.program_id(1)

    @pl.when(j == 0)
    def _():
        _norm_rows(x_ref, mod_ref, g_ref, h_sc, mi)

    @pl.when(j < nmain)
    def _():
        o_ref[...] = _dot(h_sc[...], w_ref[...].astype(BF16))

    if wx_ref is not None:
        @pl.when(j == nmain)
        def _():
            ox_ref[...] = _dot(h_sc[...], wx_ref[...])


def _norm_proj(x, mod_l, norm_g, w, e, w_extra, *, mi, row_base, rows_per_cond):
    m, d = x.shape
    tm = min(PROJ_TM, rows_per_cond)
    tn = PROJ_TN
    nmain = w.shape[2] // tn
    nj = nmain + (0 if w_extra is None else 1)
    jm = lambda j: jnp.minimum(j, nmain - 1)
    in_specs = [pl.BlockSpec((tm, d), lambda i, j: (i, 0)),
                pl.BlockSpec((None, N_MOD, d), lambda i, j: (row_base + (i * tm) // rows_per_cond, 0, 0)),
                pl.BlockSpec((1, d), lambda i, j: (0, 0)),
                pl.BlockSpec((None, d, tn), lambda i, j: (e, 0, jm(j)))]
    args = [x, mod_l, norm_g, w]
    out_shape = [jax.ShapeDtypeStruct((m, nmain * tn), F32)]
    out_specs = [pl.BlockSpec((tm, tn), lambda i, j: (i, jm(j)))]
    if w_extra is not None:
        in_specs.append(pl.BlockSpec(w_extra.shape, lambda i, j: (0, 0)))
        args.append(w_extra)
        out_shape.append(jax.ShapeDtypeStruct((m, w_extra.shape[1]), F32))
        out_specs.append(pl.BlockSpec((tm, w_extra.shape[1]), lambda i, j: (i, 0)))
    res = pl.pallas_call(
        functools.partial(_norm_proj_kernel, mi=mi, nmain=nmain), grid=(m // tm, nj),
        in_specs=in_specs, out_specs=out_specs, out_shape=out_shape,
        scratch_shapes=[pltpu.VMEM((tm, d), BF16)],
        compiler_params=_cparams("arbitrary", "arbitrary"), name="norm_proj",
    )(*args)
    return res if w_extra is not None else (res[0], None)


S5_S = 8
S5_GP = 2
S5_PPG = 4
S5_UNROLL = 4


def _s5_kernel(u_ref, w1_ref, w2_ref, w3_ref, a_ref, at_ref, h0_ref, y_ref, *rest, nblk, nseg, segmented):
    if segmented:
        v_sc, st_sc = rest
    else:
        stout_ref, v_sc, st_sc = rest
    npair = w3_ref.shape[0]
    cw = w3_ref.shape[1]
    hw = cw // 2
    u = u_ref[...]

    for d in range(2):
        for q in range(npair):
            v_sc[d, :, q * cw:(q + 1) * cw] = _dot(u[:, q * cw:(q + 1) * cw], w1_ref[d, q])

        def advance(state, v):
            outs = []
            for q in range(npair):
                re = slice(q * cw, q * cw + hw)
                im = slice(q * cw + hw, (q + 1) * cw)
                ar, ai = a_ref[d, :, re], a_ref[d, :, im]
                hr, hi = state[:, re], state[:, im]
                outs.append(ar * hr - ai * hi + v[:, re])
                outs.append(ar * hi + ai * hr + v[:, im])
            return jnp.concatenate(outs, axis=1)

        def rows(kk):
            k = kk if d == 0 else nblk - 1 - kk
            return pl.ds(pl.multiple_of(k * NPS, NPS), NPS)

        if segmented:
            st_sc[...] = lax.fori_loop(0, nblk, lambda kk, s: advance(s, v_sc[d, rows(kk), :]),
                                       jnp.zeros(st_sc.shape, F32), unroll=S5_UNROLL)
            nb = NPS // nseg
            for b in range(nb):
                for q in range(npair):
                    re = slice(q * cw, q * cw + hw)
                    im = slice(q * cw + hw, (q + 1) * cw)
                    pr, pi = h0_ref[d, b:b + 1, re], h0_ref[d, b:b + 1, im]
                    ar, ai = at_ref[d, :, re], at_ref[d, :, im]
                    for k in (range(nseg) if d == 0 else range(nseg - 1, -1, -1)):
                        j = b * nseg + k
                        er, ei = st_sc[j:j + 1, re], st_sc[j:j + 1, im]
                        st_sc[j:j + 1, re] = pr
                        st_sc[j:j + 1, im] = pi
                        pr, pi = ar * pr - ai * pi + er, ar * pi + ai * pr + ei
            init = st_sc[...]
        else:
            init = h0_ref[d]

        def body(kk, s):
            r = rows(kk)
            v = v_sc[d, r, :]
            v_sc[d, r, :] = s
            return advance(s, v)
        fin = lax.fori_loop(0, nblk, body, init, unroll=S5_UNROLL)
        if not segmented:
            stout_ref[d] = fin

    for q in range(npair):
        cols = slice(q * cw, (q + 1) * cw)
        hcat = jnp.concatenate([v_sc[0, :, cols], v_sc[1, :, cols]], axis=1).astype(BF16)
        y_ref[:, cols] = _dot(hcat, w2_ref[q]) + _dot(u[:, cols], w3_ref[q])


def _s5(u8, w1, w2, w3, a_bc, a_t, h0, *, nblk, nseg):
    segmented = nseg > 1
    r, w = u8.shape
    npair, cw = w3.shape[0], w3.shape[1]
    sl = S5_PPG * cw
    col2 = lambda g: (0, g)
    col3 = lambda g: (0, 0, g)
    out_shape = [jax.ShapeDtypeStruct((r, w), F32)]
    out_specs = [pl.BlockSpec((r, sl), col2)]
    if not segmented:
        out_shape.append(jax.ShapeDtypeStruct((2, NPS, w), F32))
        out_specs.append(pl.BlockSpec((2, NPS, sl), col3))
    res = pl.pallas_call(
        functools.partial(_s5_kernel, nblk=nblk, nseg=nseg, segmented=segmented),
        grid=(npair // S5_PPG,),
        in_specs=[pl.BlockSpec((r, sl), col2),
                  pl.BlockSpec((2, S5_PPG, cw, cw), lambda g: (0, g, 0, 0)),
                  pl.BlockSpec((S5_PPG, 2 * cw, cw), lambda g: (g, 0, 0)),
                  pl.BlockSpec((S5_PPG, cw, cw), lambda g: (g, 0, 0)),
                  pl.BlockSpec((2, NPS, sl), col3),
                  pl.BlockSpec((2, 1, sl), col3),
                  pl.BlockSpec((2, h0.shape[1], sl), col3)],
        out_specs=out_specs, out_shape=out_shape,
        scratch_shapes=[pltpu.VMEM((2, r, sl), F32), pltpu.VMEM((NPS, sl), F32)],
        compiler_params=_cparams("arbitrary"), name="s5",
    )(u8, w1, w2, w3, a_bc, a_t, h0)
    return (res[0], res[1]) if not segmented else (res[0], None)


def _s5_params(lam_re, lam_im, log_step, b_re, b_im, c_re, c_im, t_seg):
    nd, g, p = lam_re.shape
    n = b_re.shape[-1]
    npair = g // S5_GP
    hi = lax.Precision.HIGHEST
    dt = jnp.exp(log_step)[..., None]
    z_re, z_im = lam_re * dt, lam_im * dt
    mag = jnp.exp(z_re)
    ab_re, ab_im = mag * jnp.cos(z_im), mag * jnp.sin(z_im)
    den = lam_re * lam_re + lam_im * lam_im
    n_re = ab_re - 1.0
    f_re = (n_re * lam_re + ab_im * lam_im) / den
    f_im = (ab_im * lam_re - n_re * lam_im) / den
    bb_re = f_re[..., None] * b_re - f_im[..., None] * b_im
    bb_im = f_re[..., None] * b_im + f_im[..., None] * b_re

    def power(tau):
        m = jnp.exp(z_re[..., None] * tau)
        return m * jnp.cos(z_im[..., None] * tau), m * jnp.sin(z_im[..., None] * tau)

    pw_re, pw_im = power(jnp.arange(S5_S + 1, dtype=F32))
    abt_re = pw_re[:, :, :, None, :] * bb_re[..., None] - pw_im[:, :, :, None, :] * bb_im[..., None]
    abt_im = pw_re[:, :, :, None, :] * bb_im[..., None] + pw_im[:, :, :, None, :] * bb_re[..., None]
    eye = jnp.eye(S5_GP, dtype=F32)
    steps = jnp.arange(S5_S)
    pairs = lambda x: x.reshape((npair, S5_GP) + x.shape[1:])

    w1, w2 = [], []
    for d in range(nd):
        e1 = (S5_S - 1 - steps) if d == 0 else steps
        e2 = (steps + 1) if d == 0 else (S5_S - steps)
        sel = lambda x, e: jnp.take(x, e, axis=-1)
        parts = [jnp.einsum('qgpnj,gh->qjgnhp', pairs(sel(x[d], e1)), eye, precision=hi) for x in (abt_re, abt_im)]
        w1.append(jnp.stack(parts, axis=4).reshape(npair, S5_S * S5_GP * n, 2 * S5_GP * p))
        ca_re = c_re[d][..., None] * sel(pw_re[d], e2)[:, None] - c_im[d][..., None] * sel(pw_im[d], e2)[:, None]
        ca_im = c_re[d][..., None] * sel(pw_im[d], e2)[:, None] + c_im[d][..., None] * sel(pw_re[d], e2)[:, None]
        parts = [jnp.einsum('qgnpj,gh->qgpjhn', pairs(x), eye, precision=hi) for x in (ca_re, -ca_im)]
        w2.append(jnp.stack(parts, axis=1).reshape(npair, 2 * S5_GP * p, S5_S * S5_GP * n))
    w1 = jnp.stack(w1).astype(BF16)
    w2 = jnp.concatenate(w2, axis=1).astype(BF16)

    kt = (jnp.einsum('dgnp,dgpmt->dgnmt', c_re, abt_re, precision=hi)
          - jnp.einsum('dgnp,dgpmt->dgnmt', c_im, abt_im, precision=hi))
    ii, jj = steps[:, None], steps[None, :]
    k_fwd = jnp.take(kt[0], jnp.clip(jj - ii, 0, S5_S), axis=-1) * (jj >= ii)
    k_bwd = jnp.take(kt[1], jnp.clip(ii - jj, 0, S5_S), axis=-1) * (ii >= jj)
    w3 = jnp.einsum('qgnmij,gh->qigmjhn', pairs(k_fwd + k_bwd), eye, precision=hi)
    w3 = w3.reshape(npair, S5_S * S5_GP * n, S5_S * S5_GP * n).astype(BF16)

    def cols(re, im):
        return jnp.stack([re.reshape(nd, npair, S5_GP * p), im.reshape(nd, npair, S5_GP * p)], axis=2).reshape(nd, -1)

    a_s = cols(pw_re[..., S5_S], pw_im[..., S5_S])
    a_bc = jnp.broadcast_to(a_s[:, None, :], (nd, NPS, a_s.shape[-1]))
    at_re, at_im = power(t_seg)
    a_t = cols(at_re[..., 0], at_im[..., 0])[:, None, :]
    return w1, w2, w3, a_bc, a_t


def _s5_state_to_cols(s_re, s_im):
    b, nd, g, p = s_re.shape
    npair = g // S5_GP
    st = jnp.stack([s_re.reshape(b, nd, npair, S5_GP * p), s_im.reshape(b, nd, npair, S5_GP * p)], axis=3)
    return st.reshape(b, nd, -1).transpose(1, 0, 2)


def _s5_cols_to_state(st, g, p):
    nd, b, _ = st.shape
    npair = g // S5_GP
    st = st.reshape(nd, b, npair, 2, S5_GP, p).transpose(3, 1, 0, 2, 4, 5).reshape(2, b, nd, g, p)
    return st[0], st[1]


def _s5_blocks(u, nblk, gw):
    npair = u.shape[1] // gw
    x = u.reshape(NPS, nblk, S5_S, npair, gw).transpose(1, 0, 3, 2, 4)
    return x.reshape(nblk * NPS, npair * S5_S * gw)


def _s5_unblocks(y8, nblk, gw):
    npair = y8.shape[1] // (S5_S * gw)
    x = y8.reshape(nblk, NPS, npair, S5_S, gw).transpose(1, 0, 3, 2, 4)
    return x.reshape(NPS * nblk * S5_S, npair * gw)


GLA_RB = 256


def _gla_kernel(q_ref, k_ref, v_ref, g_ref, glr_ref, w2_ref, gb_ref, ng_ref, *rest, seq, zero_init, want_state):
    rest = list(rest)
    s0_ref = None if zero_init else rest.pop(0)
    o_ref = rest.pop(0)
    sfin_ref = rest.pop(0) if want_state else None
    qt_sc, kt_sc, ke_sc, dec_sc, osum_sc = rest
    dk = q_ref.shape[1]
    dv = v_ref.shape[1]
    n = seq // GLA_CHUNK
    scale = dk ** -0.5
    ri = lax.broadcasted_iota(jnp.int32, (GLA_RB, GLA_RB), 0)
    ci = lax.broadcasted_iota(jnp.int32, (GLA_RB, GLA_RB), 1)
    same = (ri // GLA_CHUNK) == (ci // GLA_CHUNK)
    ones_blk = jnp.where(same, 1.0, 0.0).astype(BF16)

    for d in range(2):
        mask = same & ((ci <= ri) if d == 0 else (ci >= ri))
        tri = jnp.where(mask, 1.0, 0.0).astype(BF16)
        for rb in range(seq // GLA_RB):
            rows = slice(rb * GLA_RB, (rb + 1) * GLA_RB)
            x = _dot(glr_ref[rows, :].astype(BF16), w2_ref[d]) + gb_ref[d]
            la = (jnp.minimum(x, 0.0) - jnp.log1p(jnp.exp(-jnp.abs(x)))) * (1.0 / GLA_TAU)
            la_hi = la.astype(BF16)
            la_lo = (la - la_hi.astype(F32)).astype(BF16)
            bc = _dot(tri, la_hi) + _dot(tri, la_lo)
            tot = _dot(ones_blk, la_hi) + _dot(ones_blk, la_lo)
            qf = q_ref[rows, :] * scale
            kf = k_ref[rows, :]
            q_t = (qf * jnp.exp(bc)).astype(BF16)
            k_t = (kf * jnp.exp(-bc)).astype(BF16)
            qt_sc[d, rows, :] = q_t
            kt_sc[d, rows, :] = k_t
            ke_sc[d, rows, :] = (kf * jnp.exp(tot - bc)).astype(BF16)
            dec_sc[d, rows, :] = jnp.exp(tot)
            att = jnp.where(mask, _dot_nt(q_t, k_t), 0.0).astype(BF16)
            o_intra = _dot(att, v_ref[rows, :].astype(BF16))
            if d == 0:
                osum_sc[rows, :] = o_intra
            else:
                osum_sc[rows, :] += o_intra

        st = jnp.zeros((dv, dk), F32) if zero_init else s0_ref[d].T
        for c in (range(n) if d == 0 else range(n - 1, -1, -1)):
            rows = slice(c * GLA_CHUNK, (c + 1) * GLA_CHUNK)
            osum_sc[rows, :] += _dot_nt(qt_sc[d, rows, :], st.astype(BF16))
            kv = _dot_tn(v_ref[rows, :].astype(BF16), ke_sc[d, rows, :])
            st = st * dec_sc[d, c * GLA_CHUNK:c * GLA_CHUNK + 1, :] + kv
        if want_state:
            sfin_ref[d] = st.T

    for rb in range(seq // GLA_RB):
        rows = slice(rb * GLA_RB, (rb + 1) * GLA_RB)
        o = osum_sc[rows, :]
        o = o * lax.rsqrt(jnp.mean(o * o, axis=-1, keepdims=True) + EPS) * ng_ref[...]
        o_ref[rows, :] = (o * _silu(g_ref[rows, :])).astype(BF16)


def _gla(proj, q_col, glr, w2p, gate_b, norm_g, s0, *, nb, seq, heads, want_state):
    qk = w2p.shape[-1]
    dk = qk // heads
    dv = norm_g.shape[1]
    vdim = dv * heads
    zero_init = s0 is None
    in_specs = [pl.BlockSpec((seq, dk), lambda b, h: (b, q_col // dk + h)),
                pl.BlockSpec((seq, dk), lambda b, h: (b, (q_col + qk) // dk + h)),
                pl.BlockSpec((seq, dv), lambda b, h: (b, (q_col + 2 * qk) // dv + h)),
                pl.BlockSpec((seq, dv), lambda b, h: (b, (q_col + 2 * qk + vdim) // dv + h)),
                pl.BlockSpec((seq, glr.shape[1]), lambda b, h: (b, 0)),
                pl.BlockSpec((2, w2p.shape[1], dk), lambda b, h: (0, 0, h)),
                pl.BlockSpec((2, 1, dk), lambda b, h: (0, 0, h)),
                pl.BlockSpec((1, dv), lambda b, h: (0, 0))]
    args = [proj, proj, proj, proj, glr, w2p, gate_b, norm_g]
    if not zero_init:
        in_specs.append(pl.BlockSpec((None, 2, None, dk, dv), lambda b, h: (b, 0, h, 0, 0)))
        args.append(s0)
    out_shape = [jax.ShapeDtypeStruct((nb * seq, vdim), BF16)]
    out_specs = [pl.BlockSpec((seq, dv), lambda b, h: (b, h))]
    if want_state:
        out_shape.append(jax.ShapeDtypeStruct((nb, 2, heads, dk, dv), F32))
        out_specs.append(pl.BlockSpec((None, 2, None, dk, dv), lambda b, h: (b, 0, h, 0, 0)))
    res = pl.pallas_call(
        functools.partial(_gla_kernel, seq=seq, zero_init=zero_init, want_state=want_state),
        grid=(nb, heads), in_specs=in_specs, out_specs=out_specs, out_shape=out_shape,
        scratch_shapes=[pltpu.VMEM((2, seq, dk), BF16), pltpu.VMEM((2, seq, dk), BF16),
                        pltpu.VMEM((2, seq, dk), BF16), pltpu.VMEM((2, seq, dk), F32),
                        pltpu.VMEM((seq, dv), F32)],
        compiler_params=_cparams("arbitrary", "arbitrary"), name="gla",
    )(*args)
    return (res[0], res[1]) if want_state else (res[0], None)


def _even_out_kernel(y_ref, u_ref, o_ref, x_ref, mod_ref, sd_ref, gw_ref, gb_ref, wo_ref, out_ref):
    sw = u_ref.shape[1]
    ys = _gelu(y_ref[...] + sd_ref[...] * u_ref[...])
    ys = ys * _sigmoid(_dot(ys.astype(BF16), gw_ref[...]) + gb_ref[...])
    y = _dot(ys.astype(BF16), wo_ref[:sw, :]) + _dot(o_ref[...], wo_ref[sw:, :])
    out_ref[...] = x_ref[...] + mod_ref[5:6, :] * y


def _even_out(y_s5, proj, o_gla, x, mod_l, s5_d, glu_w, glu_b, w_out, *, t, row_base, pseq_per_cond):
    m, d = x.shape
    sw = y_s5.shape[1]
    return pl.pallas_call(
        _even_out_kernel, grid=(NPS,),
        in_specs=[pl.BlockSpec((t, sw), lambda i: (i, 0)),
                  pl.BlockSpec((t, sw), lambda i: (i, 0)),
                  pl.BlockSpec((t, o_gla.shape[1]), lambda i: (i, 0)),
                  pl.BlockSpec((t, d), lambda i: (i, 0)),
                  pl.BlockSpec((None, N_MOD, d), lambda i: (row_base + i // pseq_per_cond, 0, 0)),
                  pl.BlockSpec((1, sw), lambda i: (0, 0)),
                  pl.BlockSpec((sw, sw), lambda i: (0, 0)),
                  pl.BlockSpec((1, sw), lambda i: (0, 0)),
                  pl.BlockSpec(w_out.shape, lambda i: (0, 0))],
        out_specs=pl.BlockSpec((t, d), lambda i: (i, 0)),
        out_shape=jax.ShapeDtypeStruct((m, d), F32),
        compiler_params=_cparams("arbitrary"), name="even_out",
    )(y_s5, proj, o_gla, x, mod_l, s5_d, glu_w, glu_b, w_out)


LRU_TB = 16
CONV_W = 4
CONV_LEFT = 2
LRU_UNROLL = 8


def _lru_kernel(x_ref, cw_ref, cb_ref, wa_ref, ba_ref, wx_ref, bx_ref, lam_ref, h0_ref, hs_ref, *rest,
                t, glen, nseg, segmented):
    if segmented:
        xp_sc, xc_sc, a_sc, b_sc, hs_sc, e_sc, p_sc, s_sc = rest
    else:
        stout_ref, xp_sc, xc_sc, a_sc, b_sc, hs_sc = rest
    r, w = x_ref.shape
    rb = LRU_TB * NPS
    pad = CONV_LEFT * NPS
    perm = _perm_matrix(NPS, LRU_TB)
    perm_back = _perm_matrix(LRU_TB, NPS)

    xp_sc[0:pad, :] = jnp.zeros((pad, w), F32)
    xp_sc[pad + r:pad + r + pad, :] = jnp.zeros((pad, w), F32)

    def load_tm(i, c):
        xin = jnp.concatenate([x_ref[pl.ds(pl.multiple_of(b * t + i * LRU_TB, LRU_TB), LRU_TB), :]
                               for b in range(NPS)], axis=0)
        xp_sc[pl.ds(pl.multiple_of(pad + i * rb, NPS), rb), :] = _permute_rows_f32(perm, xin)
        return c
    lax.fori_loop(0, t // LRU_TB, load_tm, 0, unroll=2)

    tloc = lax.broadcasted_iota(jnp.int32, (rb, w), 0) // NPS

    def conv(i, c):
        tg = (i * LRU_TB + tloc) % glen
        acc = jnp.zeros((rb, w), F32) + cb_ref[...]
        for kk in range(CONV_W):
            off = kk - CONV_LEFT
            xs = xp_sc[pl.ds(pl.multiple_of(pad + i * rb + off * NPS, NPS), rb), :]
            ok = (tg + off >= 0) & (tg + off < glen)
            acc = acc + jnp.where(ok, xs, 0.0) * cw_ref[kk:kk + 1, :]
        xc_sc[pl.ds(pl.multiple_of(i * rb, rb), rb), :] = acc
        return c
    lax.fori_loop(0, r // rb, conv, 0)

    for d in range(2):
        c2 = (-0.25 * LRU_C) * _softplus(-lam_ref[d])

        def gates(i, c):
            rows = pl.ds(pl.multiple_of(i * rb, rb), rb)
            xc = xc_sc[rows, :]
            xb = xc.astype(BF16)
            tr = jnp.tanh(0.5 * (_dot(xb, wa_ref[d]) + ba_ref[d]))
            ti = jnp.tanh(0.5 * (_dot(xb, wx_ref[d]) + bx_ref[d]))
            th = jnp.tanh(c2 + c2 * tr)
            rcp = 1.0 / (1.0 - th)
            a_sc[rows, :] = (1.0 + th) * rcp
            b_sc[rows, :] = rcp * jnp.sqrt(-th) * ((1.0 + ti) * xc)
            return c
        lax.fori_loop(0, r // rb, gates, 0, unroll=2)

        def trow(s):
            return pl.ds(pl.multiple_of((s if d == 0 else t - 1 - s) * NPS, NPS), NPS)

        if segmented:
            def sweep(s, carry):
                h, p = carry
                rows = trow(s)
                a = a_sc[rows, :]
                return a * h + b_sc[rows, :], a * p
            e, p = lax.fori_loop(0, t, sweep, (jnp.zeros((NPS, w), F32), jnp.ones((NPS, w), F32)),
                                 unroll=LRU_UNROLL)
            e_sc[...] = e
            p_sc[...] = p
            nb = NPS // nseg
            for b in range(nb):
                prev = h0_ref[d, b:b + 1, :]
                for k in (range(nseg) if d == 0 else range(nseg - 1, -1, -1)):
                    j = b * nseg + k
                    s_sc[j:j + 1, :] = prev
                    prev = p_sc[j:j + 1, :] * prev + e_sc[j:j + 1, :]
            h_init = s_sc[...]
        else:
            h_init = h0_ref[d]

        def scan(s, h):
            rows = trow(s)
            h = a_sc[rows, :] * h + b_sc[rows, :]
            if d == 0:
                hs_sc[rows, :] = h
            else:
                hs_sc[rows, :] += h
            return h
        h_fin = lax.fori_loop(0, t, scan, h_init, unroll=LRU_UNROLL)
        if not segmented:
            stout_ref[d] = h_fin

    def store_bm(i, c):
        hb = _permute_rows_f32(perm_back, hs_sc[pl.ds(pl.multiple_of(i * rb, rb), rb), :])
        for b in range(NPS):
            hs_ref[pl.ds(pl.multiple_of(b * t + i * LRU_TB, LRU_TB), LRU_TB), :] = hb[b * LRU_TB:(b + 1) * LRU_TB]
        return c
    lax.fori_loop(0, t // LRU_TB, store_bm, 0, unroll=2)


def _lru(proj, x_col, width, conv_w, conv_b, wa, ba, wx, bx, lam, h0, *, t, glen, nseg):
    segmented = nseg > 1
    r = proj.shape[0]
    heads, blk = wa.shape[1], wa.shape[2]
    col = lambda hd: (0, hd)
    col3 = lambda hd: (0, 0, hd)
    out_shape = [jax.ShapeDtypeStruct((r, width), F32)]
    out_specs = [pl.BlockSpec((r, blk), col)]
    scratch = [pltpu.VMEM((r + 2 * CONV_LEFT * NPS, blk), F32)] + [pltpu.VMEM((r, blk), F32)] * 4
    if segmented:
        scratch += [pltpu.VMEM((NPS, blk), F32)] * 3
    else:
        out_shape.append(jax.ShapeDtypeStruct((2, NPS, width), F32))
        out_specs.append(pl.BlockSpec((2, NPS, blk), col3))
    res = pl.pallas_call(
        functools.partial(_lru_kernel, t=t, glen=glen, nseg=nseg, segmented=segmented),
        grid=(heads,),
        in_specs=[pl.BlockSpec((r, blk), lambda hd: (0, x_col // blk + hd)),
                  pl.BlockSpec((CONV_W, blk), col),
                  pl.BlockSpec((1, blk), col),
                  pl.BlockSpec((2, None, blk, blk), lambda hd: (0, hd, 0, 0)),
                  pl.BlockSpec((2, 1, blk), col3),
                  pl.BlockSpec((2, None, blk, blk), lambda hd: (0, hd, 0, 0)),
                  pl.BlockSpec((2, 1, blk), col3),
                  pl.BlockSpec((2, 1, blk), col3),
                  pl.BlockSpec((2, h0.shape[1], blk), col3)],
        out_specs=out_specs, out_shape=out_shape, scratch_shapes=scratch,
        compiler_params=_cparams("arbitrary"), name="lru",
    )(proj, conv_w, conv_b, wa, ba, wx, bx, lam, h0)
    return (res[0], res[1]) if not segmented else (res[0], None)


def _odd_out_kernel(hs_ref, gate_ref, x_ref, mod_ref, wo_ref, out_ref):
    y = _dot((hs_ref[...] * _gelu(gate_ref[...])).astype(BF16), wo_ref[...])
    out_ref[...] = x_ref[...] + mod_ref[5:6, :] * y


def _odd_out(hs, proj, x, mod_l, w_out, *, t, row_base, pseq_per_cond):
    m, d = x.shape
    w = hs.shape[1]
    return pl.pallas_call(
        _odd_out_kernel, grid=(NPS,),
        in_specs=[pl.BlockSpec((t, w), lambda i: (i, 0)),
                  pl.BlockSpec((t, w), lambda i: (i, 0)),
                  pl.BlockSpec((t, d), lambda i: (i, 0)),
                  pl.BlockSpec((None, N_MOD, d), lambda i: (row_base + i // pseq_per_cond, 0, 0)),
                  pl.BlockSpec(w_out.shape, lambda i: (0, 0))],
        out_specs=pl.BlockSpec((t, d), lambda i: (i, 0)),
        out_shape=jax.ShapeDtypeStruct((m, d), F32),
        compiler_params=_cparams("arbitrary"), name="odd_out",
    )(hs, proj, x, mod_l, w_out)


def kernel(x_prompt, x_sample, state_s5_re, state_s5_im, state_gla, state_lru, c, c_ctx, norm_g, ada_w, ada_b, ffn_w_in, ffn_w_out, final_norm_g, ev_w_in, ev_w_out, s5_lam_re, s5_lam_im, s5_log_step, s5_b_re, s5_b_im, s5_c_re, s5_c_im, s5_d, s5_glu_w, s5_glu_b, gla_gate_w2, gla_gate_b, gla_norm_g, od_w_in, od_w_out, lru_conv_w, lru_conv_b, lru_wa, lru_ba, lru_wx, lru_bx, lru_lam):
    nbc, seq, d = x_prompt.shape
    nbl, dseq, _ = x_sample.shape
    depth = norm_g.shape[0]
    assert nbc == NPS and NPS % nbl == 0
    nseg = NPS // nbl
    tl = dseq // nseg
    grid_w = 64
    assert tl % grid_w == 0 and tl % LRU_TB == 0 and seq % LRU_TB == 0 and seq % GLA_RB == 0 and dseq % GLA_RB == 0
    sw = s5_d.shape[1]
    qk = gla_gate_w2.shape[-1]
    heads = state_gla.shape[3]
    rank = gla_gate_w2.shape[2]
    g5, p5 = s5_lam_re.shape[2], s5_lam_re.shape[3]
    gw5 = S5_GP * (sw // g5)
    main = ev_w_in.shape[2] - 2 * rank
    assert main % PROJ_TN == 0 and (g5 // S5_GP) % S5_PPG == 0

    passes = [dict(x=x_prompt.reshape(nbc * seq, d), base=0, rpc=nbc * seq, ppc=NPS, t=seq, nseg=1,
                   glen=seq, nb=nbc, seq=seq),
              dict(x=x_sample.reshape(nbl * dseq, d), base=1, rpc=dseq, ppc=nseg, t=tl, nseg=nseg,
                   glen=grid_w, nb=nbl, seq=dseq)]

    cond8 = jnp.concatenate([c_ctx[None, :], c, jnp.zeros((8 - 1 - nbl, d), F32)], axis=0)
    mod = _ada(cond8, ada_w, ada_b).reshape(depth, 8, N_MOD, d)
    fg = final_norm_g.reshape(1, d)

    new_s5_re, new_s5_im, new_gla, new_lru = [], [], [], []
    for l in range(depth):
        mod_l = mod[l]
        ng = lambda s: norm_g[l, s].reshape(1, d)
        for ps in passes:
            ps['x'] = _ffn(ps['x'], mod_l, ng(0), fg, ffn_w_in, ffn_w_out, l, 0, mi=0,
                           row_base=ps['base'], rows_per_cond=ps['rpc'], final=False)
        if l % 2 == 0:
            e = l // 2
            w_glr = jnp.pad(ev_w_in[e][:, main:], ((0, 0), (0, 128 - 2 * rank))).astype(BF16)
            w_o = ev_w_out[e].astype(BF16)
            glu_w = s5_glu_w[e].astype(BF16)
            w2p = jnp.stack([jnp.pad(gla_gate_w2[e, dd], ((dd * rank, 128 - (dd + 1) * rank), (0, 0)))
                             for dd in range(2)]).astype(BF16)
            gate_b = gla_gate_b[e].reshape(2, 1, qk)
            for pi, ps in enumerate(passes):
                t = ps['t']
                nblk = t // S5_S
                w1, w2, w3, a_bc, a_t = _s5_params(s5_lam_re[e], s5_lam_im[e], s5_log_step[e], s5_b_re[e],
                                                   s5_b_im[e], s5_c_re[e], s5_c_im[e], float(t))
                proj, glr = _norm_proj(ps['x'], mod_l, ng(1), ev_w_in, e, w_glr, mi=3,
                                       row_base=ps['base'], rows_per_cond=ps['rpc'])
                if pi == 0:
                    h0 = jnp.zeros((2, NPS, a_bc.shape[-1]), F32)
                    s0 = None
                else:
                    h0 = _s5_state_to_cols(state_s5_re[:, e], state_s5_im[:, e])
                    s0 = state_gla[:, e]
                u8 = _s5_blocks(proj[:, :sw], nblk, gw5).astype(BF16)
                y8, s5_fin = _s5(u8, w1, w2, w3, a_bc, a_t, h0, nblk=nblk, nseg=ps['nseg'])
                y_s5 = _s5_unblocks(y8, nblk, gw5)
                o_gla, gla_fin = _gla(proj, sw, glr, w2p, gate_b, gla_norm_g[e].reshape(1, -1), s0,
                                      nb=ps['nb'], seq=ps['seq'], heads=heads, want_state=(pi == 0))
                if pi == 0:
                    sr, si = _s5_cols_to_state(s5_fin, g5, p5)
                    new_s5_re.append(sr)
                    new_s5_im.append(si)
                    new_gla.append(gla_fin)
                ps['x'] = _even_out(y_s5, proj, o_gla, ps['x'], mod_l, s5_d[e].reshape(1, sw), glu_w,
                                    s5_glu_b[e].reshape(1, sw), w_o, t=t, row_base=ps['base'],
                                    pseq_per_cond=ps['ppc'])
        else:
            o = l // 2
            lw = od_w_in.shape[2] // 2
            w_o = od_w_out[o].astype(BF16)
            wa16 = lru_wa[o].astype(BF16)
            wx16 = lru_wx[o].astype(BF16)
            for pi, ps in enumerate(passes):
                t = ps['t']
                proj, _ = _norm_proj(ps['x'], mod_l, ng(1), od_w_in, o, None, mi=3,
                                     row_base=ps['base'], rows_per_cond=ps['rpc'])
                if pi == 0:
                    h0 = jnp.zeros((2, NPS, lw), F32)
                else:
                    h0 = state_lru[:, o].transpose(1, 0, 2)
                hs, lru_fin = _lru(proj, lw, lw, lru_conv_w[o], lru_conv_b[o].reshape(1, lw), wa16,
                                   lru_ba[o].reshape(2, 1, lw), wx16, lru_bx[o].reshape(2, 1, lw),
                                   lru_lam[o].reshape(2, 1, lw), h0, t=t, glen=ps['glen'], nseg=ps['nseg'])
                if pi == 0:
                    new_lru.append(lru_fin.transpose(1, 0, 2))
                ps['x'] = _odd_out(hs, proj, ps['x'], mod_l, w_o, t=t, row_base=ps['base'],
                                   pseq_per_cond=ps['ppc'])
        for ps in passes:
            ps['x'] = _ffn(ps['x'], mod_l, ng(2), fg, ffn_w_in, ffn_w_out, l, 1, mi=6,
                           row_base=ps['base'], rows_per_cond=ps['rpc'], final=(l == depth - 1))

    y_prompt = passes[0]['x'].reshape(nbc, seq, d)
    y_sample = passes[1]['x'].reshape(nbl, dseq, d)
    return (y_prompt, y_sample, jnp.stack(new_s5_re, 1), jnp.stack(new_s5_im, 1),
            jnp.stack(new_gla, 1), jnp.stack(new_lru, 1))
```

```python
import functools
import math

import jax
import jax.numpy as jnp
from jax import lax
from jax.experimental import pallas as pl
from jax.experimental.pallas import tpu as pltpu

F32 = jnp.float32
BF16 = jnp.bfloat16
EPS = 1e-6
NPS = 16
N_MOD = 9
GLA_CHUNK = 64
GLA_TAU = 16.0
LRU_C = 8.0
VMEM_LIMIT = 58 * 1024 * 1024
ROW_CHUNK = 64


def _cparams(*sem):
    return pltpu.CompilerParams(dimension_semantics=sem, vmem_limit_bytes=VMEM_LIMIT)


def _dot(a, b):
    return jnp.dot(a, b, preferred_element_type=F32)


def _dot_nt(a, b):
    return lax.dot_general(a, b, (((1,), (1,)), ((), ())), preferred_element_type=F32)


def _dot_tn(a, b):
    return lax.dot_general(a, b, (((0,), (0,)), ((), ())), preferred_element_type=F32)


def _sigmoid(x):
    return 0.5 * (1.0 + jnp.tanh(0.5 * x))


def _silu(x):
    return x * _sigmoid(x)


def _gelu(x):
    return 0.5 * x * (1.0 + jnp.tanh(math.sqrt(2.0 / math.pi) * (x + 0.044715 * (x * x * x))))


def _softplus(x):
    return jnp.maximum(x, 0.0) + jnp.log1p(jnp.exp(-jnp.abs(x)))


def _rms_mod(x, g, scale, shift):
    y = x * lax.rsqrt(jnp.mean(x * x, axis=-1, keepdims=True) + EPS) * g
    return y * (1.0 + scale) + shift


def _perm_matrix(n_outer, n_inner):
    n = n_outer * n_inner
    ro = lax.broadcasted_iota(jnp.int32, (n, n), 0)
    ci = lax.broadcasted_iota(jnp.int32, (n, n), 1)
    return jnp.where(ci == (ro % n_outer) * n_inner + ro // n_outer, 1.0, 0.0).astype(BF16)


def _permute_rows_f32(perm, x):
    hi = x.astype(BF16)
    r1 = x - hi.astype(F32)
    mid = r1.astype(BF16)
    lo = (r1 - mid.astype(F32)).astype(BF16)
    return _dot(perm, hi) + _dot(perm, mid) + _dot(perm, lo)


def _norm_rows(x_ref, mod_ref, g_ref, h_sc, mi):
    def body(r, c):
        rows = pl.ds(pl.multiple_of(r * ROW_CHUNK, ROW_CHUNK), ROW_CHUNK)
        h = _rms_mod(x_ref[rows, :], g_ref[...], mod_ref[mi + 1:mi + 2, :], mod_ref[mi:mi + 1, :])
        h_sc[rows, :] = h.astype(BF16)
        return c
    lax.fori_loop(0, x_ref.shape[0] // ROW_CHUNK, body, 0)


def _ada_kernel(c_ref, w_ref, b_ref, o_ref):
    ca = _silu(c_ref[...])
    o_ref[...] = _dot(ca.astype(BF16), w_ref[...].astype(BF16)) + b_ref[...]


def _ada(cond8, ada_w, ada_b):
    depth, d, n = ada_w.shape
    tn = 1024
    return pl.pallas_call(
        _ada_kernel, grid=(depth, n // tn),
        in_specs=[pl.BlockSpec((8, d), lambda l, j: (0, 0)),
                  pl.BlockSpec((None, d, tn), lambda l, j: (l, 0, j)),
                  pl.BlockSpec((None, 1, tn), lambda l, j: (l, 0, j))],
        out_specs=pl.BlockSpec((None, 8, tn), lambda l, j: (l, 0, j)),
        out_shape=jax.ShapeDtypeStruct((depth, 8, n), F32),
        compiler_params=_cparams("arbitrary", "arbitrary"), name="ada",
    )(cond8, ada_w, ada_b.reshape(depth, 1, n))


FFN_TM = 1024
FFN_TF = 256
FFN_NC = 512


def _ffn_kernel(x_ref, mod_ref, g_ref, fg_ref, wa_ref, wb_ref, wo_ref, o_ref, h_sc, *, mi, nf, final):
    j = pl.program_id(1)
    tm, d = x_ref.shape

    @pl.when(j == 0)
    def _():
        _norm_rows(x_ref, mod_ref, g_ref, h_sc, mi)
        o_ref[...] = jnp.zeros_like(o_ref)

    h = h_sc[...]
    a = _dot(h, wa_ref[...].astype(BF16))
    b = _dot(h, wb_ref[...].astype(BF16))
    act = (_silu(a) * b).astype(BF16)
    for n in range(d // FFN_NC):
        cols = slice(n * FFN_NC, (n + 1) * FFN_NC)
        o_ref[:, cols] += _dot(act, wo_ref[:, cols].astype(BF16))

    @pl.when(j == nf - 1)
    def _():
        def body(r, c):
            rows = pl.ds(pl.multiple_of(r * ROW_CHUNK, ROW_CHUNK), ROW_CHUNK)
            y = x_ref[rows, :] + 0.5 * mod_ref[mi + 2:mi + 3, :] * o_ref[rows, :]
            if final:
                y = y * lax.rsqrt(jnp.mean(y * y, axis=-1, keepdims=True) + EPS) * fg_ref[...]
            o_ref[rows, :] = y
            return c
        lax.fori_loop(0, tm // ROW_CHUNK, body, 0)


def _ffn(x, mod_l, norm_g, final_g, w_in, w_out, l, k, *, mi, row_base, rows_per_cond, final):
    m, d = x.shape
    f = w_out.shape[2]
    tm = min(FFN_TM, rows_per_cond)
    tf = FFN_TF
    nf = f // tf
    row = lambda i, j: (row_base + (i * tm) // rows_per_cond, 0, 0)
    return pl.pallas_call(
        functools.partial(_ffn_kernel, mi=mi, nf=nf, final=final),
        grid=(m // tm, nf),
        in_specs=[pl.BlockSpec((tm, d), lambda i, j: (i, 0), pipeline_mode=pl.Buffered(1)),
                  pl.BlockSpec((None, N_MOD, d), row),
                  pl.BlockSpec((1, d), lambda i, j: (0, 0)),
                  pl.BlockSpec((1, d), lambda i, j: (0, 0)),
                  pl.BlockSpec((None, None, d, tf), lambda i, j: (l, k, 0, j)),
                  pl.BlockSpec((None, None, d, tf), lambda i, j: (l, k, 0, j + nf)),
                  pl.BlockSpec((None, None, tf, d), lambda i, j: (l, k, j, 0))],
        out_specs=pl.BlockSpec((tm, d), lambda i, j: (i, 0)),
        out_shape=jax.ShapeDtypeStruct((m, d), F32),
        scratch_shapes=[pltpu.VMEM((tm, d), BF16)],
        compiler_params=_cparams("arbitrary", "arbitrary"), name="ffn",
    )(x, mod_l, norm_g, final_g, w_in, w_in, w_out)


PROJ_TM = 1024
PROJ_TN = 1024


def _norm_proj_kernel(x_ref, mod_ref, g_ref, w_ref, *rest, mi, nmain):
    if len(rest) == 4:
        wx_ref, o_ref, ox_ref, h_sc = rest
    else:
        (o_ref, h_sc), wx_ref, ox_ref = rest, None, None
    j = pl.program_id(1)

    @pl.when(j == 0)
    def _():
        _norm_rows(x_ref, mod_ref, g_ref, h_sc, mi)

    @pl.when(j < nmain)
    def _():
        o_ref[...] = _dot(h_sc[...], w_ref[...].astype(BF16))

    if wx_ref is not None:
        @pl.when(j == nmain)
        def _():
            ox_ref[...] = _dot(h_sc[...], wx_ref[...])


def _norm_proj(x, mod_l, norm_g, w, e, w_extra, *, mi, row_base, rows_per_cond):
    m, d = x.shape
    tm = min(PROJ_TM, rows_per_cond)
    tn = PROJ_TN
    nmain = w.shape[2] // tn
    nj = nmain + (0 if w_extra is None else 1)
    jm = lambda j: jnp.minimum(j, nmain - 1)
    in_specs = [pl.BlockSpec((tm, d), lambda i, j: (i, 0)),
                pl.BlockSpec((None, N_MOD, d), lambda i, j: (row_base + (i * tm) // rows_per_cond, 0, 0)),
                pl.BlockSpec((1, d), lambda i, j: (0, 0)),
                pl.BlockSpec((None, d, tn), lambda i, j: (e, 0, jm(j)))]
    args = [x, mod_l, norm_g, w]
    out_shape = [jax.ShapeDtypeStruct((m, nmain * tn), F32)]
    out_specs = [pl.BlockSpec((tm, tn), lambda i, j: (i, jm(j)))]
    if w_extra is not None:
        in_specs.append(pl.BlockSpec(w_extra.shape, lambda i, j: (0, 0)))
        args.append(w_extra)
        out_shape.append(jax.ShapeDtypeStruct((m, w_extra.shape[1]), F32))
        out_specs.append(pl.BlockSpec((tm, w_extra.shape[1]), lambda i, j: (i, 0)))
    res = pl.pallas_call(
        functools.partial(_norm_proj_kernel, mi=mi, nmain=nmain), grid=(m // tm, nj),
        in_specs=in_specs, out_specs=out_specs, out_shape=out_shape,
        scratch_shapes=[pltpu.VMEM((tm, d), BF16)],
        compiler_params=_cparams("arbitrary", "arbitrary"), name="norm_proj",
    )(*args)
    return res if w_extra is not None else (res[0], None)


S5_S = 8
S5_GP = 2
S5_PPG = 4
S5_UNROLL = 4


def _s5_kernel(u_ref, pc_ref, w1_ref, w2t_ref, kx_ref, a_ref, at_ref, h0_ref, y_ref, *rest, t, nseg, segmented):
    if segmented:
        g_sc, v_sc, st_sc, w3_sc = rest
    else:
        stout_ref, g_sc, v_sc, st_sc, w3_sc = rest
    nblk = t // S5_S
    npair = w1_ref.shape[1]
    cw = w1_ref.shape[2]
    hw = cw // 2
    lt = u_ref.shape[1]
    sw = cw // S5_S

    jcol = lax.broadcasted_iota(jnp.int32, (sw, cw), 1) // sw
    for q in range(npair):
        for i in range(S5_S):
            acc = jnp.zeros((sw, cw), F32)
            for lag in range(S5_S - i):
                acc = acc + jnp.where(jcol == i + lag, kx_ref[0, lag, q], 0.0)
            for lag in range(i + 1):
                acc = acc + jnp.where(jcol == i - lag, kx_ref[1, lag, q], 0.0)
            w3_sc[q, i * sw:(i + 1) * sw, :] = acc.astype(BF16)

    def gather(k, c):
        r = pl.ds(pl.multiple_of(k * NPS, NPS), NPS)
        for j in range(S5_S):
            g_sc[r, j * lt:(j + 1) * lt] = u_ref[pl.ds(k * S5_S + j, NPS, stride=t), :]
        return c
    lax.fori_loop(0, nblk, gather, 0)
    u = _dot(g_sc[...].astype(BF16), pc_ref[...]).astype(BF16)

    for d in range(2):
        for q in range(npair):
            v_sc[d, :, q * cw:(q + 1) * cw] = _dot(u[:, q * cw:(q + 1) * cw], w1_ref[d, q])

        def advance(state, v):
            outs = []
            for q in range(npair):
                re = slice(q * cw, q * cw + hw)
                im = slice(q * cw + hw, (q + 1) * cw)
                ar, ai = a_ref[d, :, re], a_ref[d, :, im]
                hr, hi = state[:, re], state[:, im]
                outs.append(ar * hr - ai * hi + v[:, re])
                outs.append(ar * hi + ai * hr + v[:, im])
            return jnp.concatenate(outs, axis=1)

        def rows(kk):
            k = kk if d == 0 else nblk - 1 - kk
            return pl.ds(pl.multiple_of(k * NPS, NPS), NPS)

        if segmented:
            st_sc[...] = lax.fori_loop(0, nblk, lambda kk, s: advance(s, v_sc[d, rows(kk), :]),
                                       jnp.zeros(st_sc.shape, F32), unroll=S5_UNROLL)
            nb = NPS // nseg
            for b in range(nb):
                for q in range(npair):
                    re = slice(q * cw, q * cw + hw)
                    im = slice(q * cw + hw, (q + 1) * cw)
                    pr, pi = h0_ref[d, b:b + 1, re], h0_ref[d, b:b + 1, im]
                    ar, ai = at_ref[d, :, re], at_ref[d, :, im]
                    for k in (range(nseg) if d == 0 else range(nseg - 1, -1, -1)):
                        j = b * nseg + k
                        er, ei = st_sc[j:j + 1, re], st_sc[j:j + 1, im]
                        st_sc[j:j + 1, re] = pr
                        st_sc[j:j + 1, im] = pi
                        pr, pi = ar * pr - ai * pi + er, ar * pi + ai * pr + ei
            init = st_sc[...]
        else:
            init = h0_ref[d]

        def body(kk, s):
            r = rows(kk)
            v = v_sc[d, r, :]
            v_sc[d, r, :] = s
            return advance(s, v)
        fin = lax.fori_loop(0, nblk, body, init, unroll=S5_UNROLL)
        if not segmented:
            stout_ref[d] = fin

    for q in range(npair):
        cols = slice(q * cw, (q + 1) * cw)
        hcat = jnp.concatenate([v_sc[0, :, cols], v_sc[1, :, cols]], axis=1).astype(BF16)
        g_sc[:, cols] = _dot_nt(hcat, w2t_ref[q]) + _dot(u[:, cols], w3_sc[q])

    y = g_sc[...]
    hi = y.astype(BF16)
    r1 = y - hi.astype(F32)
    mid = r1.astype(BF16)
    lo = (r1 - mid.astype(F32)).astype(BF16)
    g_sc[...] = _dot_nt(hi, pc_ref[...]) + _dot_nt(mid, pc_ref[...]) + _dot_nt(lo, pc_ref[...])

    def scatter(k, c):
        r = pl.ds(pl.multiple_of(k * NPS, NPS), NPS)
        for j in range(S5_S):
            y_ref[pl.ds(k * S5_S + j, NPS, stride=t), :] = g_sc[r, j * lt:(j + 1) * lt]
        return c
    lax.fori_loop(0, nblk, scatter, 0)


def _s5_col_perm(lt, gw):
    r = jnp.arange(S5_S * lt)
    j, lane = r // lt, r % lt
    dst = (lane // gw) * (S5_S * gw) + j * gw + lane % gw
    return (dst[:, None] == jnp.arange(S5_S * lt)[None, :]).astype(BF16)


def _s5(proj, width, w1, w2t, kx, a_bc, a_t, h0, *, t, nseg):
    segmented = nseg > 1
    m = proj.shape[0]
    npair, cw = w1.shape[1], w1.shape[2]
    sl = S5_PPG * cw
    lt = sl // S5_S
    r = (t // S5_S) * NPS
    pc = _s5_col_perm(lt, cw // S5_S)
    col2 = lambda g: (0, g)
    col3 = lambda g: (0, 0, g)
    out_shape = [jax.ShapeDtypeStruct((m, width), F32)]
    out_specs = [pl.BlockSpec((m, lt), col2)]
    if not segmented:
        out_shape.append(jax.ShapeDtypeStruct((2, NPS, npair * cw), F32))
        out_specs.append(pl.BlockSpec((2, NPS, sl), col3))
    res = pl.pallas_call(
        functools.partial(_s5_kernel, t=t, nseg=nseg, segmented=segmented),
        grid=(npair // S5_PPG,),
        in_specs=[pl.BlockSpec((m, lt), col2),
                  pl.BlockSpec(pc.shape, lambda g: (0, 0)),
                  pl.BlockSpec((2, S5_PPG, cw, cw), lambda g: (0, g, 0, 0)),
                  pl.BlockSpec((S5_PPG, cw, 2 * cw), lambda g: (g, 0, 0)),
                  pl.BlockSpec((2, S5_S, S5_PPG, cw // S5_S, cw), lambda g: (0, 0, g, 0, 0)),
                  pl.BlockSpec((2, NPS, sl), col3),
                  pl.BlockSpec((2, 1, sl), col3),
                  pl.BlockSpec((2, h0.shape[1], sl), col3)],
        out_specs=out_specs, out_shape=out_shape,
        scratch_shapes=[pltpu.VMEM((r, sl), F32), pltpu.VMEM((2, r, sl), F32), pltpu.VMEM((NPS, sl), F32),
                        pltpu.VMEM((S5_PPG, cw, cw), BF16)],
        compiler_params=_cparams("arbitrary"), name="s5",
    )(proj, pc, w1, w2t, kx, a_bc, a_t, h0)
    return (res[0], res[1]) if not segmented else (res[0], None)


def _s5_params(lam_re, lam_im, log_step, b_re, b_im, c_re, c_im, t_segs):
    nd, g, p = lam_re.shape
    n = b_re.shape[-1]
    npair = g // S5_GP
    hi = lax.Precision.HIGHEST
    dt = jnp.exp(log_step)[..., None]
    z_re, z_im = lam_re * dt, lam_im * dt
    mag = jnp.exp(z_re)
    ab_re, ab_im = mag * jnp.cos(z_im), mag * jnp.sin(z_im)
    den = lam_re * lam_re + lam_im * lam_im
    n_re = ab_re - 1.0
    f_re = (n_re * lam_re + ab_im * lam_im) / den
    f_im = (ab_im * lam_re - n_re * lam_im) / den
    bb_re = f_re[..., None] * b_re - f_im[..., None] * b_im
    bb_im = f_re[..., None] * b_im + f_im[..., None] * b_re

    cw = 2 * S5_GP * p
    col = jnp.arange(cw)
    is_re = col < cw // 2
    smask = ((col // p) % S5_GP == jnp.arange(S5_GP)[:, None]).astype(F32)[:, None, :]
    omask = ((col // n) % S5_GP == jnp.arange(S5_GP)[:, None]).astype(F32)[:, None, :]

    def dup(x):
        return jnp.concatenate([x, x], axis=-1)

    zr = dup(z_re.reshape(nd, npair, S5_GP * p))
    zi = dup(z_im.reshape(nd, npair, S5_GP * p))

    def power(e):
        m = jnp.exp(zr[:, :, None, :] * e[:, None, :, None])
        ang = zi[:, :, None, :] * e[:, None, :, None]
        return m * jnp.cos(ang), m * jnp.sin(ang)

    steps = jnp.arange(S5_S, dtype=F32)
    p1r, p1i = power(jnp.stack([S5_S - 1 - steps, steps]))
    p2r, p2i = power(jnp.stack([steps + 1, S5_S - steps]))
    br, bi = [dup(x.reshape(nd, npair, S5_GP, p, n).transpose(0, 1, 4, 2, 3).reshape(nd, npair, n, S5_GP * p))
              for x in (bb_re, bb_im)]
    cr, ci = [dup(x.reshape(nd, npair, S5_GP, n, p).transpose(0, 1, 3, 2, 4).reshape(nd, npair, n, S5_GP * p))
              for x in (c_re, c_im)]
    pj = lambda x: x[:, :, :, None, None, :]
    pn = lambda x: x[:, :, None, None, :, :]
    w1 = jnp.where(is_re, pj(p1r) * pn(br) - pj(p1i) * pn(bi), pj(p1r) * pn(bi) + pj(p1i) * pn(br)) * smask
    w1 = w1.reshape(nd, npair, S5_S * S5_GP * n, cw).astype(BF16)
    w2t = jnp.where(is_re, pj(p2r) * pn(cr) - pj(p2i) * pn(ci), -(pj(p2r) * pn(ci) + pj(p2i) * pn(cr))) * smask
    w2t = w2t.reshape(nd, npair, S5_S * S5_GP * n, cw).astype(BF16)
    w2t = jnp.concatenate([w2t[dd] for dd in range(nd)], axis=-1)

    mag = jnp.exp(z_re[..., None] * steps)
    lr, li = mag * jnp.cos(z_im[..., None] * steps), mag * jnp.sin(z_im[..., None] * steps)
    abt_re = lr[:, :, :, None, :] * bb_re[..., None] - li[:, :, :, None, :] * bb_im[..., None]
    abt_im = lr[:, :, :, None, :] * bb_im[..., None] + li[:, :, :, None, :] * bb_re[..., None]
    kt = (jnp.einsum('dgnp,dgpmt->dgtnm', c_re, abt_re, precision=hi)
          - jnp.einsum('dgnp,dgpmt->dgtnm', c_im, abt_im, precision=hi))
    kx = jnp.tile(kt.reshape(nd, npair, S5_GP, S5_S, n, n).transpose(0, 3, 1, 2, 5, 4), (1, 1, 1, 1, 1, S5_S * S5_GP))
    kx = (kx * omask).reshape(nd, S5_S, npair, S5_GP * n, cw)

    def state_cols(e):
        er, ei = power(jnp.full((nd, 1), e, F32))
        return jnp.where(is_re, er, ei).reshape(nd, npair * cw)

    a_s = state_cols(float(S5_S))
    a_bc = jnp.broadcast_to(a_s[:, None, :], (nd, NPS, a_s.shape[-1]))
    a_ts = [state_cols(float(ts))[:, None, :] for ts in t_segs]
    return w1, w2t, kx, a_bc, a_ts


def _s5_state_to_cols(s_re, s_im):
    b, nd, g, p = s_re.shape
    npair = g // S5_GP
    st = jnp.stack([s_re.reshape(b, nd, npair, S5_GP * p), s_im.reshape(b, nd, npair, S5_GP * p)], axis=3)
    return st.reshape(b, nd, -1).transpose(1, 0, 2)


def _s5_cols_to_state(st, g, p):
    nd, b, _ = st.shape
    npair = g // S5_GP
    st = st.reshape(nd, b, npair, 2, S5_GP, p).transpose(3, 1, 0, 2, 4, 5).reshape(2, b, nd, g, p)
    return st[0], st[1]


GLA_RB = 256


def _gla_kernel(q_ref, k_ref, v_ref, g_ref, glr_ref, w2_ref, gb_ref, ng_ref, *rest, seq, zero_init, want_state):
    rest = list(rest)
    s0_ref = None if zero_init else rest.pop(0)
    o_ref = rest.pop(0)
    sfin_ref = rest.pop(0) if want_state else None
    qt_sc, ke_sc, dec_sc, osum_sc = rest
    dk = q_ref.shape[1]
    dv = v_ref.shape[1]
    n = seq // GLA_CHUNK
    scale = dk ** -0.5
    ri = lax.broadcasted_iota(jnp.int32, (GLA_RB, GLA_RB), 0)
    ci = lax.broadcasted_iota(jnp.int32, (GLA_RB, GLA_RB), 1)
    same = (ri // GLA_CHUNK) == (ci // GLA_CHUNK)
    lower = same & (ci <= ri)
    upper = same & (ci >= ri)
    ones_blk = jnp.where(same, 1.0, 0.0).astype(BF16)
    tri = jnp.where(lower, 1.0, 0.0).astype(BF16)

    for rb in range(seq // GLA_RB):
        rows = slice(rb * GLA_RB, (rb + 1) * GLA_RB)
        x = _dot(glr_ref[rows, :].astype(BF16), w2_ref[...]) + gb_ref[...]
        la = (jnp.minimum(x, 0.0) - jnp.log1p(jnp.exp(-jnp.abs(x)))) * (1.0 / GLA_TAU)
        la_hi = la.astype(BF16)
        la_lo = (la - la_hi.astype(F32)).astype(BF16)
        pre = _dot(tri, la_hi) + _dot(tri, la_lo)
        tot = _dot(ones_blk, la_hi) + _dot(ones_blk, la_lo)
        cum = (pre[:, :dk], tot[:, dk:] - pre[:, dk:] + la[:, dk:])
        qf = q_ref[rows, :] * scale
        kf = k_ref[rows, :]
        att = None
        for d in range(2):
            td = tot[:, d * dk:(d + 1) * dk]
            q_t = (qf * jnp.exp(cum[d])).astype(BF16)
            k_t = (kf * jnp.exp(-cum[d])).astype(BF16)
            qt_sc[d, rows, :] = q_t
            ke_sc[d, rows, :] = (kf * jnp.exp(td - cum[d])).astype(BF16)
            dec_sc[d, rows, :] = jnp.exp(td)
            a_d = jnp.where(lower if d == 0 else upper, _dot_nt(q_t, k_t), 0.0)
            att = a_d if att is None else att + a_d
        osum_sc[rows, :] = _dot(att.astype(BF16), v_ref[rows, :].astype(BF16))

    st = [jnp.zeros((dv, dk), F32) if zero_init else s0_ref[d].T for d in range(2)]
    for c in range(n):
        for d in range(2):
            cc = c if d == 0 else n - 1 - c
            rows = slice(cc * GLA_CHUNK, (cc + 1) * GLA_CHUNK)
            osum_sc[rows, :] += _dot_nt(qt_sc[d, rows, :], st[d].astype(BF16))
            kv = _dot_tn(v_ref[rows, :].astype(BF16), ke_sc[d, rows, :])
            st[d] = st[d] * dec_sc[d, cc * GLA_CHUNK:cc * GLA_CHUNK + 1, :] + kv
    if want_state:
        for d in range(2):
            sfin_ref[d] = st[d].T

    for rb in range(seq // GLA_RB):
        rows = slice(rb * GLA_RB, (rb + 1) * GLA_RB)
        o = osum_sc[rows, :]
        o = o * lax.rsqrt(jnp.mean(o * o, axis=-1, keepdims=True) + EPS) * ng_ref[...]
        o_ref[rows, :] = (o * _silu(g_ref[rows, :])).astype(BF16)


def _gla(proj, q_col, glr, w2p, gate_b, norm_g, s0, *, nb, seq, heads, want_state):
    qk = w2p.shape[-1] // 2
    dk = qk // heads
    dv = norm_g.shape[1]
    vdim = dv * heads
    zero_init = s0 is None
    in_specs = [pl.BlockSpec((seq, dk), lambda b, h: (b, q_col // dk + h)),
                pl.BlockSpec((seq, dk), lambda b, h: (b, (q_col + qk) // dk + h)),
                pl.BlockSpec((seq, dv), lambda b, h: (b, (q_col + 2 * qk) // dv + h)),
                pl.BlockSpec((seq, dv), lambda b, h: (b, (q_col + 2 * qk + vdim) // dv + h)),
                pl.BlockSpec((seq, glr.shape[1]), lambda b, h: (b, 0)),
                pl.BlockSpec((w2p.shape[0], 2 * dk), lambda b, h: (0, h)),
                pl.BlockSpec((1, 2 * dk), lambda b, h: (0, h)),
                pl.BlockSpec((1, dv), lambda b, h: (0, 0))]
    args = [proj, proj, proj, proj, glr, w2p, gate_b, norm_g]
    if not zero_init:
        in_specs.append(pl.BlockSpec((None, 2, None, dk, dv), lambda b, h: (b, 0, h, 0, 0)))
        args.append(s0)
    out_shape = [jax.ShapeDtypeStruct((nb * seq, vdim), BF16)]
    out_specs = [pl.BlockSpec((seq, dv), lambda b, h: (b, h))]
    if want_state:
        out_shape.append(jax.ShapeDtypeStruct((nb, 2, heads, dk, dv), F32))
        out_specs.append(pl.BlockSpec((None, 2, None, dk, dv), lambda b, h: (b, 0, h, 0, 0)))
    res = pl.pallas_call(
        functools.partial(_gla_kernel, seq=seq, zero_init=zero_init, want_state=want_state),
        grid=(nb, heads), in_specs=in_specs, out_specs=out_specs, out_shape=out_shape,
        scratch_shapes=[pltpu.VMEM((2, seq, dk), BF16), pltpu.VMEM((2, seq, dk), BF16),
                        pltpu.VMEM((2, seq, dk), F32), pltpu.VMEM((seq, dv), F32)],
        compiler_params=_cparams("arbitrary", "arbitrary"), name="gla",
    )(*args)
    return (res[0], res[1]) if want_state else (res[0], None)


def _even_out_kernel(y_ref, u_ref, o_ref, x_ref, mod_ref, sd_ref, gw_ref, gb_ref, wo_ref, out_ref):
    sw = u_ref.shape[1]
    ys = _gelu(y_ref[...] + sd_ref[...] * u_ref[...])
    ys = ys * _sigmoid(_dot(ys.astype(BF16), gw_ref[...]) + gb_ref[...])
    y = _dot(ys.astype(BF16), wo_ref[:sw, :]) + _dot(o_ref[...], wo_ref[sw:, :])
    out_ref[...] = x_ref[...] + mod_ref[5:6, :] * y


def _even_out(y_s5, proj, o_gla, x, mod_l, s5_d, glu_w, glu_b, w_out, *, t, row_base, pseq_per_cond):
    m, d = x.shape
    sw = y_s5.shape[1]
    return pl.pallas_call(
        _even_out_kernel, grid=(NPS,),
        in_specs=[pl.BlockSpec((t, sw), lambda i: (i, 0)),
                  pl.BlockSpec((t, sw), lambda i: (i, 0)),
                  pl.BlockSpec((t, o_gla.shape[1]), lambda i: (i, 0)),
                  pl.BlockSpec((t, d), lambda i: (i, 0)),
                  pl.BlockSpec((None, N_MOD, d), lambda i: (row_base + i // pseq_per_cond, 0, 0)),
                  pl.BlockSpec((1, sw), lambda i: (0, 0)),
                  pl.BlockSpec((sw, sw), lambda i: (0, 0)),
                  pl.BlockSpec((1, sw), lambda i: (0, 0)),
                  pl.BlockSpec(w_out.shape, lambda i: (0, 0))],
        out_specs=pl.BlockSpec((t, d), lambda i: (i, 0)),
        out_shape=jax.ShapeDtypeStruct((m, d), F32),
        compiler_params=_cparams("arbitrary"), name="even_out",
    )(y_s5, proj, o_gla, x, mod_l, s5_d, glu_w, glu_b, w_out)


LRU_TB = 16
CONV_W = 4
CONV_LEFT = 2
LRU_UNROLL = 8


def _lru_kernel(x_ref, cw_ref, cb_ref, wa_ref, ba_ref, wx_ref, bx_ref, lam_ref, h0_ref, hs_ref, *rest,
                t, glen, nseg, segmented):
    if segmented:
        xp_sc, xc_sc, a_sc, b_sc, hs_sc, e_sc, p_sc, s_sc = rest
    else:
        stout_ref, xp_sc, xc_sc, a_sc, b_sc, hs_sc = rest
    r, w = x_ref.shape
    rb = LRU_TB * NPS
    pad = CONV_LEFT * NPS
    perm = _perm_matrix(NPS, LRU_TB)
    perm_back = _perm_matrix(LRU_TB, NPS)

    xp_sc[0:pad, :] = jnp.zeros((pad, w), F32)
    xp_sc[pad + r:pad + r + pad, :] = jnp.zeros((pad, w), F32)

    def load_tm(i, c):
        xin = jnp.concatenate([x_ref[pl.ds(pl.multiple_of(b * t + i * LRU_TB, LRU_TB), LRU_TB), :]
                               for b in range(NPS)], axis=0)
        xp_sc[pl.ds(pl.multiple_of(pad + i * rb, NPS), rb), :] = _permute_rows_f32(perm, xin)
        return c
    lax.fori_loop(0, t // LRU_TB, load_tm, 0, unroll=2)

    tloc = lax.broadcasted_iota(jnp.int32, (rb, w), 0) // NPS

    def conv(i, c):
        tg = (i * LRU_TB + tloc) % glen
        acc = jnp.zeros((rb, w), F32) + cb_ref[...]
        for kk in range(CONV_W):
            off = kk - CONV_LEFT
            xs = xp_sc[pl.ds(pl.multiple_of(pad + i * rb + off * NPS, NPS), rb), :]
            ok = (tg + off >= 0) & (tg + off < glen)
            acc = acc + jnp.where(ok, xs, 0.0) * cw_ref[kk:kk + 1, :]
        xc_sc[pl.ds(pl.multiple_of(i * rb, rb), rb), :] = acc
        return c
    lax.fori_loop(0, r // rb, conv, 0)

    for d in range(2):
        c2 = (-0.25 * LRU_C) * _softplus(-lam_ref[d])

        def gates(i, c):
            rows = pl.ds(pl.multiple_of(i * rb, rb), rb)
            xc = xc_sc[rows, :]
            xb = xc.astype(BF16)
            tr = jnp.tanh(_dot(xb, wa_ref[d]) + ba_ref[d])
            ti = jnp.tanh(_dot(xb, wx_ref[d]) + bx_ref[d])
            th = jnp.tanh(c2 + c2 * tr)
            rcp = 1.0 / (1.0 - th)
            a_sc[rows, :] = (1.0 + th) * rcp
            b_sc[rows, :] = rcp * jnp.sqrt(-th) * ((1.0 + ti) * xc)
            return c
        lax.fori_loop(0, r // rb, gates, 0, unroll=2)

        def trow(s):
            return pl.ds(pl.multiple_of((s if d == 0 else t - 1 - s) * NPS, NPS), NPS)

        if segmented:
            def sweep(s, carry):
                h, p = carry
                rows = trow(s)
                a = a_sc[rows, :]
                return a * h + b_sc[rows, :], a * p
            e, p = lax.fori_loop(0, t, sweep, (jnp.zeros((NPS, w), F32), jnp.ones((NPS, w), F32)),
                                 unroll=LRU_UNROLL)
            e_sc[...] = e
            p_sc[...] = p
            nb = NPS // nseg
            for b in range(nb):
                prev = h0_ref[d, b:b + 1, :]
                for k in (range(nseg) if d == 0 else range(nseg - 1, -1, -1)):
                    j = b * nseg + k
                    s_sc[j:j + 1, :] = prev
                    prev = p_sc[j:j + 1, :] * prev + e_sc[j:j + 1, :]
            h_init = s_sc[...]
        else:
            h_init = h0_ref[d]

        def scan(s, h):
            rows = trow(s)
            h = a_sc[rows, :] * h + b_sc[rows, :]
            if d == 0:
                hs_sc[rows, :] = h
            else:
                hs_sc[rows, :] += h
            return h
        h_fin = lax.fori_loop(0, t, scan, h_init, unroll=LRU_UNROLL)
        if not segmented:
            stout_ref[d] = h_fin

    def store_bm(i, c):
        hb = _permute_rows_f32(perm_back, hs_sc[pl.ds(pl.multiple_of(i * rb, rb), rb), :])
        for b in range(NPS):
            hs_ref[pl.ds(pl.multiple_of(b * t + i * LRU_TB, LRU_TB), LRU_TB), :] = hb[b * LRU_TB:(b + 1) * LRU_TB]
        return c
    lax.fori_loop(0, t // LRU_TB, store_bm, 0, unroll=2)


def _lru(proj, x_col, width, conv_w, conv_b, wa, ba, wx, bx, lam, h0, *, t, glen, nseg):
    segmented = nseg > 1
    r = proj.shape[0]
    heads, blk = wa.shape[1], wa.shape[2]
    col = lambda hd: (0, hd)
    col3 = lambda hd: (0, 0, hd)
    out_shape = [jax.ShapeDtypeStruct((r, width), F32)]
    out_specs = [pl.BlockSpec((r, blk), col)]
    scratch = [pltpu.VMEM((r + 2 * CONV_LEFT * NPS, blk), F32)] + [pltpu.VMEM((r, blk), F32)] * 4
    if segmented:
        scratch += [pltpu.VMEM((NPS, blk), F32)] * 3
    else:
        out_shape.append(jax.ShapeDtypeStruct((2, NPS, width), F32))
        out_specs.append(pl.BlockSpec((2, NPS, blk), col3))
    res = pl.pallas_call(
        functools.partial(_lru_kernel, t=t, glen=glen, nseg=nseg, segmented=segmented),
        grid=(heads,),
        in_specs=[pl.BlockSpec((r, blk), lambda hd: (0, x_col // blk + hd)),
                  pl.BlockSpec((CONV_W, blk), col),
                  pl.BlockSpec((1, blk), col),
                  pl.BlockSpec((2, None, blk, blk), lambda hd: (0, hd, 0, 0)),
                  pl.BlockSpec((2, 1, blk), col3),
                  pl.BlockSpec((2, None, blk, blk), lambda hd: (0, hd, 0, 0)),
                  pl.BlockSpec((2, 1, blk), col3),
                  pl.BlockSpec((2, 1, blk), col3),
                  pl.BlockSpec((2, h0.shape[1], blk), col3)],
        out_specs=out_specs, out_shape=out_shape, scratch_shapes=scratch,
        compiler_params=_cparams("arbitrary"), name="lru",
    )(proj, conv_w, conv_b, wa, ba, wx, bx, lam, h0)
    return (res[0], res[1]) if not segmented else (res[0], None)


def _odd_out_kernel(hs_ref, gate_ref, x_ref, mod_ref, wo_ref, out_ref):
    y = _dot((hs_ref[...] * _gelu(gate_ref[...])).astype(BF16), wo_ref[...])
    out_ref[...] = x_ref[...] + mod_ref[5:6, :] * y


def _odd_out(hs, proj, x, mod_l, w_out, *, t, row_base, pseq_per_cond):
    m, d = x.shape
    w = hs.shape[1]
    return pl.pallas_call(
        _odd_out_kernel, grid=(NPS,),
        in_specs=[pl.BlockSpec((t, w), lambda i: (i, 0)),
                  pl.BlockSpec((t, w), lambda i: (i, 0)),
                  pl.BlockSpec((t, d), lambda i: (i, 0)),
                  pl.BlockSpec((None, N_MOD, d), lambda i: (row_base + i // pseq_per_cond, 0, 0)),
                  pl.BlockSpec(w_out.shape, lambda i: (0, 0))],
        out_specs=pl.BlockSpec((t, d), lambda i: (i, 0)),
        out_shape=jax.ShapeDtypeStruct((m, d), F32),
        compiler_params=_cparams("arbitrary"), name="odd_out",
    )(hs, proj, x, mod_l, w_out)


def kernel(x_prompt, x_sample, state_s5_re, state_s5_im, state_gla, state_lru, c, c_ctx, norm_g, ada_w, ada_b, ffn_w_in, ffn_w_out, final_norm_g, ev_w_in, ev_w_out, s5_lam_re, s5_lam_im, s5_log_step, s5_b_re, s5_b_im, s5_c_re, s5_c_im, s5_d, s5_glu_w, s5_glu_b, gla_gate_w2, gla_gate_b, gla_norm_g, od_w_in, od_w_out, lru_conv_w, lru_conv_b, lru_wa, lru_ba, lru_wx, lru_bx, lru_lam):
    nbc, seq, d = x_prompt.shape
    nbl, dseq, _ = x_sample.shape
    depth = norm_g.shape[0]
    assert nbc == NPS and NPS % nbl == 0
    nseg = NPS // nbl
    tl = dseq // nseg
    grid_w = 64
    assert tl % grid_w == 0 and tl % LRU_TB == 0 and seq % LRU_TB == 0 and seq % GLA_RB == 0 and dseq % GLA_RB == 0
    sw = s5_d.shape[1]
    qk = gla_gate_w2.shape[-1]
    heads = state_gla.shape[3]
    rank = gla_gate_w2.shape[2]
    g5, p5 = s5_lam_re.shape[2], s5_lam_re.shape[3]
    main = ev_w_in.shape[2] - 2 * rank
    assert main % PROJ_TN == 0 and (g5 // S5_GP) % S5_PPG == 0

    passes = [dict(x=x_prompt.reshape(nbc * seq, d), base=0, rpc=nbc * seq, ppc=NPS, t=seq, nseg=1,
                   glen=seq, nb=nbc, seq=seq),
              dict(x=x_sample.reshape(nbl * dseq, d), base=1, rpc=dseq, ppc=nseg, t=tl, nseg=nseg,
                   glen=grid_w, nb=nbl, seq=dseq)]

    cond8 = jnp.concatenate([c_ctx[None, :], c, jnp.zeros((8 - 1 - nbl, d), F32)], axis=0)
    mod = _ada(cond8, ada_w, ada_b).reshape(depth, 8, N_MOD, d)
    fg = final_norm_g.reshape(1, d)

    new_s5_re, new_s5_im, new_gla, new_lru = [], [], [], []
    for l in range(depth):
        mod_l = mod[l]
        ng = lambda s: norm_g[l, s].reshape(1, d)
        for ps in passes:
            ps['x'] = _ffn(ps['x'], mod_l, ng(0), fg, ffn_w_in, ffn_w_out, l, 0, mi=0,
                           row_base=ps['base'], rows_per_cond=ps['rpc'], final=False)
        if l % 2 == 0:
            e = l // 2
            w_glr = jnp.pad(ev_w_in[e][:, main:], ((0, 0), (0, 128 - 2 * rank))).astype(BF16)
            w_o = ev_w_out[e].astype(BF16)
            glu_w = s5_glu_w[e].astype(BF16)
            dkh = qk // heads
            w2p = jnp.stack([jnp.pad(gla_gate_w2[e, dd], ((dd * rank, 128 - (dd + 1) * rank), (0, 0)))
                             for dd in range(2)])
            w2p = w2p.reshape(2, 128, heads, dkh).transpose(1, 2, 0, 3).reshape(128, 2 * qk).astype(BF16)
            gate_b = gla_gate_b[e].reshape(2, heads, dkh).transpose(1, 0, 2).reshape(1, 2 * qk)
            w1, w2t, kx, a_bc, a_ts = _s5_params(s5_lam_re[e], s5_lam_im[e], s5_log_step[e], s5_b_re[e], s5_b_im[e],
                                                 s5_c_re[e], s5_c_im[e], [ps['t'] for ps in passes])
            for pi, ps in enumerate(passes):
                t = ps['t']
                proj, glr = _norm_proj(ps['x'], mod_l, ng(1), ev_w_in, e, w_glr, mi=3,
                                       row_base=ps['base'], rows_per_cond=ps['rpc'])
                if pi == 0:
                    h0 = jnp.zeros((2, NPS, a_bc.shape[-1]), F32)
                    s0 = None
                else:
                    h0 = _s5_state_to_cols(state_s5_re[:, e], state_s5_im[:, e])
                    s0 = state_gla[:, e]
                y_s5, s5_fin = _s5(proj, sw, w1, w2t, kx, a_bc, a_ts[pi], h0, t=t, nseg=ps['nseg'])
                o_gla, gla_fin = _gla(proj, sw, glr, w2p, gate_b, gla_norm_g[e].reshape(1, -1), s0,
                                      nb=ps['nb'], seq=ps['seq'], heads=heads, want_state=(pi == 0))
                if pi == 0:
                    sr, si = _s5_cols_to_state(s5_fin, g5, p5)
                    new_s5_re.append(sr)
                    new_s5_im.append(si)
                    new_gla.append(gla_fin)
                ps['x'] = _even_out(y_s5, proj, o_gla, ps['x'], mod_l, s5_d[e].reshape(1, sw), glu_w,
                                    s5_glu_b[e].reshape(1, sw), w_o, t=t, row_base=ps['base'],
                                    pseq_per_cond=ps['ppc'])
        else:
            o = l // 2
            lw = od_w_in.shape[2] // 2
            w_o = od_w_out[o].astype(BF16)
            wa16 = (0.5 * lru_wa[o]).astype(BF16)
            wx16 = (0.5 * lru_wx[o]).astype(BF16)
            for pi, ps in enumerate(passes):
                t = ps['t']
                proj, _ = _norm_proj(ps['x'], mod_l, ng(1), od_w_in, o, None, mi=3,
                                     row_base=ps['base'], rows_per_cond=ps['rpc'])
                if pi == 0:
                    h0 = jnp.zeros((2, NPS, lw), F32)
                else:
                    h0 = state_lru[:, o].transpose(1, 0, 2)
                hs, lru_fin = _lru(proj, lw, lw, lru_conv_w[o], lru_conv_b[o].reshape(1, lw), wa16,
                                   0.5 * lru_ba[o].reshape(2, 1, lw), wx16, 0.5 * lru_bx[o].reshape(2, 1, lw),
                                   lru_lam[o].reshape(2, 1, lw), h0, t=t, glen=ps['glen'], nseg=ps['nseg'])
                if pi == 0:
                    new_lru.append(lru_fin.transpose(1, 0, 2))
                ps['x'] = _odd_out(hs, proj, ps['x'], mod_l, w_o, t=t, row_base=ps['base'],
                                   pseq_per_cond=ps['ppc'])
        for ps in passes:
            ps['x'] = _ffn(ps['x'], mod_l, ng(2), fg, ffn_w_in, ffn_w_out, l, 1, mi=6,
                           row_base=ps['base'], rows_per_cond=ps['rpc'], final=(l == depth - 1))

    y_prompt = passes[0]['x'].reshape(nbc, seq, d)
    y_sample = passes[1]['x'].reshape(nbl, dseq, d)
    return (y_prompt, y_sample, jnp.stack(new_s5_re, 1), jnp.stack(new_s5_im, 1),
            jnp.stack(new_gla, 1), jnp.stack(new_lru, 1))
```

```python
import functools
import math

import jax
import jax.numpy as jnp
from jax import lax
from jax.experimental import pallas as pl
from jax.experimental.pallas import tpu as pltpu

F32 = jnp.float32
BF16 = jnp.bfloat16
EPS = 1e-6
NPS = 16
N_MOD = 9
GLA_CHUNK = 64
GLA_TAU = 16.0
LRU_C = 8.0
VMEM_LIMIT = 58 * 1024 * 1024
ROW_CHUNK = 64


def _cparams(*sem):
    return pltpu.CompilerParams(dimension_semantics=sem, vmem_limit_bytes=VMEM_LIMIT)


def _dot(a, b):
    return jnp.dot(a, b, preferred_element_type=F32)


def _dot_nt(a, b):
    return lax.dot_general(a, b, (((1,), (1,)), ((), ())), preferred_element_type=F32)


def _dot_tn(a, b):
    return lax.dot_general(a, b, (((0,), (0,)), ((), ())), preferred_element_type=F32)


def _sigmoid(x):
    return 0.5 * (1.0 + jnp.tanh(0.5 * x))


def _silu(x):
    return x * _sigmoid(x)


def _gelu(x):
    return 0.5 * x * (1.0 + jnp.tanh(math.sqrt(2.0 / math.pi) * (x + 0.044715 * (x * x * x))))


def _softplus(x):
    return jnp.maximum(x, 0.0) + jnp.log1p(jnp.exp(-jnp.abs(x)))


def _rms_mod(x, g, scale, shift):
    return x * lax.rsqrt(jnp.mean(x * x, axis=-1, keepdims=True) + EPS) * (g * (1.0 + scale)) + shift


def _perm_matrix(n_outer, n_inner):
    n = n_outer * n_inner
    ro = lax.broadcasted_iota(jnp.int32, (n, n), 0)
    ci = lax.broadcasted_iota(jnp.int32, (n, n), 1)
    return jnp.where(ci == (ro % n_outer) * n_inner + ro // n_outer, 1.0, 0.0).astype(BF16)


def _permute_rows_f32(perm, x):
    hi = x.astype(BF16)
    r1 = x - hi.astype(F32)
    mid = r1.astype(BF16)
    lo = (r1 - mid.astype(F32)).astype(BF16)
    return _dot(perm, hi) + _dot(perm, mid) + _dot(perm, lo)


def _norm_rows(x_ref, mod_ref, g_ref, h_sc, mi):
    def body(r, c):
        rows = pl.ds(pl.multiple_of(r * ROW_CHUNK, ROW_CHUNK), ROW_CHUNK)
        h = _rms_mod(x_ref[rows, :], g_ref[...], mod_ref[mi + 1:mi + 2, :], mod_ref[mi:mi + 1, :])
        h_sc[rows, :] = h.astype(BF16)
        return c
    lax.fori_loop(0, x_ref.shape[0] // ROW_CHUNK, body, 0, unroll=2)


def _ada_kernel(c_ref, w_ref, b_ref, o_ref):
    ca = _silu(c_ref[...])
    o_ref[...] = _dot(ca.astype(BF16), w_ref[...].astype(BF16)) + b_ref[...]


def _ada(cond8, ada_w, ada_b):
    depth, d, n = ada_w.shape
    tn = 1024
    return pl.pallas_call(
        _ada_kernel, grid=(depth, n // tn),
        in_specs=[pl.BlockSpec((8, d), lambda l, j: (0, 0)),
                  pl.BlockSpec((None, d, tn), lambda l, j: (l, 0, j)),
                  pl.BlockSpec((None, 1, tn), lambda l, j: (l, 0, j))],
        out_specs=pl.BlockSpec((None, 8, tn), lambda l, j: (l, 0, j)),
        out_shape=jax.ShapeDtypeStruct((depth, 8, n), F32),
        compiler_params=_cparams("arbitrary", "arbitrary"), name="ada",
    )(cond8, ada_w, ada_b.reshape(depth, 1, n))


FFN_TM = 1024
FFN_TF = 256
FFN_NC = 512


def _ffn_kernel(x_ref, mod_ref, g_ref, fg_ref, wa_ref, wb_ref, wo_ref, *rest, mi, nf, final, has_h):
    h_ref, o_ref = rest if has_h else rest[::-1]
    j = pl.program_id(1)
    tm, d = x_ref.shape

    @pl.when(j == 0)
    def _():
        if not has_h:
            _norm_rows(x_ref, mod_ref, g_ref, h_ref, mi)
        o_ref[...] = jnp.zeros_like(o_ref)

    h = h_ref[...]
    a = _dot(h, wa_ref[...].astype(BF16))
    b = _dot(h, wb_ref[...].astype(BF16))
    act = (_silu(a) * b).astype(BF16)
    for n in range(d // FFN_NC):
        cols = slice(n * FFN_NC, (n + 1) * FFN_NC)
        o_ref[:, cols] += _dot(act, wo_ref[:, cols].astype(BF16))

    @pl.when(j == nf - 1)
    def _():
        def body(r, c):
            rows = pl.ds(pl.multiple_of(r * ROW_CHUNK, ROW_CHUNK), ROW_CHUNK)
            y = x_ref[rows, :] + 0.5 * mod_ref[mi + 2:mi + 3, :] * o_ref[rows, :]
            if final:
                y = y * lax.rsqrt(jnp.mean(y * y, axis=-1, keepdims=True) + EPS) * fg_ref[...]
            o_ref[rows, :] = y
            return c
        lax.fori_loop(0, tm // ROW_CHUNK, body, 0)


def _ffn(x, mod_l, norm_g, final_g, w_in, w_out, l, k, *, mi, row_base, rows_per_cond, final, h_pre=None):
    m, d = x.shape
    f = w_out.shape[2]
    tm = min(FFN_TM, rows_per_cond)
    tf = FFN_TF
    nf = f // tf
    row = lambda i, j: (row_base + (i * tm) // rows_per_cond, 0, 0)
    once = dict(pipeline_mode=pl.Buffered(1))
    in_specs = [pl.BlockSpec((tm, d), lambda i, j: (i, 0), **once),
                pl.BlockSpec((None, N_MOD, d), row),
                pl.BlockSpec((1, d), lambda i, j: (0, 0)),
                pl.BlockSpec((1, d), lambda i, j: (0, 0)),
                pl.BlockSpec((None, None, d, tf), lambda i, j: (l, k, 0, j)),
                pl.BlockSpec((None, None, d, tf), lambda i, j: (l, k, 0, j + nf)),
                pl.BlockSpec((None, None, tf, d), lambda i, j: (l, k, j, 0))]
    args = [x, mod_l, norm_g, final_g, w_in, w_in, w_out]
    if h_pre is not None:
        in_specs.append(pl.BlockSpec((tm, d), lambda i, j: (i, 0), **once))
        args.append(h_pre)
    return pl.pallas_call(
        functools.partial(_ffn_kernel, mi=mi, nf=nf, final=final, has_h=h_pre is not None),
        grid=(m // tm, nf), in_specs=in_specs,
        out_specs=pl.BlockSpec((tm, d), lambda i, j: (i, 0)),
        out_shape=jax.ShapeDtypeStruct((m, d), F32),
        scratch_shapes=[] if h_pre is not None else [pltpu.VMEM((tm, d), BF16)],
        compiler_params=_cparams("arbitrary", "arbitrary"), name="ffn",
    )(*args)


PROJ_TM = 1024
PROJ_TN = 1024


def _norm_proj_kernel(x_ref, mod_ref, g_ref, w_ref, *rest, mi, nmain):
    if len(rest) == 4:
        wx_ref, o_ref, ox_ref, h_sc = rest
    else:
        (o_ref, h_sc), wx_ref, ox_ref = rest, None, None
    j = pl.program_id(1)

    @pl.when(j == 0)
    def _():
        _norm_rows(x_ref, mod_ref, g_ref, h_sc, mi)

    @pl.when(j < nmain)
    def _():
        o_ref[...] = _dot(h_sc[...], w_ref[...].astype(BF16))

    if wx_ref is not None:
        @pl.when(j == nmain)
        def _():
            ox_ref[...] = _dot(h_sc[...], wx_ref[...])


def _norm_proj(x, mod_l, norm_g, w, e, w_extra, *, mi, row_base, rows_per_cond):
    m, d = x.shape
    tm = min(PROJ_TM, rows_per_cond)
    tn = PROJ_TN
    nmain = w.shape[2] // tn
    nj = nmain + (0 if w_extra is None else 1)
    jm = lambda j: jnp.minimum(j, nmain - 1)
    in_specs = [pl.BlockSpec((tm, d), lambda i, j: (i, 0)),
                pl.BlockSpec((None, N_MOD, d), lambda i, j: (row_base + (i * tm) // rows_per_cond, 0, 0)),
                pl.BlockSpec((1, d), lambda i, j: (0, 0)),
                pl.BlockSpec((None, d, tn), lambda i, j: (e, 0, jm(j)))]
    args = [x, mod_l, norm_g, w]
    out_shape = [jax.ShapeDtypeStruct((m, nmain * tn), F32)]
    out_specs = [pl.BlockSpec((tm, tn), lambda i, j: (i, jm(j)))]
    if w_extra is not None:
        in_specs.append(pl.BlockSpec(w_extra.shape, lambda i, j: (0, 0)))
        args.append(w_extra)
        out_shape.append(jax.ShapeDtypeStruct((m, w_extra.shape[1]), F32))
        out_specs.append(pl.BlockSpec((tm, w_extra.shape[1]), lambda i, j: (i, 0)))
    res = pl.pallas_call(
        functools.partial(_norm_proj_kernel, mi=mi, nmain=nmain), grid=(m // tm, nj),
        in_specs=in_specs, out_specs=out_specs, out_shape=out_shape,
        scratch_shapes=[pltpu.VMEM((tm, d), BF16)],
        compiler_params=_cparams("arbitrary", "arbitrary"), name="norm_proj",
    )(*args)
    return res if w_extra is not None else (res[0], None)


S5_S = 8
S5_GP = 2
S5_PPG = 4
S5_UNROLL = 4


def _s5_kernel(u_ref, pc_ref, w1_ref, w2t_ref, kx_ref, a_ref, at_ref, h0_ref, y_ref, *rest, t, nseg, segmented):
    if segmented:
        g_sc, v_sc, st_sc, w3_sc = rest
    else:
        stout_ref, g_sc, v_sc, st_sc, w3_sc = rest
    nblk = t // S5_S
    npair = w1_ref.shape[1]
    cw = w1_ref.shape[2]
    hw = cw // 2
    lt = u_ref.shape[1]
    sw = cw // S5_S

    jcol = lax.broadcasted_iota(jnp.int32, (sw, cw), 1) // sw
    for q in range(npair):
        for i in range(S5_S):
            acc = jnp.zeros((sw, cw), F32)
            for lag in range(S5_S - i):
                acc = acc + jnp.where(jcol == i + lag, kx_ref[0, lag, q], 0.0)
            for lag in range(i + 1):
                acc = acc + jnp.where(jcol == i - lag, kx_ref[1, lag, q], 0.0)
            w3_sc[q, i * sw:(i + 1) * sw, :] = acc.astype(BF16)

    def gather(k, c):
        r = pl.ds(pl.multiple_of(k * NPS, NPS), NPS)
        for j in range(S5_S):
            g_sc[r, j * lt:(j + 1) * lt] = u_ref[pl.ds(k * S5_S + j, NPS, stride=t), :]
        return c
    lax.fori_loop(0, nblk, gather, 0)
    u = _dot(g_sc[...].astype(BF16), pc_ref[...]).astype(BF16)

    for d in range(2):
        for q in range(npair):
            v_sc[d, :, q * cw:(q + 1) * cw] = _dot(u[:, q * cw:(q + 1) * cw], w1_ref[d, q])

        def advance(state, v):
            outs = []
            for q in range(npair):
                re = slice(q * cw, q * cw + hw)
                im = slice(q * cw + hw, (q + 1) * cw)
                ar, ai = a_ref[d, :, re], a_ref[d, :, im]
                hr, hi = state[:, re], state[:, im]
                outs.append(ar * hr - ai * hi + v[:, re])
                outs.append(ar * hi + ai * hr + v[:, im])
            return jnp.concatenate(outs, axis=1)

        def rows(kk):
            k = kk if d == 0 else nblk - 1 - kk
            return pl.ds(pl.multiple_of(k * NPS, NPS), NPS)

        if segmented:
            st_sc[...] = lax.fori_loop(0, nblk, lambda kk, s: advance(s, v_sc[d, rows(kk), :]),
                                       jnp.zeros(st_sc.shape, F32), unroll=S5_UNROLL)
            nb = NPS // nseg
            for b in range(nb):
                for q in range(npair):
                    re = slice(q * cw, q * cw + hw)
                    im = slice(q * cw + hw, (q + 1) * cw)
                    pr, pi = h0_ref[d, b:b + 1, re], h0_ref[d, b:b + 1, im]
                    ar, ai = at_ref[d, :, re], at_ref[d, :, im]
                    for k in (range(nseg) if d == 0 else range(nseg - 1, -1, -1)):
                        j = b * nseg + k
                        er, ei = st_sc[j:j + 1, re], st_sc[j:j + 1, im]
                        st_sc[j:j + 1, re] = pr
                        st_sc[j:j + 1, im] = pi
                        pr, pi = ar * pr - ai * pi + er, ar * pi + ai * pr + ei
            init = st_sc[...]
        else:
            init = h0_ref[d]

        def body(kk, s):
            r = rows(kk)
            v = v_sc[d, r, :]
            v_sc[d, r, :] = s
            return advance(s, v)
        fin = lax.fori_loop(0, nblk, body, init, unroll=S5_UNROLL)
        if not segmented:
            stout_ref[d] = fin

    for q in range(npair):
        cols = slice(q * cw, (q + 1) * cw)
        hcat = jnp.concatenate([v_sc[0, :, cols], v_sc[1, :, cols]], axis=1).astype(BF16)
        g_sc[:, cols] = _dot_nt(hcat, w2t_ref[q]) + _dot(u[:, cols], w3_sc[q])

    y = g_sc[...]
    hi = y.astype(BF16)
    lo = (y - hi.astype(F32)).astype(BF16)
    g_sc[...] = _dot_nt(hi, pc_ref[...]) + _dot_nt(lo, pc_ref[...])

    def scatter(k, c):
        r = pl.ds(pl.multiple_of(k * NPS, NPS), NPS)
        for j in range(S5_S):
            y_ref[pl.ds(k * S5_S + j, NPS, stride=t), :] = g_sc[r, j * lt:(j + 1) * lt]
        return c
    lax.fori_loop(0, nblk, scatter, 0)


def _s5_col_perm(lt, gw):
    r = jnp.arange(S5_S * lt)
    j, lane = r // lt, r % lt
    dst = (lane // gw) * (S5_S * gw) + j * gw + lane % gw
    return (dst[:, None] == jnp.arange(S5_S * lt)[None, :]).astype(BF16)


def _s5(proj, width, w1, w2t, kx, a_bc, a_t, h0, *, t, nseg):
    segmented = nseg > 1
    m = proj.shape[0]
    npair, cw = w1.shape[1], w1.shape[2]
    sl = S5_PPG * cw
    lt = sl // S5_S
    r = (t // S5_S) * NPS
    pc = _s5_col_perm(lt, cw // S5_S)
    col2 = lambda g: (0, g)
    col3 = lambda g: (0, 0, g)
    out_shape = [jax.ShapeDtypeStruct((m, width), F32)]
    out_specs = [pl.BlockSpec((m, lt), col2)]
    if not segmented:
        out_shape.append(jax.ShapeDtypeStruct((2, NPS, npair * cw), F32))
        out_specs.append(pl.BlockSpec((2, NPS, sl), col3))
    res = pl.pallas_call(
        functools.partial(_s5_kernel, t=t, nseg=nseg, segmented=segmented),
        grid=(npair // S5_PPG,),
        in_specs=[pl.BlockSpec((m, lt), col2),
                  pl.BlockSpec(pc.shape, lambda g: (0, 0)),
                  pl.BlockSpec((2, S5_PPG, cw, cw), lambda g: (0, g, 0, 0)),
                  pl.BlockSpec((S5_PPG, cw, 2 * cw), lambda g: (g, 0, 0)),
                  pl.BlockSpec((2, S5_S, S5_PPG, cw // S5_S, cw), lambda g: (0, 0, g, 0, 0)),
                  pl.BlockSpec((2, NPS, sl), col3),
                  pl.BlockSpec((2, 1, sl), col3),
                  pl.BlockSpec((2, h0.shape[1], sl), col3)],
        out_specs=out_specs, out_shape=out_shape,
        scratch_shapes=[pltpu.VMEM((r, sl), F32), pltpu.VMEM((2, r, sl), F32), pltpu.VMEM((NPS, sl), F32),
                        pltpu.VMEM((S5_PPG, cw, cw), BF16)],
        compiler_params=_cparams("arbitrary"), name="s5",
    )(proj, pc, w1, w2t, kx, a_bc, a_t, h0)
    return (res[0], res[1]) if not segmented else (res[0], None)


def _s5_params(lam_re, lam_im, log_step, b_re, b_im, c_re, c_im, t_segs):
    nd, g, p = lam_re.shape
    n = b_re.shape[-1]
    npair = g // S5_GP
    hi = lax.Precision.HIGHEST
    dt = jnp.exp(log_step)[..., None]
    z_re, z_im = lam_re * dt, lam_im * dt
    mag = jnp.exp(z_re)
    ab_re, ab_im = mag * jnp.cos(z_im), mag * jnp.sin(z_im)
    den = lam_re * lam_re + lam_im * lam_im
    n_re = ab_re - 1.0
    f_re = (n_re * lam_re + ab_im * lam_im) / den
    f_im = (ab_im * lam_re - n_re * lam_im) / den
    bb_re = f_re[..., None] * b_re - f_im[..., None] * b_im
    bb_im = f_re[..., None] * b_im + f_im[..., None] * b_re

    cw = 2 * S5_GP * p
    col = jnp.arange(cw)
    is_re = col < cw // 2
    smask = ((col // p) % S5_GP == jnp.arange(S5_GP)[:, None]).astype(F32)[:, None, :]
    omask = ((col // n) % S5_GP == jnp.arange(S5_GP)[:, None]).astype(F32)[:, None, :]

    def dup(x):
        return jnp.concatenate([x, x], axis=-1)

    zr = dup(z_re.reshape(nd, npair, S5_GP * p))
    zi = dup(z_im.reshape(nd, npair, S5_GP * p))

    def power(e):
        m = jnp.exp(zr[:, :, None, :] * e[:, None, :, None])
        ang = zi[:, :, None, :] * e[:, None, :, None]
        return m * jnp.cos(ang), m * jnp.sin(ang)

    steps = jnp.arange(S5_S, dtype=F32)
    p1r, p1i = power(jnp.stack([S5_S - 1 - steps, steps]))
    p2r, p2i = power(jnp.stack([steps + 1, S5_S - steps]))
    br, bi = [dup(x.reshape(nd, npair, S5_GP, p, n).transpose(0, 1, 4, 2, 3).reshape(nd, npair, n, S5_GP * p))
              for x in (bb_re, bb_im)]
    cr, ci = [dup(x.reshape(nd, npair, S5_GP, n, p).transpose(0, 1, 3, 2, 4).reshape(nd, npair, n, S5_GP * p))
              for x in (c_re, c_im)]
    pj = lambda x: x[:, :, :, None, None, :]
    pn = lambda x: x[:, :, None, None, :, :]
    w1 = jnp.where(is_re, pj(p1r) * pn(br) - pj(p1i) * pn(bi), pj(p1r) * pn(bi) + pj(p1i) * pn(br)) * smask
    w1 = w1.reshape(nd, npair, S5_S * S5_GP * n, cw).astype(BF16)
    w2t = jnp.where(is_re, pj(p2r) * pn(cr) - pj(p2i) * pn(ci), -(pj(p2r) * pn(ci) + pj(p2i) * pn(cr))) * smask
    w2t = w2t.reshape(nd, npair, S5_S * S5_GP * n, cw).astype(BF16)
    w2t = jnp.concatenate([w2t[dd] for dd in range(nd)], axis=-1)

    mag = jnp.exp(z_re[..., None] * steps)
    lr, li = mag * jnp.cos(z_im[..., None] * steps), mag * jnp.sin(z_im[..., None] * steps)
    abt_re = lr[:, :, :, None, :] * bb_re[..., None] - li[:, :, :, None, :] * bb_im[..., None]
    abt_im = lr[:, :, :, None, :] * bb_im[..., None] + li[:, :, :, None, :] * bb_re[..., None]
    reps = (1, 1, S5_S * S5_GP, 1)
    kt = (jnp.einsum('dgcp,dgpmt->dgtmc', jnp.tile(c_re, reps), abt_re, precision=hi)
          - jnp.einsum('dgcp,dgpmt->dgtmc', jnp.tile(c_im, reps), abt_im, precision=hi))
    kx = kt.reshape(nd, npair, S5_GP, S5_S, n, cw).transpose(0, 3, 1, 2, 4, 5) * omask
    kx = kx.reshape(nd, S5_S, npair, S5_GP * n, cw)

    def state_cols(e):
        er, ei = power(jnp.full((nd, 1), e, F32))
        return jnp.where(is_re, er, ei).reshape(nd, npair * cw)

    a_s = state_cols(float(S5_S))
    a_bc = jnp.broadcast_to(a_s[:, None, :], (nd, NPS, a_s.shape[-1]))
    a_ts = [state_cols(float(ts))[:, None, :] for ts in t_segs]
    return w1, w2t, kx, a_bc, a_ts


def _s5_state_to_cols(s_re, s_im):
    b, nd, g, p = s_re.shape
    npair = g // S5_GP
    st = jnp.stack([s_re.reshape(b, nd, npair, S5_GP * p), s_im.reshape(b, nd, npair, S5_GP * p)], axis=3)
    return st.reshape(b, nd, -1).transpose(1, 0, 2)


def _s5_cols_to_state(st, g, p):
    nd, b, _ = st.shape
    npair = g // S5_GP
    st = st.reshape(nd, b, npair, 2, S5_GP, p).transpose(3, 1, 0, 2, 4, 5).reshape(2, b, nd, g, p)
    return st[0], st[1]


GLA_RB = 256


GLA_HPS = 2


def _gla_kernel(q_ref, k_ref, v_ref, g_ref, glr_ref, w2_ref, gb_ref, ng_ref, *rest, seq, zero_init, want_state):
    rest = list(rest)
    s0_ref = None if zero_init else rest.pop(0)
    o_ref = rest.pop(0)
    sfin_ref = rest.pop(0) if want_state else None
    dk = q_ref.shape[1] // GLA_HPS
    dv = v_ref.shape[1] // GLA_HPS
    for hh in range(GLA_HPS):
        lk, lv, l2 = pl.ds(hh * dk, dk), pl.ds(hh * dv, dv), pl.ds(hh * 2 * dk, 2 * dk)
        args = [q_ref.at[:, lk], k_ref.at[:, lk], v_ref.at[:, lv], g_ref.at[:, lv], glr_ref,
                w2_ref.at[:, l2], gb_ref.at[:, l2], ng_ref]
        if not zero_init:
            args.append(s0_ref.at[:, hh])
        args.append(o_ref.at[:, lv])
        if want_state:
            args.append(sfin_ref.at[:, hh])
        _gla_head(*args, *[sc.at[hh] for sc in rest], seq=seq, zero_init=zero_init, want_state=want_state)


def _gla_head(q_ref, k_ref, v_ref, g_ref, glr_ref, w2_ref, gb_ref, ng_ref, *rest, seq, zero_init, want_state):
    rest = list(rest)
    s0_ref = None if zero_init else rest.pop(0)
    o_ref = rest.pop(0)
    sfin_ref = rest.pop(0) if want_state else None
    qt_sc, ke_sc, dec_sc, osum_sc = rest
    dk = q_ref.shape[1]
    dv = v_ref.shape[1]
    n = seq // GLA_CHUNK
    scale = dk ** -0.5
    ri = lax.broadcasted_iota(jnp.int32, (GLA_RB, GLA_RB), 0)
    ci = lax.broadcasted_iota(jnp.int32, (GLA_RB, GLA_RB), 1)
    same = (ri // GLA_CHUNK) == (ci // GLA_CHUNK)
    lower = same & (ci <= ri)
    upper = same & (ci >= ri)
    ones_blk = jnp.where(same, 1.0, 0.0).astype(BF16)
    tri = jnp.where(lower, 1.0, 0.0).astype(BF16)

    for rb in range(seq // GLA_RB):
        rows = slice(rb * GLA_RB, (rb + 1) * GLA_RB)
        x = _dot(glr_ref[rows, :].astype(BF16), w2_ref[...]) + gb_ref[...]
        la = (jnp.minimum(x, 0.0) - jnp.log1p(jnp.exp(-jnp.abs(x)))) * (1.0 / GLA_TAU)
        la_hi = la.astype(BF16)
        la_lo = (la - la_hi.astype(F32)).astype(BF16)
        pre = _dot(tri, la_hi) + _dot(tri, la_lo)
        tot = _dot(ones_blk, la_hi) + _dot(ones_blk, la_lo)
        cum = (pre[:, :dk], tot[:, dk:] - pre[:, dk:] + la[:, dk:])
        qf = q_ref[rows, :] * scale
        kf = k_ref[rows, :]
        att = None
        for d in range(2):
            td = tot[:, d * dk:(d + 1) * dk]
            q_t = (qf * jnp.exp(cum[d])).astype(BF16)
            k_t = (kf * jnp.exp(-cum[d])).astype(BF16)
            qt_sc[d, rows, :] = q_t
            ke_sc[d, rows, :] = (kf * jnp.exp(td - cum[d])).astype(BF16)
            dec_sc[d, rows, :] = jnp.exp(td)
            a_d = jnp.where(lower if d == 0 else upper, _dot_nt(q_t, k_t), 0.0)
            att = a_d if att is None else att + a_d
        osum_sc[rows, :] = _dot(att.astype(BF16), v_ref[rows, :].astype(BF16))

    st = [jnp.zeros((dv, dk), F32) if zero_init else s0_ref[d].T for d in range(2)]
    for c in range(n):
        for d in range(2):
            cc = c if d == 0 else n - 1 - c
            rows = slice(cc * GLA_CHUNK, (cc + 1) * GLA_CHUNK)
            osum_sc[rows, :] += _dot_nt(qt_sc[d, rows, :], st[d].astype(BF16))
            kv = _dot_tn(v_ref[rows, :].astype(BF16), ke_sc[d, rows, :])
            st[d] = st[d] * dec_sc[d, cc * GLA_CHUNK:cc * GLA_CHUNK + 1, :] + kv
    if want_state:
        for d in range(2):
            sfin_ref[d] = st[d].T

    for rb in range(seq // GLA_RB):
        rows = slice(rb * GLA_RB, (rb + 1) * GLA_RB)
        o = osum_sc[rows, :]
        o = o * lax.rsqrt(jnp.mean(o * o, axis=-1, keepdims=True) + EPS) * ng_ref[...]
        o_ref[rows, :] = (o * _silu(g_ref[rows, :])).astype(BF16)


def _gla(proj, q_col, glr, w2p, gate_b, norm_g, s0, *, nb, seq, heads, want_state):
    qk = w2p.shape[-1] // 2
    dk = qk // heads
    dv = norm_g.shape[1]
    vdim = dv * heads
    zero_init = s0 is None
    hp = GLA_HPS
    bk, bv = hp * dk, hp * dv
    assert heads % hp == 0 and q_col % bk == 0 and qk % bk == 0 and (q_col + 2 * qk) % bv == 0 and vdim % bv == 0
    in_specs = [pl.BlockSpec((seq, bk), lambda b, h: (b, q_col // bk + h)),
                pl.BlockSpec((seq, bk), lambda b, h: (b, (q_col + qk) // bk + h)),
                pl.BlockSpec((seq, bv), lambda b, h: (b, (q_col + 2 * qk) // bv + h)),
                pl.BlockSpec((seq, bv), lambda b, h: (b, (q_col + 2 * qk + vdim) // bv + h)),
                pl.BlockSpec((seq, glr.shape[1]), lambda b, h: (b, 0)),
                pl.BlockSpec((w2p.shape[0], 2 * bk), lambda b, h: (0, h)),
                pl.BlockSpec((1, 2 * bk), lambda b, h: (0, h)),
                pl.BlockSpec((1, dv), lambda b, h: (0, 0))]
    args = [proj, proj, proj, proj, glr, w2p, gate_b, norm_g]
    if not zero_init:
        in_specs.append(pl.BlockSpec((None, 2, hp, dk, dv), lambda b, h: (b, 0, h, 0, 0)))
        args.append(s0)
    out_shape = [jax.ShapeDtypeStruct((nb * seq, vdim), BF16)]
    out_specs = [pl.BlockSpec((seq, bv), lambda b, h: (b, h))]
    if want_state:
        out_shape.append(jax.ShapeDtypeStruct((nb, 2, heads, dk, dv), F32))
        out_specs.append(pl.BlockSpec((None, 2, hp, dk, dv), lambda b, h: (b, 0, h, 0, 0)))
    res = pl.pallas_call(
        functools.partial(_gla_kernel, seq=seq, zero_init=zero_init, want_state=want_state),
        grid=(nb, heads // hp), in_specs=in_specs, out_specs=out_specs, out_shape=out_shape,
        scratch_shapes=[pltpu.VMEM((hp, 2, seq, dk), BF16), pltpu.VMEM((hp, 2, seq, dk), BF16),
                        pltpu.VMEM((hp, 2, seq, dk), F32), pltpu.VMEM((hp, seq, dv), F32)],
        compiler_params=_cparams("arbitrary", "arbitrary"), name="gla",
    )(*args)
    return (res[0], res[1]) if want_state else (res[0], None)


def _mixer_residual(x_ref, mod_ref, ng_ref, y, out_ref, h_ref):
    xn = x_ref[...] + mod_ref[5:6, :] * y
    out_ref[...] = xn
    h_ref[...] = _rms_mod(xn, ng_ref[...], mod_ref[7:8, :], mod_ref[6:7, :]).astype(BF16)


def _even_out_kernel(y_ref, u_ref, o_ref, x_ref, mod_ref, sd_ref, gw_ref, gb_ref, wo_ref, ng_ref, out_ref, h_ref):
    sw = u_ref.shape[1]
    ys = _gelu(y_ref[...] + sd_ref[...] * u_ref[...])
    ys = ys * _sigmoid(_dot(ys.astype(BF16), gw_ref[...]) + gb_ref[...])
    y = _dot(ys.astype(BF16), wo_ref[:sw, :]) + _dot(o_ref[...], wo_ref[sw:, :])
    _mixer_residual(x_ref, mod_ref, ng_ref, y, out_ref, h_ref)


def _even_out(y_s5, proj, o_gla, x, mod_l, s5_d, glu_w, glu_b, w_out, norm_next, *, t, row_base, pseq_per_cond):
    m, d = x.shape
    sw = y_s5.shape[1]
    return pl.pallas_call(
        _even_out_kernel, grid=(NPS,),
        in_specs=[pl.BlockSpec((t, sw), lambda i: (i, 0)),
                  pl.BlockSpec((t, sw), lambda i: (i, 0)),
                  pl.BlockSpec((t, o_gla.shape[1]), lambda i: (i, 0)),
                  pl.BlockSpec((t, d), lambda i: (i, 0)),
                  pl.BlockSpec((None, N_MOD, d), lambda i: (row_base + i // pseq_per_cond, 0, 0)),
                  pl.BlockSpec((1, sw), lambda i: (0, 0)),
                  pl.BlockSpec((sw, sw), lambda i: (0, 0)),
                  pl.BlockSpec((1, sw), lambda i: (0, 0)),
                  pl.BlockSpec(w_out.shape, lambda i: (0, 0)),
                  pl.BlockSpec((1, d), lambda i: (0, 0))],
        out_specs=[pl.BlockSpec((t, d), lambda i: (i, 0)), pl.BlockSpec((t, d), lambda i: (i, 0))],
        out_shape=[jax.ShapeDtypeStruct((m, d), F32), jax.ShapeDtypeStruct((m, d), BF16)],
        compiler_params=_cparams("arbitrary"), name="even_out",
    )(y_s5, proj, o_gla, x, mod_l, s5_d, glu_w, glu_b, w_out, norm_next)


LRU_TB = 16
CONV_W = 4
CONV_LEFT = 2
LRU_UNROLL = 8


def _lru_kernel(x_ref, cw_ref, cb_ref, wa_ref, ba_ref, wx_ref, bx_ref, lam_ref, h0_ref, hs_ref, *rest,
                t, glen, nseg, segmented):
    if segmented:
        xp_sc, xc_sc, a_sc, b_sc, hs_sc, e_sc, p_sc, s_sc = rest
    else:
        stout_ref, xp_sc, xc_sc, a_sc, b_sc, hs_sc = rest
    r, w = x_ref.shape
    rb = LRU_TB * NPS
    pad = CONV_LEFT * NPS
    perm = _perm_matrix(NPS, LRU_TB)
    perm_back = _perm_matrix(LRU_TB, NPS)

    for g in range(t // glen + 1):
        xp_sc[g * (glen * NPS + pad):g * (glen * NPS + pad) + pad, :] = jnp.zeros((pad, w), F32)

    def xp_row(i):
        return pl.multiple_of(pad * (1 + (i * LRU_TB) // glen) + i * rb, NPS)

    def load_tm(i, c):
        xin = jnp.concatenate([x_ref[pl.ds(pl.multiple_of(b * t + i * LRU_TB, LRU_TB), LRU_TB), :]
                               for b in range(NPS)], axis=0)
        xp_sc[pl.ds(xp_row(i), rb), :] = _permute_rows_f32(perm, xin)
        return c
    lax.fori_loop(0, t // LRU_TB, load_tm, 0, unroll=2)

    def conv(i, c):
        acc = jnp.zeros((rb, w), F32) + cb_ref[...]
        for kk in range(CONV_W):
            xs = xp_sc[pl.ds(pl.multiple_of(xp_row(i) + (kk - CONV_LEFT) * NPS, NPS), rb), :]
            acc = acc + xs * cw_ref[kk:kk + 1, :]
        xc_sc[pl.ds(pl.multiple_of(i * rb, rb), rb), :] = acc
        return c
    lax.fori_loop(0, r // rb, conv, 0)

    c2 = [(-0.25 * LRU_C) * _softplus(-lam_ref[d]) for d in range(2)]

    def gates(i, c):
        rows = pl.ds(pl.multiple_of(i * rb, rb), rb)
        xc = xc_sc[rows, :]
        xb = xc.astype(BF16)
        for d in range(2):
            tr = jnp.tanh(_dot(xb, wa_ref[d]) + ba_ref[d])
            ti = jnp.tanh(_dot(xb, wx_ref[d]) + bx_ref[d])
            th = jnp.tanh(c2[d] + c2[d] * tr)
            rcp = 1.0 / (1.0 - th)
            a_sc[d, rows, :] = (1.0 + th) * rcp
            b_sc[d, rows, :] = rcp * jnp.sqrt(-th) * ((1.0 + ti) * xc)
        return c
    lax.fori_loop(0, r // rb, gates, 0)

    for d in range(2):
        def trow(s):
            return pl.ds(pl.multiple_of((s if d == 0 else t - 1 - s) * NPS, NPS), NPS)

        if segmented:
            def sweep(s, carry):
                h, p = carry
                rows = trow(s)
                a = a_sc[d, rows, :]
                return a * h + b_sc[d, rows, :], a * p
            e, p = lax.fori_loop(0, t, sweep, (jnp.zeros((NPS, w), F32), jnp.ones((NPS, w), F32)),
                                 unroll=LRU_UNROLL)
            e_sc[...] = e
            p_sc[...] = p
            nb = NPS // nseg
            for b in range(nb):
                prev = h0_ref[d, b:b + 1, :]
                for k in (range(nseg) if d == 0 else range(nseg - 1, -1, -1)):
                    j = b * nseg + k
                    s_sc[j:j + 1, :] = prev
                    prev = p_sc[j:j + 1, :] * prev + e_sc[j:j + 1, :]
            h_init = s_sc[...]
        else:
            h_init = h0_ref[d]

        def scan(s, h):
            rows = trow(s)
            h = a_sc[d, rows, :] * h + b_sc[d, rows, :]
            if d == 0:
                hs_sc[rows, :] = h
            else:
                hs_sc[rows, :] += h
            return h
        h_fin = lax.fori_loop(0, t, scan, h_init, unroll=LRU_UNROLL)
        if not segmented:
            stout_ref[d] = h_fin

    def store_bm(i, c):
        hb = _permute_rows_f32(perm_back, hs_sc[pl.ds(pl.multiple_of(i * rb, rb), rb), :])
        for b in range(NPS):
            hs_ref[pl.ds(pl.multiple_of(b * t + i * LRU_TB, LRU_TB), LRU_TB), :] = hb[b * LRU_TB:(b + 1) * LRU_TB]
        return c
    lax.fori_loop(0, t // LRU_TB, store_bm, 0, unroll=2)


def _lru(proj, x_col, width, conv_w, conv_b, wa, ba, wx, bx, lam, h0, *, t, glen, nseg):
    segmented = nseg > 1
    r = proj.shape[0]
    heads, blk = wa.shape[1], wa.shape[2]
    col = lambda hd: (0, hd)
    col3 = lambda hd: (0, 0, hd)
    out_shape = [jax.ShapeDtypeStruct((r, width), F32)]
    out_specs = [pl.BlockSpec((r, blk), col)]
    scratch = [pltpu.VMEM((r + (t // glen + 1) * CONV_LEFT * NPS, blk), F32), pltpu.VMEM((r, blk), F32),
               pltpu.VMEM((2, r, blk), F32), pltpu.VMEM((2, r, blk), F32), pltpu.VMEM((r, blk), F32)]
    if segmented:
        scratch += [pltpu.VMEM((NPS, blk), F32)] * 3
    else:
        out_shape.append(jax.ShapeDtypeStruct((2, NPS, width), F32))
        out_specs.append(pl.BlockSpec((2, NPS, blk), col3))
    res = pl.pallas_call(
        functools.partial(_lru_kernel, t=t, glen=glen, nseg=nseg, segmented=segmented),
        grid=(heads,),
        in_specs=[pl.BlockSpec((r, blk), lambda hd: (0, x_col // blk + hd)),
                  pl.BlockSpec((CONV_W, blk), col),
                  pl.BlockSpec((1, blk), col),
                  pl.BlockSpec((2, None, blk, blk), lambda hd: (0, hd, 0, 0)),
                  pl.BlockSpec((2, 1, blk), col3),
                  pl.BlockSpec((2, None, blk, blk), lambda hd: (0, hd, 0, 0)),
                  pl.BlockSpec((2, 1, blk), col3),
                  pl.BlockSpec((2, 1, blk), col3),
                  pl.BlockSpec((2, h0.shape[1], blk), col3)],
        out_specs=out_specs, out_shape=out_shape, scratch_shapes=scratch,
        compiler_params=_cparams("arbitrary"), name="lru",
    )(proj, conv_w, conv_b, wa, ba, wx, bx, lam, h0)
    return (res[0], res[1]) if not segmented else (res[0], None)


def _odd_out_kernel(hs_ref, gate_ref, x_ref, mod_ref, wo_ref, ng_ref, out_ref, h_ref):
    y = _dot((hs_ref[...] * _gelu(gate_ref[...])).astype(BF16), wo_ref[...])
    _mixer_residual(x_ref, mod_ref, ng_ref, y, out_ref, h_ref)


def _odd_out(hs, proj, x, mod_l, w_out, norm_next, *, t, row_base, pseq_per_cond):
    m, d = x.shape
    w = hs.shape[1]
    return pl.pallas_call(
        _odd_out_kernel, grid=(NPS,),
        in_specs=[pl.BlockSpec((t, w), lambda i: (i, 0)),
                  pl.BlockSpec((t, w), lambda i: (i, 0)),
                  pl.BlockSpec((t, d), lambda i: (i, 0)),
                  pl.BlockSpec((None, N_MOD, d), lambda i: (row_base + i // pseq_per_cond, 0, 0)),
                  pl.BlockSpec(w_out.shape, lambda i: (0, 0)),
                  pl.BlockSpec((1, d), lambda i: (0, 0))],
        out_specs=[pl.BlockSpec((t, d), lambda i: (i, 0)), pl.BlockSpec((t, d), lambda i: (i, 0))],
        out_shape=[jax.ShapeDtypeStruct((m, d), F32), jax.ShapeDtypeStruct((m, d), BF16)],
        compiler_params=_cparams("arbitrary"), name="odd_out",
    )(hs, proj, x, mod_l, w_out, norm_next)


def kernel(x_prompt, x_sample, state_s5_re, state_s5_im, state_gla, state_lru, c, c_ctx, norm_g, ada_w, ada_b, ffn_w_in, ffn_w_out, final_norm_g, ev_w_in, ev_w_out, s5_lam_re, s5_lam_im, s5_log_step, s5_b_re, s5_b_im, s5_c_re, s5_c_im, s5_d, s5_glu_w, s5_glu_b, gla_gate_w2, gla_gate_b, gla_norm_g, od_w_in, od_w_out, lru_conv_w, lru_conv_b, lru_wa, lru_ba, lru_wx, lru_bx, lru_lam):
    nbc, seq, d = x_prompt.shape
    nbl, dseq, _ = x_sample.shape
    depth = norm_g.shape[0]
    assert nbc == NPS and NPS % nbl == 0
    nseg = NPS // nbl
    tl = dseq // nseg
    grid_w = 64
    assert tl % grid_w == 0 and tl % LRU_TB == 0 and seq % LRU_TB == 0 and seq % GLA_RB == 0 and dseq % GLA_RB == 0
    sw = s5_d.shape[1]
    qk = gla_gate_w2.shape[-1]
    heads = state_gla.shape[3]
    rank = gla_gate_w2.shape[2]
    g5, p5 = s5_lam_re.shape[2], s5_lam_re.shape[3]
    main = ev_w_in.shape[2] - 2 * rank
    assert main % PROJ_TN == 0 and (g5 // S5_GP) % S5_PPG == 0

    passes = [dict(x=x_prompt.reshape(nbc * seq, d), base=0, rpc=nbc * seq, ppc=NPS, t=seq, nseg=1,
                   glen=seq, nb=nbc, seq=seq),
              dict(x=x_sample.reshape(nbl * dseq, d), base=1, rpc=dseq, ppc=nseg, t=tl, nseg=nseg,
                   glen=grid_w, nb=nbl, seq=dseq)]

    cond8 = jnp.concatenate([c_ctx[None, :], c, jnp.zeros((8 - 1 - nbl, d), F32)], axis=0)
    mod = _ada(cond8, ada_w, ada_b).reshape(depth, 8, N_MOD, d)
    fg = final_norm_g.reshape(1, d)

    new_s5_re, new_s5_im, new_gla, new_lru = [], [], [], []
    for l in range(depth):
        mod_l = mod[l]
        ng = lambda s: norm_g[l, s].reshape(1, d)
        for ps in passes:
            ps['x'] = _ffn(ps['x'], mod_l, ng(0), fg, ffn_w_in, ffn_w_out, l, 0, mi=0,
                           row_base=ps['base'], rows_per_cond=ps['rpc'], final=False)
        if l % 2 == 0:
            e = l // 2
            w_glr = jnp.pad(ev_w_in[e][:, main:], ((0, 0), (0, 128 - 2 * rank))).astype(BF16)
            w_o = ev_w_out[e].astype(BF16)
            glu_w = s5_glu_w[e].astype(BF16)
            dkh = qk // heads
            w2p = jnp.stack([jnp.pad(gla_gate_w2[e, dd], ((dd * rank, 128 - (dd + 1) * rank), (0, 0)))
                             for dd in range(2)])
            w2p = w2p.reshape(2, 128, heads, dkh).transpose(1, 2, 0, 3).reshape(128, 2 * qk).astype(BF16)
            gate_b = gla_gate_b[e].reshape(2, heads, dkh).transpose(1, 0, 2).reshape(1, 2 * qk)
            w1, w2t, kx, a_bc, a_ts = _s5_params(s5_lam_re[e], s5_lam_im[e], s5_log_step[e], s5_b_re[e], s5_b_im[e],
                                                 s5_c_re[e], s5_c_im[e], [ps['t'] for ps in passes])
            for pi, ps in enumerate(passes):
                t = ps['t']
                proj, glr = _norm_proj(ps['x'], mod_l, ng(1), ev_w_in, e, w_glr, mi=3,
                                       row_base=ps['base'], rows_per_cond=ps['rpc'])
                if pi == 0:
                    h0 = jnp.zeros((2, NPS, a_bc.shape[-1]), F32)
                    s0 = None
                else:
                    h0 = _s5_state_to_cols(state_s5_re[:, e], state_s5_im[:, e])
                    s0 = state_gla[:, e]
                y_s5, s5_fin = _s5(proj, sw, w1, w2t, kx, a_bc, a_ts[pi], h0, t=t, nseg=ps['nseg'])
                o_gla, gla_fin = _gla(proj, sw, glr, w2p, gate_b, gla_norm_g[e].reshape(1, -1), s0,
                                      nb=ps['nb'], seq=ps['seq'], heads=heads, want_state=(pi == 0))
                if pi == 0:
                    sr, si = _s5_cols_to_state(s5_fin, g5, p5)
                    new_s5_re.append(sr)
                    new_s5_im.append(si)
                    new_gla.append(gla_fin)
                ps['x'], ps['h'] = _even_out(y_s5, proj, o_gla, ps['x'], mod_l, s5_d[e].reshape(1, sw), glu_w,
                                             s5_glu_b[e].reshape(1, sw), w_o, ng(2), t=t, row_base=ps['base'],
                                             pseq_per_cond=ps['ppc'])
        else:
            o = l // 2
            lw = od_w_in.shape[2] // 2
            w_o = od_w_out[o].astype(BF16)
            wa16 = (0.5 * lru_wa[o]).astype(BF16)
            wx16 = (0.5 * lru_wx[o]).astype(BF16)
            for pi, ps in enumerate(passes):
                t = ps['t']
                proj, _ = _norm_proj(ps['x'], mod_l, ng(1), od_w_in, o, None, mi=3,
                                     row_base=ps['base'], rows_per_cond=ps['rpc'])
                if pi == 0:
                    h0 = jnp.zeros((2, NPS, lw), F32)
                else:
                    h0 = state_lru[:, o].transpose(1, 0, 2)
                hs, lru_fin = _lru(proj, lw, lw, lru_conv_w[o], lru_conv_b[o].reshape(1, lw), wa16,
                                   0.5 * lru_ba[o].reshape(2, 1, lw), wx16, 0.5 * lru_bx[o].reshape(2, 1, lw),
                                   lru_lam[o].reshape(2, 1, lw), h0, t=t, glen=ps['glen'], nseg=ps['nseg'])
                if pi == 0:
                    new_lru.append(lru_fin.transpose(1, 0, 2))
                ps['x'], ps['h'] = _odd_out(hs, proj, ps['x'], mod_l, w_o, ng(2), t=t, row_base=ps['base'],
                                            pseq_per_cond=ps['ppc'])
        for ps in passes:
            ps['x'] = _ffn(ps['x'], mod_l, ng(2), fg, ffn_w_in, ffn_w_out, l, 1, mi=6, row_base=ps['base'],
                           rows_per_cond=ps['rpc'], final=(l == depth - 1), h_pre=ps['h'])

    y_prompt = passes[0]['x'].reshape(nbc, seq, d)
    y_sample = passes[1]['x'].reshape(nbl, dseq, d)
    return (y_prompt, y_sample, jnp.stack(new_s5_re, 1), jnp.stack(new_s5_im, 1),
            jnp.stack(new_gla, 1), jnp.stack(new_lru, 1))
```

```python
import functools
import math

import jax
import jax.numpy as jnp
from jax import lax
from jax.experimental import pallas as pl
from jax.experimental.pallas import tpu as pltpu

F32 = jnp.float32
BF16 = jnp.bfloat16
EPS = 1e-6
NPS = 16
N_MOD = 9
GLA_CHUNK = 64
GLA_TAU = 16.0
LRU_C = 8.0
VMEM_LIMIT = 58 * 1024 * 1024
ROW_CHUNK = 64
NORM_CHUNK = 64
NORM_UNROLL = 2


def _cparams(*sem):
    return pltpu.CompilerParams(dimension_semantics=sem, vmem_limit_bytes=VMEM_LIMIT)


def _dot(a, b):
    return jnp.dot(a, b, preferred_element_type=F32)


def _dot_nt(a, b):
    return lax.dot_general(a, b, (((1,), (1,)), ((), ())), preferred_element_type=F32)


def _dot_tn(a, b):
    return lax.dot_general(a, b, (((0,), (0,)), ((), ())), preferred_element_type=F32)


def _sigmoid(x):
    return 0.5 * (1.0 + jnp.tanh(0.5 * x))


def _silu(x):
    return x * _sigmoid(x)


def _gelu(x):
    return 0.5 * x * (1.0 + jnp.tanh(math.sqrt(2.0 / math.pi) * (x + 0.044715 * (x * x * x))))


def _softplus(x):
    return jnp.maximum(x, 0.0) + jnp.log1p(jnp.exp(-jnp.abs(x)))


def _rms_mod(x, g, scale, shift):
    return x * lax.rsqrt(jnp.mean(x * x, axis=-1, keepdims=True) + EPS) * (g * (1.0 + scale)) + shift


def _perm_matrix(n_outer, n_inner):
    n = n_outer * n_inner
    ro = lax.broadcasted_iota(jnp.int32, (n, n), 0)
    ci = lax.broadcasted_iota(jnp.int32, (n, n), 1)
    return jnp.where(ci == (ro % n_outer) * n_inner + ro // n_outer, 1.0, 0.0).astype(BF16)


def _permute_rows_f32(perm, x):
    hi = x.astype(BF16)
    r1 = x - hi.astype(F32)
    mid = r1.astype(BF16)
    lo = (r1 - mid.astype(F32)).astype(BF16)
    return _dot(perm, hi) + _dot(perm, mid) + _dot(perm, lo)


def _norm_rows(x_ref, mod_ref, g_ref, h_sc, mi):
    def body(r, c):
        rows = pl.ds(pl.multiple_of(r * NORM_CHUNK, NORM_CHUNK), NORM_CHUNK)
        h = _rms_mod(x_ref[rows, :], g_ref[...], mod_ref[mi + 1:mi + 2, :], mod_ref[mi:mi + 1, :])
        h_sc[rows, :] = h.astype(BF16)
        return c
    lax.fori_loop(0, x_ref.shape[0] // NORM_CHUNK, body, 0, unroll=NORM_UNROLL)


def _ada_kernel(c_ref, w_ref, b_ref, o_ref):
    ca = _silu(c_ref[...])
    o_ref[...] = _dot(ca.astype(BF16), w_ref[...].astype(BF16)) + b_ref[...]


def _ada(cond8, ada_w, ada_b):
    depth, d, n = ada_w.shape
    tn = 1024
    return pl.pallas_call(
        _ada_kernel, grid=(depth, n // tn),
        in_specs=[pl.BlockSpec((8, d), lambda l, j: (0, 0)),
                  pl.BlockSpec((None, d, tn), lambda l, j: (l, 0, j)),
                  pl.BlockSpec((None, 1, tn), lambda l, j: (l, 0, j))],
        out_specs=pl.BlockSpec((None, 8, tn), lambda l, j: (l, 0, j)),
        out_shape=jax.ShapeDtypeStruct((depth, 8, n), F32),
        compiler_params=_cparams("arbitrary", "arbitrary"), name="ada",
    )(cond8, ada_w, ada_b.reshape(depth, 1, n))


FFN_TM = 1024
FFN_TF = 256
FFN_NC = 512


def _ffn_kernel(x_ref, mod_ref, g_ref, fg_ref, wa_ref, wb_ref, wo_ref, *rest, mi, nf, final, has_h):
    h_ref, o_ref = rest if has_h else rest[::-1]
    j = pl.program_id(1)
    tm, d = x_ref.shape

    @pl.when(j == 0)
    def _():
        if not has_h:
            _norm_rows(x_ref, mod_ref, g_ref, h_ref, mi)
        o_ref[...] = jnp.zeros_like(o_ref)

    h = h_ref[...]
    a = _dot(h, wa_ref[...].astype(BF16))
    b = _dot(h, wb_ref[...].astype(BF16))
    act = (_silu(a) * b).astype(BF16)
    for n in range(d // FFN_NC):
        cols = slice(n * FFN_NC, (n + 1) * FFN_NC)
        o_ref[:, cols] += _dot(act, wo_ref[:, cols].astype(BF16))

    @pl.when(j == nf - 1)
    def _():
        def body(r, c):
            rows = pl.ds(pl.multiple_of(r * ROW_CHUNK, ROW_CHUNK), ROW_CHUNK)
            y = x_ref[rows, :] + 0.5 * mod_ref[mi + 2:mi + 3, :] * o_ref[rows, :]
            if final:
                y = y * lax.rsqrt(jnp.mean(y * y, axis=-1, keepdims=True) + EPS) * fg_ref[...]
            o_ref[rows, :] = y
            return c
        lax.fori_loop(0, tm // ROW_CHUNK, body, 0)


def _ffn(x, mod_l, norm_g, final_g, w_in, w_out, l, k, *, mi, row_base, rows_per_cond, final, h_pre=None):
    m, d = x.shape
    f = w_out.shape[2]
    tm = min(FFN_TM, rows_per_cond)
    tf = FFN_TF
    nf = f // tf
    row = lambda i, j: (row_base + (i * tm) // rows_per_cond, 0, 0)
    once = dict(pipeline_mode=pl.Buffered(1))
    in_specs = [pl.BlockSpec((tm, d), lambda i, j: (i, 0), **once),
                pl.BlockSpec((None, N_MOD, d), row),
                pl.BlockSpec((1, d), lambda i, j: (0, 0)),
                pl.BlockSpec((1, d), lambda i, j: (0, 0)),
                pl.BlockSpec((None, None, d, tf), lambda i, j: (l, k, 0, j)),
                pl.BlockSpec((None, None, d, tf), lambda i, j: (l, k, 0, j + nf)),
                pl.BlockSpec((None, None, tf, d), lambda i, j: (l, k, j, 0))]
    args = [x, mod_l, norm_g, final_g, w_in, w_in, w_out]
    if h_pre is not None:
        in_specs.append(pl.BlockSpec((tm, d), lambda i, j: (i, 0), **once))
        args.append(h_pre)
    return pl.pallas_call(
        functools.partial(_ffn_kernel, mi=mi, nf=nf, final=final, has_h=h_pre is not None),
        grid=(m // tm, nf), in_specs=in_specs,
        out_specs=pl.BlockSpec((tm, d), lambda i, j: (i, 0)),
        out_shape=jax.ShapeDtypeStruct((m, d), F32),
        scratch_shapes=[] if h_pre is not None else [pltpu.VMEM((tm, d), BF16)],
        compiler_params=_cparams("arbitrary", "arbitrary"), name="ffn",
    )(*args)


PROJ_TM = 1024
PROJ_TN = 1024


def _norm_proj_kernel(x_ref, mod_ref, g_ref, w_ref, *rest, mi, nmain):
    if len(rest) == 4:
        wx_ref, o_ref, ox_ref, h_sc = rest
    else:
        (o_ref, h_sc), wx_ref, ox_ref = rest, None, None
    j = pl.program_id(1)

    @pl.when(j == 0)
    def _():
        _norm_rows(x_ref, mod_ref, g_ref, h_sc, mi)

    @pl.when(j < nmain)
    def _():
        o_ref[...] = _dot(h_sc[...], w_ref[...].astype(BF16))

    if wx_ref is not None:
        @pl.when(j == nmain)
        def _():
            ox_ref[...] = _dot(h_sc[...], wx_ref[...])


def _norm_proj(x, mod_l, norm_g, w, e, w_extra, *, mi, row_base, rows_per_cond):
    m, d = x.shape
    tm = min(PROJ_TM, rows_per_cond)
    tn = PROJ_TN
    nmain = w.shape[2] // tn
    nj = nmain + (0 if w_extra is None else 1)
    jm = lambda j: jnp.minimum(j, nmain - 1)
    in_specs = [pl.BlockSpec((tm, d), lambda i, j: (i, 0)),
                pl.BlockSpec((None, N_MOD, d), lambda i, j: (row_base + (i * tm) // rows_per_cond, 0, 0)),
                pl.BlockSpec((1, d), lambda i, j: (0, 0)),
                pl.BlockSpec((None, d, tn), lambda i, j: (e, 0, jm(j)))]
    args = [x, mod_l, norm_g, w]
    out_shape = [jax.ShapeDtypeStruct((m, nmain * tn), F32)]
    out_specs = [pl.BlockSpec((tm, tn), lambda i, j: (i, jm(j)))]
    if w_extra is not None:
        in_specs.append(pl.BlockSpec(w_extra.shape, lambda i, j: (0, 0)))
        args.append(w_extra)
        out_shape.append(jax.ShapeDtypeStruct((m, w_extra.shape[1]), F32))
        out_specs.append(pl.BlockSpec((tm, w_extra.shape[1]), lambda i, j: (i, 0)))
    res = pl.pallas_call(
        functools.partial(_norm_proj_kernel, mi=mi, nmain=nmain), grid=(m // tm, nj),
        in_specs=in_specs, out_specs=out_specs, out_shape=out_shape,
        scratch_shapes=[pltpu.VMEM((tm, d), BF16)],
        compiler_params=_cparams("arbitrary", "arbitrary"), name="norm_proj",
    )(*args)
    return res if w_extra is not None else (res[0], None)


S5_S = 8
S5_GP = 2
S5_PPG = 4
S5_UNROLL = 4


def _s5_kernel(u_ref, pc_ref, w1_ref, w2t_ref, kx_ref, a_ref, at_ref, h0_ref, y_ref, *rest, t, nseg, segmented):
    if segmented:
        g_sc, v_sc, st_sc, w3_sc = rest
    else:
        stout_ref, g_sc, v_sc, st_sc, w3_sc = rest
    nblk = t // S5_S
    npair = w1_ref.shape[1]
    cw = w1_ref.shape[2]
    hw = cw // 2
    lt = u_ref.shape[1]
    sw = cw // S5_S

    jcol = lax.broadcasted_iota(jnp.int32, (sw, cw), 1) // sw
    for q in range(npair):
        for i in range(S5_S):
            acc = jnp.zeros((sw, cw), F32)
            for lag in range(S5_S - i):
                acc = acc + jnp.where(jcol == i + lag, kx_ref[0, lag, q], 0.0)
            for lag in range(i + 1):
                acc = acc + jnp.where(jcol == i - lag, kx_ref[1, lag, q], 0.0)
            w3_sc[q, i * sw:(i + 1) * sw, :] = acc.astype(BF16)

    def gather(k, c):
        r = pl.ds(pl.multiple_of(k * NPS, NPS), NPS)
        for j in range(S5_S):
            g_sc[r, j * lt:(j + 1) * lt] = u_ref[pl.ds(k * S5_S + j, NPS, stride=t), :]
        return c
    lax.fori_loop(0, nblk, gather, 0)
    u = _dot(g_sc[...].astype(BF16), pc_ref[...]).astype(BF16)

    for d in range(2):
        for q in range(npair):
            v_sc[d, :, q * cw:(q + 1) * cw] = _dot(u[:, q * cw:(q + 1) * cw], w1_ref[d, q])

        def advance(state, v):
            outs = []
            for q in range(npair):
                re = slice(q * cw, q * cw + hw)
                im = slice(q * cw + hw, (q + 1) * cw)
                ar, ai = a_ref[d, :, re], a_ref[d, :, im]
                hr, hi = state[:, re], state[:, im]
                outs.append(ar * hr - ai * hi + v[:, re])
                outs.append(ar * hi + ai * hr + v[:, im])
            return jnp.concatenate(outs, axis=1)

        def rows(kk):
            k = kk if d == 0 else nblk - 1 - kk
            return pl.ds(pl.multiple_of(k * NPS, NPS), NPS)

        if segmented:
            st_sc[...] = lax.fori_loop(0, nblk, lambda kk, s: advance(s, v_sc[d, rows(kk), :]),
                                       jnp.zeros(st_sc.shape, F32), unroll=S5_UNROLL)
            nb = NPS // nseg
            for b in range(nb):
                for q in range(npair):
                    re = slice(q * cw, q * cw + hw)
                    im = slice(q * cw + hw, (q + 1) * cw)
                    pr, pi = h0_ref[d, b:b + 1, re], h0_ref[d, b:b + 1, im]
                    ar, ai = at_ref[d, :, re], at_ref[d, :, im]
                    for k in (range(nseg) if d == 0 else range(nseg - 1, -1, -1)):
                        j = b * nseg + k
                        er, ei = st_sc[j:j + 1, re], st_sc[j:j + 1, im]
                        st_sc[j:j + 1, re] = pr
                        st_sc[j:j + 1, im] = pi
                        pr, pi = ar * pr - ai * pi + er, ar * pi + ai * pr + ei
            init = st_sc[...]
        else:
            init = h0_ref[d]

        def body(kk, s):
            r = rows(kk)
            v = v_sc[d, r, :]
            v_sc[d, r, :] = s
            return advance(s, v)
        fin = lax.fori_loop(0, nblk, body, init, unroll=S5_UNROLL)
        if not segmented:
            stout_ref[d] = fin

    for q in range(npair):
        cols = slice(q * cw, (q + 1) * cw)
        hcat = jnp.concatenate([v_sc[0, :, cols], v_sc[1, :, cols]], axis=1).astype(BF16)
        g_sc[:, cols] = _dot_nt(hcat, w2t_ref[q]) + _dot(u[:, cols], w3_sc[q])

    y = g_sc[...]
    hi = y.astype(BF16)
    lo = (y - hi.astype(F32)).astype(BF16)
    g_sc[...] = _dot_nt(hi, pc_ref[...]) + _dot_nt(lo, pc_ref[...])

    def scatter(k, c):
        r = pl.ds(pl.multiple_of(k * NPS, NPS), NPS)
        for j in range(S5_S):
            y_ref[pl.ds(k * S5_S + j, NPS, stride=t), :] = g_sc[r, j * lt:(j + 1) * lt]
        return c
    lax.fori_loop(0, nblk, scatter, 0)


def _s5_col_perm(lt, gw):
    r = jnp.arange(S5_S * lt)
    j, lane = r // lt, r % lt
    dst = (lane // gw) * (S5_S * gw) + j * gw + lane % gw
    return (dst[:, None] == jnp.arange(S5_S * lt)[None, :]).astype(BF16)


def _s5(proj, width, w1, w2t, kx, a_bc, a_t, h0, *, t, nseg):
    segmented = nseg > 1
    m = proj.shape[0]
    npair, cw = w1.shape[1], w1.shape[2]
    sl = S5_PPG * cw
    lt = sl // S5_S
    r = (t // S5_S) * NPS
    pc = _s5_col_perm(lt, cw // S5_S)
    col2 = lambda g: (0, g)
    col3 = lambda g: (0, 0, g)
    out_shape = [jax.ShapeDtypeStruct((m, width), F32)]
    out_specs = [pl.BlockSpec((m, lt), col2)]
    if not segmented:
        out_shape.append(jax.ShapeDtypeStruct((2, NPS, npair * cw), F32))
        out_specs.append(pl.BlockSpec((2, NPS, sl), col3))
    res = pl.pallas_call(
        functools.partial(_s5_kernel, t=t, nseg=nseg, segmented=segmented),
        grid=(npair // S5_PPG,),
        in_specs=[pl.BlockSpec((m, lt), col2),
                  pl.BlockSpec(pc.shape, lambda g: (0, 0)),
                  pl.BlockSpec((2, S5_PPG, cw, cw), lambda g: (0, g, 0, 0)),
                  pl.BlockSpec((S5_PPG, cw, 2 * cw), lambda g: (g, 0, 0)),
                  pl.BlockSpec((2, S5_S, S5_PPG, cw // S5_S, cw), lambda g: (0, 0, g, 0, 0)),
                  pl.BlockSpec((2, NPS, sl), col3),
                  pl.BlockSpec((2, 1, sl), col3),
                  pl.BlockSpec((2, h0.shape[1], sl), col3)],
        out_specs=out_specs, out_shape=out_shape,
        scratch_shapes=[pltpu.VMEM((r, sl), F32), pltpu.VMEM((2, r, sl), F32), pltpu.VMEM((NPS, sl), F32),
                        pltpu.VMEM((S5_PPG, cw, cw), BF16)],
        compiler_params=_cparams("arbitrary"), name="s5",
    )(proj, pc, w1, w2t, kx, a_bc, a_t, h0)
    return (res[0], res[1]) if not segmented else (res[0], None)


def _s5_params(lam_re, lam_im, log_step, b_re, b_im, c_re, c_im, t_segs):
    nd, g, p = lam_re.shape
    n = b_re.shape[-1]
    npair = g // S5_GP
    hi = lax.Precision.HIGHEST
    dt = jnp.exp(log_step)[..., None]
    z_re, z_im = lam_re * dt, lam_im * dt
    mag = jnp.exp(z_re)
    ab_re, ab_im = mag * jnp.cos(z_im), mag * jnp.sin(z_im)
    den = lam_re * lam_re + lam_im * lam_im
    n_re = ab_re - 1.0
    f_re = (n_re * lam_re + ab_im * lam_im) / den
    f_im = (ab_im * lam_re - n_re * lam_im) / den
    bb_re = f_re[..., None] * b_re - f_im[..., None] * b_im
    bb_im = f_re[..., None] * b_im + f_im[..., None] * b_re

    cw = 2 * S5_GP * p
    col = jnp.arange(cw)
    is_re = col < cw // 2
    smask = ((col // p) % S5_GP == jnp.arange(S5_GP)[:, None]).astype(F32)[:, None, :]
    omask = ((col // n) % S5_GP == jnp.arange(S5_GP)[:, None]).astype(F32)[:, None, :]

    def dup(x):
        return jnp.concatenate([x, x], axis=-1)

    zr = dup(z_re.reshape(nd, npair, S5_GP * p))
    zi = dup(z_im.reshape(nd, npair, S5_GP * p))

    def power(e):
        m = jnp.exp(zr[:, :, None, :] * e[:, None, :, None])
        ang = zi[:, :, None, :] * e[:, None, :, None]
        return m * jnp.cos(ang), m * jnp.sin(ang)

    steps = jnp.arange(S5_S, dtype=F32)
    p1r, p1i = power(jnp.stack([S5_S - 1 - steps, steps]))
    p2r, p2i = power(jnp.stack([steps + 1, S5_S - steps]))
    br, bi = [dup(x.reshape(nd, npair, S5_GP, p, n).transpose(0, 1, 4, 2, 3).reshape(nd, npair, n, S5_GP * p))
              for x in (bb_re, bb_im)]
    cr, ci = [dup(x.reshape(nd, npair, S5_GP, n, p).transpose(0, 1, 3, 2, 4).reshape(nd, npair, n, S5_GP * p))
              for x in (c_re, c_im)]
    pj = lambda x: x[:, :, :, None, None, :]
    pn = lambda x: x[:, :, None, None, :, :]
    w1 = jnp.where(is_re, pj(p1r) * pn(br) - pj(p1i) * pn(bi), pj(p1r) * pn(bi) + pj(p1i) * pn(br)) * smask
    w1 = w1.reshape(nd, npair, S5_S * S5_GP * n, cw).astype(BF16)
    w2t = jnp.where(is_re, pj(p2r) * pn(cr) - pj(p2i) * pn(ci), -(pj(p2r) * pn(ci) + pj(p2i) * pn(cr))) * smask
    w2t = w2t.reshape(nd, npair, S5_S * S5_GP * n, cw).astype(BF16)
    w2t = jnp.concatenate([w2t[dd] for dd in range(nd)], axis=-1)

    mag = jnp.exp(z_re[..., None] * steps)
    lr, li = mag * jnp.cos(z_im[..., None] * steps), mag * jnp.sin(z_im[..., None] * steps)
    abt_re = lr[:, :, :, None, :] * bb_re[..., None] - li[:, :, :, None, :] * bb_im[..., None]
    abt_im = lr[:, :, :, None, :] * bb_im[..., None] + li[:, :, :, None, :] * bb_re[..., None]
    reps = (1, 1, S5_S * S5_GP, 1)
    kt = (jnp.einsum('dgcp,dgpmt->dgtmc', jnp.tile(c_re, reps), abt_re, precision=hi)
          - jnp.einsum('dgcp,dgpmt->dgtmc', jnp.tile(c_im, reps), abt_im, precision=hi))
    kx = kt.reshape(nd, npair, S5_GP, S5_S, n, cw).transpose(0, 3, 1, 2, 4, 5) * omask
    kx = kx.reshape(nd, S5_S, npair, S5_GP * n, cw)

    def state_cols(e):
        er, ei = power(jnp.full((nd, 1), e, F32))
        return jnp.where(is_re, er, ei).reshape(nd, npair * cw)

    a_s = state_cols(float(S5_S))
    a_bc = jnp.broadcast_to(a_s[:, None, :], (nd, NPS, a_s.shape[-1]))
    a_ts = [state_cols(float(ts))[:, None, :] for ts in t_segs]
    return w1, w2t, kx, a_bc, a_ts


def _s5_state_to_cols(s_re, s_im):
    b, nd, g, p = s_re.shape
    npair = g // S5_GP
    st = jnp.stack([s_re.reshape(b, nd, npair, S5_GP * p), s_im.reshape(b, nd, npair, S5_GP * p)], axis=3)
    return st.reshape(b, nd, -1).transpose(1, 0, 2)


def _s5_cols_to_state(st, g, p):
    nd, b, _ = st.shape
    npair = g // S5_GP
    st = st.reshape(nd, b, npair, 2, S5_GP, p).transpose(3, 1, 0, 2, 4, 5).reshape(2, b, nd, g, p)
    return st[0], st[1]


GLA_RB = 256


GLA_HPS = 2


def _gla_kernel(q_ref, k_ref, v_ref, g_ref, glr_ref, w2_ref, gb_ref, ng_ref, *rest, seq, zero_init, want_state):
    rest = list(rest)
    s0_ref = None if zero_init else rest.pop(0)
    o_ref = rest.pop(0)
    sfin_ref = rest.pop(0) if want_state else None
    dk = q_ref.shape[1] // GLA_HPS
    dv = v_ref.shape[1] // GLA_HPS
    for hh in range(GLA_HPS):
        lk, lv, l2 = pl.ds(hh * dk, dk), pl.ds(hh * dv, dv), pl.ds(hh * 2 * dk, 2 * dk)
        args = [q_ref.at[:, lk], k_ref.at[:, lk], v_ref.at[:, lv], g_ref.at[:, lv], glr_ref,
                w2_ref.at[:, l2], gb_ref.at[:, l2], ng_ref]
        if not zero_init:
            args.append(s0_ref.at[:, hh])
        args.append(o_ref.at[:, lv])
        if want_state:
            args.append(sfin_ref.at[:, hh])
        _gla_head(*args, *[sc.at[hh] for sc in rest], seq=seq, zero_init=zero_init, want_state=want_state)


def _gla_head(q_ref, k_ref, v_ref, g_ref, glr_ref, w2_ref, gb_ref, ng_ref, *rest, seq, zero_init, want_state):
    rest = list(rest)
    s0_ref = None if zero_init else rest.pop(0)
    o_ref = rest.pop(0)
    sfin_ref = rest.pop(0) if want_state else None
    qt_sc, ke_sc, dec_sc, osum_sc = rest
    dk = q_ref.shape[1]
    dv = v_ref.shape[1]
    n = seq // GLA_CHUNK
    scale = dk ** -0.5
    ri = lax.broadcasted_iota(jnp.int32, (GLA_RB, GLA_RB), 0)
    ci = lax.broadcasted_iota(jnp.int32, (GLA_RB, GLA_RB), 1)
    same = (ri // GLA_CHUNK) == (ci // GLA_CHUNK)
    lower = same & (ci <= ri)
    upper = same & (ci >= ri)
    ones_blk = jnp.where(same, 1.0, 0.0).astype(BF16)
    tri = jnp.where(lower, 1.0, 0.0).astype(BF16)

    for rb in range(seq // GLA_RB):
        rows = slice(rb * GLA_RB, (rb + 1) * GLA_RB)
        x = _dot(glr_ref[rows, :].astype(BF16), w2_ref[...]) + gb_ref[...]
        la = (jnp.minimum(x, 0.0) - jnp.log1p(jnp.exp(-jnp.abs(x)))) * (1.0 / GLA_TAU)
        la_hi = la.astype(BF16)
        la_lo = (la - la_hi.astype(F32)).astype(BF16)
        pre = _dot(tri, la_hi) + _dot(tri, la_lo)
        tot = _dot(ones_blk, la_hi) + _dot(ones_blk, la_lo)
        cum = (pre[:, :dk], tot[:, dk:] - pre[:, dk:] + la[:, dk:])
        qf = q_ref[rows, :] * scale
        kf = k_ref[rows, :]
        att = None
        for d in range(2):
            td = tot[:, d * dk:(d + 1) * dk]
            q_t = (qf * jnp.exp(cum[d])).astype(BF16)
            k_t = (kf * jnp.exp(-cum[d])).astype(BF16)
            qt_sc[d, rows, :] = q_t
            ke_sc[d, rows, :] = (kf * jnp.exp(td - cum[d])).astype(BF16)
            dec_sc[d, rows, :] = jnp.exp(td)
            a_d = jnp.where(lower if d == 0 else upper, _dot_nt(q_t, k_t), 0.0)
            att = a_d if att is None else att + a_d
        osum_sc[rows, :] = _dot(att.astype(BF16), v_ref[rows, :].astype(BF16))

    st = [jnp.zeros((dv, dk), F32) if zero_init else s0_ref[d].T for d in range(2)]
    for c in range(n):
        for d in range(2):
            cc = c if d == 0 else n - 1 - c
            rows = slice(cc * GLA_CHUNK, (cc + 1) * GLA_CHUNK)
            osum_sc[rows, :] += _dot_nt(qt_sc[d, rows, :], st[d].astype(BF16))
            kv = _dot_tn(v_ref[rows, :].astype(BF16), ke_sc[d, rows, :])
            st[d] = st[d] * dec_sc[d, cc * GLA_CHUNK:cc * GLA_CHUNK + 1, :] + kv
    if want_state:
        for d in range(2):
            sfin_ref[d] = st[d].T

    for rb in range(seq // GLA_RB):
        rows = slice(rb * GLA_RB, (rb + 1) * GLA_RB)
        o = osum_sc[rows, :]
        o = o * lax.rsqrt(jnp.mean(o * o, axis=-1, keepdims=True) + EPS) * ng_ref[...]
        o_ref[rows, :] = (o * _silu(g_ref[rows, :])).astype(BF16)


def _gla(proj, q_col, glr, w2p, gate_b, norm_g, s0, *, nb, seq, heads, want_state):
    qk = w2p.shape[-1] // 2
    dk = qk // heads
    dv = norm_g.shape[1]
    vdim = dv * heads
    zero_init = s0 is None
    hp = GLA_HPS
    bk, bv = hp * dk, hp * dv
    assert heads % hp == 0 and q_col % bk == 0 and qk % bk == 0 and (q_col + 2 * qk) % bv == 0 and vdim % bv == 0
    in_specs = [pl.BlockSpec((seq, bk), lambda b, h: (b, q_col // bk + h)),
                pl.BlockSpec((seq, bk), lambda b, h: (b, (q_col + qk) // bk + h)),
                pl.BlockSpec((seq, bv), lambda b, h: (b, (q_col + 2 * qk) // bv + h)),
                pl.BlockSpec((seq, bv), lambda b, h: (b, (q_col + 2 * qk + vdim) // bv + h)),
                pl.BlockSpec((seq, glr.shape[1]), lambda b, h: (b, 0)),
                pl.BlockSpec((w2p.shape[0], 2 * bk), lambda b, h: (0, h)),
                pl.BlockSpec((1, 2 * bk), lambda b, h: (0, h)),
                pl.BlockSpec((1, dv), lambda b, h: (0, 0))]
    args = [proj, proj, proj, proj, glr, w2p, gate_b, norm_g]
    if not zero_init:
        in_specs.append(pl.BlockSpec((None, 2, hp, dk, dv), lambda b, h: (b, 0, h, 0, 0)))
        args.append(s0)
    out_shape = [jax.ShapeDtypeStruct((nb * seq, vdim), BF16)]
    out_specs = [pl.BlockSpec((seq, bv), lambda b, h: (b, h))]
    if want_state:
        out_shape.append(jax.ShapeDtypeStruct((nb, 2, heads, dk, dv), F32))
        out_specs.append(pl.BlockSpec((None, 2, hp, dk, dv), lambda b, h: (b, 0, h, 0, 0)))
    res = pl.pallas_call(
        functools.partial(_gla_kernel, seq=seq, zero_init=zero_init, want_state=want_state),
        grid=(nb, heads // hp), in_specs=in_specs, out_specs=out_specs, out_shape=out_shape,
        scratch_shapes=[pltpu.VMEM((hp, 2, seq, dk), BF16), pltpu.VMEM((hp, 2, seq, dk), BF16),
                        pltpu.VMEM((hp, 2, seq, dk), F32), pltpu.VMEM((hp, seq, dv), F32)],
        compiler_params=_cparams("arbitrary", "arbitrary"), name="gla",
    )(*args)
    return (res[0], res[1]) if want_state else (res[0], None)


def _mixer_residual(x_ref, mod_ref, ng_ref, y, out_ref, h_ref):
    xn = x_ref[...] + mod_ref[5:6, :] * y
    out_ref[...] = xn
    h_ref[...] = _rms_mod(xn, ng_ref[...], mod_ref[7:8, :], mod_ref[6:7, :]).astype(BF16)


def _even_out_kernel(y_ref, u_ref, o_ref, x_ref, mod_ref, sd_ref, gw_ref, gb_ref, wo_ref, ng_ref, out_ref, h_ref):
    sw = u_ref.shape[1]
    ys = _gelu(y_ref[...] + sd_ref[...] * u_ref[...])
    ys = ys * _sigmoid(_dot(ys.astype(BF16), gw_ref[...]) + gb_ref[...])
    y = _dot(ys.astype(BF16), wo_ref[:sw, :]) + _dot(o_ref[...], wo_ref[sw:, :])
    _mixer_residual(x_ref, mod_ref, ng_ref, y, out_ref, h_ref)


def _even_out(y_s5, proj, o_gla, x, mod_l, s5_d, glu_w, glu_b, w_out, norm_next, *, t, row_base, pseq_per_cond):
    m, d = x.shape
    sw = y_s5.shape[1]
    return pl.pallas_call(
        _even_out_kernel, grid=(NPS,),
        in_specs=[pl.BlockSpec((t, sw), lambda i: (i, 0)),
                  pl.BlockSpec((t, sw), lambda i: (i, 0)),
                  pl.BlockSpec((t, o_gla.shape[1]), lambda i: (i, 0)),
                  pl.BlockSpec((t, d), lambda i: (i, 0)),
                  pl.BlockSpec((None, N_MOD, d), lambda i: (row_base + i // pseq_per_cond, 0, 0)),
                  pl.BlockSpec((1, sw), lambda i: (0, 0)),
                  pl.BlockSpec((sw, sw), lambda i: (0, 0)),
                  pl.BlockSpec((1, sw), lambda i: (0, 0)),
                  pl.BlockSpec(w_out.shape, lambda i: (0, 0)),
                  pl.BlockSpec((1, d), lambda i: (0, 0))],
        out_specs=[pl.BlockSpec((t, d), lambda i: (i, 0)), pl.BlockSpec((t, d), lambda i: (i, 0))],
        out_shape=[jax.ShapeDtypeStruct((m, d), F32), jax.ShapeDtypeStruct((m, d), BF16)],
        compiler_params=_cparams("arbitrary"), name="even_out",
    )(y_s5, proj, o_gla, x, mod_l, s5_d, glu_w, glu_b, w_out, norm_next)


LRU_TB = 16
CONV_W = 4
CONV_LEFT = 2
LRU_UNROLL = 8


def _lru_kernel(x_ref, cw_ref, cb_ref, wa_ref, ba_ref, wx_ref, bx_ref, lam_ref, h0_ref, hs_ref, *rest,
                t, glen, nseg, segmented):
    if segmented:
        xp_sc, a_sc, b_sc, hs_sc, e_sc, p_sc, s_sc = rest
    else:
        stout_ref, xp_sc, a_sc, b_sc, hs_sc = rest
    r, w = x_ref.shape
    rb = LRU_TB * NPS
    pad = CONV_LEFT * NPS
    perm = _perm_matrix(NPS, LRU_TB)
    perm_back = _perm_matrix(LRU_TB, NPS)

    for g in range(t // glen + 1):
        xp_sc[g * (glen * NPS + pad):g * (glen * NPS + pad) + pad, :] = jnp.zeros((pad, w), F32)

    def xp_row(i):
        return pl.multiple_of(pad * (1 + (i * LRU_TB) // glen) + i * rb, NPS)

    nbt = t // LRU_TB

    def load_tm(i):
        xin = jnp.concatenate([x_ref[pl.ds(pl.multiple_of(b * t + i * LRU_TB, LRU_TB), LRU_TB), :]
                               for b in range(NPS)], axis=0)
        xp_sc[pl.ds(xp_row(i), rb), :] = _permute_rows_f32(perm, xin)

    c2 = [(-0.25 * LRU_C) * _softplus(-lam_ref[d]) for d in range(2)]

    def conv_gates(i):
        xc = jnp.zeros((rb, w), F32) + cb_ref[...]
        for kk in range(CONV_W):
            xs = xp_sc[pl.ds(pl.multiple_of(xp_row(i) + (kk - CONV_LEFT) * NPS, NPS), rb), :]
            xc = xc + xs * cw_ref[kk:kk + 1, :]
        rows = pl.ds(pl.multiple_of(i * rb, rb), rb)
        xb = xc.astype(BF16)
        for d in range(2):
            tr = jnp.tanh(_dot(xb, wa_ref[d]) + ba_ref[d])
            ti = jnp.tanh(_dot(xb, wx_ref[d]) + bx_ref[d])
            th = jnp.tanh(c2[d] + c2[d] * tr)
            rcp = 1.0 / (1.0 - th)
            a_sc[d, rows, :] = (1.0 + th) * rcp
            b_sc[d, rows, :] = rcp * jnp.sqrt(-th) * ((1.0 + ti) * xc)

    load_tm(0)
    load_tm(min(1, nbt - 1))

    def pre(i, c):
        conv_gates(i)
        load_tm(jnp.minimum(i + 2, nbt - 1))
        return c
    lax.fori_loop(0, nbt, pre, 0)

    def store_bm(i):
        hb = _permute_rows_f32(perm_back, hs_sc[pl.ds(pl.multiple_of(i * rb, rb), rb), :])
        for b in range(NPS):
            hs_ref[pl.ds(pl.multiple_of(b * t + i * LRU_TB, LRU_TB), LRU_TB), :] = hb[b * LRU_TB:(b + 1) * LRU_TB]

    for d in range(2):
        def trow(s):
            return pl.ds(pl.multiple_of((s if d == 0 else t - 1 - s) * NPS, NPS), NPS)

        if segmented:
            def sweep(s, carry):
                h, p = carry
                rows = trow(s)
                a = a_sc[d, rows, :]
                return a * h + b_sc[d, rows, :], a * p
            e, p = lax.fori_loop(0, t, sweep, (jnp.zeros((NPS, w), F32), jnp.ones((NPS, w), F32)),
                                 unroll=LRU_UNROLL)
            e_sc[...] = e
            p_sc[...] = p
            nb = NPS // nseg
            for b in range(nb):
                prev = h0_ref[d, b:b + 1, :]
                for k in (range(nseg) if d == 0 else range(nseg - 1, -1, -1)):
                    j = b * nseg + k
                    s_sc[j:j + 1, :] = prev
                    prev = p_sc[j:j + 1, :] * prev + e_sc[j:j + 1, :]
            h_init = s_sc[...]
        else:
            h_init = h0_ref[d]

        def scan(s, h):
            rows = trow(s)
            h = a_sc[d, rows, :] * h + b_sc[d, rows, :]
            if d == 0:
                hs_sc[rows, :] = h
            else:
                hs_sc[rows, :] += h
            return h
        if d == 0:
            h_fin = lax.fori_loop(0, t, scan, h_init, unroll=LRU_UNROLL)
        else:
            def sweep_block(k, h):
                store_bm(jnp.minimum(nbt - k, nbt - 1))
                for s in range(LRU_TB):
                    h = scan(k * LRU_TB + s, h)
                return h
            h_fin = lax.fori_loop(0, nbt, sweep_block, h_init)
            store_bm(0)
        if not segmented:
            stout_ref[d] = h_fin


def _lru(proj, x_col, width, conv_w, conv_b, wa, ba, wx, bx, lam, h0, *, t, glen, nseg):
    segmented = nseg > 1
    r = proj.shape[0]
    heads, blk = wa.shape[1], wa.shape[2]
    col = lambda hd: (0, hd)
    col3 = lambda hd: (0, 0, hd)
    out_shape = [jax.ShapeDtypeStruct((r, width), F32)]
    out_specs = [pl.BlockSpec((r, blk), col)]
    scratch = [pltpu.VMEM((r + (t // glen + 1) * CONV_LEFT * NPS, blk), F32),
               pltpu.VMEM((2, r, blk), F32), pltpu.VMEM((2, r, blk), F32), pltpu.VMEM((r, blk), F32)]
    if segmented:
        scratch += [pltpu.VMEM((NPS, blk), F32)] * 3
    else:
        out_shape.append(jax.ShapeDtypeStruct((2, NPS, width), F32))
        out_specs.append(pl.BlockSpec((2, NPS, blk), col3))
    res = pl.pallas_call(
        functools.partial(_lru_kernel, t=t, glen=glen, nseg=nseg, segmented=segmented),
        grid=(heads,),
        in_specs=[pl.BlockSpec((r, blk), lambda hd: (0, x_col // blk + hd)),
                  pl.BlockSpec((CONV_W, blk), col),
                  pl.BlockSpec((1, blk), col),
                  pl.BlockSpec((2, None, blk, blk), lambda hd: (0, hd, 0, 0)),
                  pl.BlockSpec((2, 1, blk), col3),
                  pl.BlockSpec((2, None, blk, blk), lambda hd: (0, hd, 0, 0)),
                  pl.BlockSpec((2, 1, blk), col3),
                  pl.BlockSpec((2, 1, blk), col3),
                  pl.BlockSpec((2, h0.shape[1], blk), col3)],
        out_specs=out_specs, out_shape=out_shape, scratch_shapes=scratch,
        compiler_params=_cparams("arbitrary"), name="lru",
    )(proj, conv_w, conv_b, wa, ba, wx, bx, lam, h0)
    return (res[0], res[1]) if not segmented else (res[0], None)


def _odd_out_kernel(hs_ref, gate_ref, x_ref, mod_ref, wo_ref, ng_ref, out_ref, h_ref):
    y = _dot((hs_ref[...] * _gelu(gate_ref[...])).astype(BF16), wo_ref[...])
    _mixer_residual(x_ref, mod_ref, ng_ref, y, out_ref, h_ref)


def _odd_out(hs, proj, x, mod_l, w_out, norm_next, *, t, row_base, pseq_per_cond):
    m, d = x.shape
    w = hs.shape[1]
    return pl.pallas_call(
        _odd_out_kernel, grid=(NPS,),
        in_specs=[pl.BlockSpec((t, w), lambda i: (i, 0)),
                  pl.BlockSpec((t, w), lambda i: (i, 0)),
                  pl.BlockSpec((t, d), lambda i: (i, 0)),
                  pl.BlockSpec((None, N_MOD, d), lambda i: (row_base + i // pseq_per_cond, 0, 0)),
                  pl.BlockSpec(w_out.shape, lambda i: (0, 0)),
                  pl.BlockSpec((1, d), lambda i: (0, 0))],
        out_specs=[pl.BlockSpec((t, d), lambda i: (i, 0)), pl.BlockSpec((t, d), lambda i: (i, 0))],
        out_shape=[jax.ShapeDtypeStruct((m, d), F32), jax.ShapeDtypeStruct((m, d), BF16)],
        compiler_params=_cparams("arbitrary"), name="odd_out",
    )(hs, proj, x, mod_l, w_out, norm_next)


def kernel(x_prompt, x_sample, state_s5_re, state_s5_im, state_gla, state_lru, c, c_ctx, norm_g, ada_w, ada_b, ffn_w_in, ffn_w_out, final_norm_g, ev_w_in, ev_w_out, s5_lam_re, s5_lam_im, s5_log_step, s5_b_re, s5_b_im, s5_c_re, s5_c_im, s5_d, s5_glu_w, s5_glu_b, gla_gate_w2, gla_gate_b, gla_norm_g, od_w_in, od_w_out, lru_conv_w, lru_conv_b, lru_wa, lru_ba, lru_wx, lru_bx, lru_lam):
    nbc, seq, d = x_prompt.shape
    nbl, dseq, _ = x_sample.shape
    depth = norm_g.shape[0]
    assert nbc == NPS and NPS % nbl == 0
    nseg = NPS // nbl
    tl = dseq // nseg
    grid_w = 64
    assert tl % grid_w == 0 and tl % LRU_TB == 0 and seq % LRU_TB == 0 and seq % GLA_RB == 0 and dseq % GLA_RB == 0
    sw = s5_d.shape[1]
    qk = gla_gate_w2.shape[-1]
    heads = state_gla.shape[3]
    rank = gla_gate_w2.shape[2]
    g5, p5 = s5_lam_re.shape[2], s5_lam_re.shape[3]
    main = ev_w_in.shape[2] - 2 * rank
    assert main % PROJ_TN == 0 and (g5 // S5_GP) % S5_PPG == 0

    passes = [dict(x=x_prompt.reshape(nbc * seq, d), base=0, rpc=nbc * seq, ppc=NPS, t=seq, nseg=1,
                   glen=seq, nb=nbc, seq=seq),
              dict(x=x_sample.reshape(nbl * dseq, d), base=1, rpc=dseq, ppc=nseg, t=tl, nseg=nseg,
                   glen=grid_w, nb=nbl, seq=dseq)]

    cond8 = jnp.concatenate([c_ctx[None, :], c, jnp.zeros((8 - 1 - nbl, d), F32)], axis=0)
    mod = _ada(cond8, ada_w, ada_b).reshape(depth, 8, N_MOD, d)
    fg = final_norm_g.reshape(1, d)

    new_s5_re, new_s5_im, new_gla, new_lru = [], [], [], []
    for l in range(depth):
        mod_l = mod[l]
        ng = lambda s: norm_g[l, s].reshape(1, d)
        for ps in passes:
            ps['x'] = _ffn(ps['x'], mod_l, ng(0), fg, ffn_w_in, ffn_w_out, l, 0, mi=0,
                           row_base=ps['base'], rows_per_cond=ps['rpc'], final=False)
        if l % 2 == 0:
            e = l // 2
            w_glr = jnp.pad(ev_w_in[e][:, main:], ((0, 0), (0, 128 - 2 * rank))).astype(BF16)
            w_o = ev_w_out[e].astype(BF16)
            glu_w = s5_glu_w[e].astype(BF16)
            dkh = qk // heads
            w2p = jnp.stack([jnp.pad(gla_gate_w2[e, dd], ((dd * rank, 128 - (dd + 1) * rank), (0, 0)))
                             for dd in range(2)])
            w2p = w2p.reshape(2, 128, heads, dkh).transpose(1, 2, 0, 3).reshape(128, 2 * qk).astype(BF16)
            gate_b = gla_gate_b[e].reshape(2, heads, dkh).transpose(1, 0, 2).reshape(1, 2 * qk)
            w1, w2t, kx, a_bc, a_ts = _s5_params(s5_lam_re[e], s5_lam_im[e], s5_log_step[e], s5_b_re[e], s5_b_im[e],
                                                 s5_c_re[e], s5_c_im[e], [ps['t'] for ps in passes])
            for pi, ps in enumerate(passes):
                t = ps['t']
                proj, glr = _norm_proj(ps['x'], mod_l, ng(1), ev_w_in, e, w_glr, mi=3,
                                       row_base=ps['base'], rows_per_cond=ps['rpc'])
                if pi == 0:
                    h0 = jnp.zeros((2, NPS, a_bc.shape[-1]), F32)
                    s0 = None
                else:
                    h0 = _s5_state_to_cols(state_s5_re[:, e], state_s5_im[:, e])
                    s0 = state_gla[:, e]
                y_s5, s5_fin = _s5(proj, sw, w1, w2t, kx, a_bc, a_ts[pi], h0, t=t, nseg=ps['nseg'])
                o_gla, gla_fin = _gla(proj, sw, glr, w2p, gate_b, gla_norm_g[e].reshape(1, -1), s0,
                                      nb=ps['nb'], seq=ps['seq'], heads=heads, want_state=(pi == 0))
                if pi == 0:
                    sr, si = _s5_cols_to_state(s5_fin, g5, p5)
                    new_s5_re.append(sr)
                    new_s5_im.append(si)
                    new_gla.append(gla_fin)
                ps['x'], ps['h'] = _even_out(y_s5, proj, o_gla, ps['x'], mod_l, s5_d[e].reshape(1, sw), glu_w,
                                             s5_glu_b[e].reshape(1, sw), w_o, ng(2), t=t, row_base=ps['base'],
                                             pseq_per_cond=ps['ppc'])
        else:
            o = l // 2
            lw = od_w_in.shape[2] // 2
            w_o = od_w_out[o].astype(BF16)
            wa16 = (0.5 * lru_wa[o]).astype(BF16)
            wx16 = (0.5 * lru_wx[o]).astype(BF16)
            for pi, ps in enumerate(passes):
                t = ps['t']
                proj, _ = _norm_proj(ps['x'], mod_l, ng(1), od_w_in, o, None, mi=3,
                                     row_base=ps['base'], rows_per_cond=ps['rpc'])
                if pi == 0:
                    h0 = jnp.zeros((2, NPS, lw), F32)
                else:
                    h0 = state_lru[:, o].transpose(1, 0, 2)
                hs, lru_fin = _lru(proj, lw, lw, lru_conv_w[o], lru_conv_b[o].reshape(1, lw), wa16,
                                   0.5 * lru_ba[o].reshape(2, 1, lw), wx16, 0.5 * lru_bx[o].reshape(2, 1, lw),
                                   lru_lam[o].reshape(2, 1, lw), h0, t=t, glen=ps['glen'], nseg=ps['nseg'])
                if pi == 0:
                    new_lru.append(lru_fin.transpose(1, 0, 2))
                ps['x'], ps['h'] = _odd_out(hs, proj, ps['x'], mod_l, w_o, ng(2), t=t, row_base=ps['base'],
                                            pseq_per_cond=ps['ppc'])
        for ps in passes:
            ps['x'] = _ffn(ps['x'], mod_l, ng(2), fg, ffn_w_in, ffn_w_out, l, 1, mi=6, row_base=ps['base'],
                           rows_per_cond=ps['rpc'], final=(l == depth - 1), h_pre=ps['h'])

    y_prompt = passes[0]['x'].reshape(nbc, seq, d)
    y_sample = passes[1]['x'].reshape(nbl, dseq, d)
    return (y_prompt, y_sample, jnp.stack(new_s5_re, 1), jnp.stack(new_s5_im, 1),
            jnp.stack(new_gla, 1), jnp.stack(new_lru, 1))
```

```python
import functools
import math

import jax
import jax.numpy as jnp
from jax import lax
from jax.experimental import pallas as pl
from jax.experimental.pallas import tpu as pltpu

F32 = jnp.float32
BF16 = jnp.bfloat16
EPS = 1e-6
NPS = 16
N_MOD = 9
GLA_CHUNK = 64
GLA_TAU = 16.0
LRU_C = 8.0
VMEM_LIMIT = 58 * 1024 * 1024
ROW_CHUNK = 64
NORM_CHUNK = 64
NORM_UNROLL = 2


def _cparams(*sem):
    return pltpu.CompilerParams(dimension_semantics=sem, vmem_limit_bytes=VMEM_LIMIT)


def _dot(a, b):
    return jnp.dot(a, b, preferred_element_type=F32)


def _dot_nt(a, b):
    return lax.dot_general(a, b, (((1,), (1,)), ((), ())), preferred_element_type=F32)


def _dot_tn(a, b):
    return lax.dot_general(a, b, (((0,), (0,)), ((), ())), preferred_element_type=F32)


def _sigmoid(x):
    return 0.5 * (1.0 + jnp.tanh(0.5 * x))


def _silu(x):
    return x * _sigmoid(x)


def _gelu(x):
    return 0.5 * x * (1.0 + jnp.tanh(math.sqrt(2.0 / math.pi) * (x + 0.044715 * (x * x * x))))


def _softplus(x):
    return jnp.maximum(x, 0.0) + jnp.log1p(jnp.exp(-jnp.abs(x)))


def _rms_mod(x, g, scale, shift):
    return x * lax.rsqrt(jnp.mean(x * x, axis=-1, keepdims=True) + EPS) * (g * (1.0 + scale)) + shift


def _perm_matrix(n_outer, n_inner):
    n = n_outer * n_inner
    ro = lax.broadcasted_iota(jnp.int32, (n, n), 0)
    ci = lax.broadcasted_iota(jnp.int32, (n, n), 1)
    return jnp.where(ci == (ro % n_outer) * n_inner + ro // n_outer, 1.0, 0.0).astype(BF16)


def _permute_rows_f32(perm, x):
    hi = x.astype(BF16)
    r1 = x - hi.astype(F32)
    mid = r1.astype(BF16)
    lo = (r1 - mid.astype(F32)).astype(BF16)
    return _dot(perm, hi) + _dot(perm, mid) + _dot(perm, lo)


def _norm_rows(x_ref, mod_ref, g_ref, h_sc, mi):
    def body(r, c):
        rows = pl.ds(pl.multiple_of(r * NORM_CHUNK, NORM_CHUNK), NORM_CHUNK)
        h = _rms_mod(x_ref[rows, :], g_ref[...], mod_ref[mi + 1:mi + 2, :], mod_ref[mi:mi + 1, :])
        h_sc[rows, :] = h.astype(BF16)
        return c
    lax.fori_loop(0, x_ref.shape[0] // NORM_CHUNK, body, 0, unroll=NORM_UNROLL)


def _ada_kernel(c_ref, w_ref, b_ref, o_ref):
    ca = _silu(c_ref[...])
    o_ref[...] = _dot(ca.astype(BF16), w_ref[...].astype(BF16)) + b_ref[...]


def _ada(cond8, ada_w, ada_b):
    depth, d, n = ada_w.shape
    tn = 1024
    return pl.pallas_call(
        _ada_kernel, grid=(depth, n // tn),
        in_specs=[pl.BlockSpec((8, d), lambda l, j: (0, 0)),
                  pl.BlockSpec((None, d, tn), lambda l, j: (l, 0, j)),
                  pl.BlockSpec((None, 1, tn), lambda l, j: (l, 0, j))],
        out_specs=pl.BlockSpec((None, 8, tn), lambda l, j: (l, 0, j)),
        out_shape=jax.ShapeDtypeStruct((depth, 8, n), F32),
        compiler_params=_cparams("arbitrary", "arbitrary"), name="ada",
    )(cond8, ada_w, ada_b.reshape(depth, 1, n))


FFN_TM = 1024
FFN_TF = 256
FFN_NC = 512


def _ffn_kernel(x_ref, mod_ref, g_ref, fg_ref, wa_ref, wb_ref, wo_ref, *rest, mi, nf, final, has_h):
    h_ref, o_ref = rest if has_h else rest[::-1]
    j = pl.program_id(1)
    tm, d = x_ref.shape

    @pl.when(j == 0)
    def _():
        if not has_h:
            _norm_rows(x_ref, mod_ref, g_ref, h_ref, mi)
        o_ref[...] = jnp.zeros_like(o_ref)

    h = h_ref[...]
    a = _dot(h, wa_ref[...].astype(BF16))
    b = _dot(h, wb_ref[...].astype(BF16))
    act = (_silu(a) * b).astype(BF16)
    for n in range(d // FFN_NC):
        cols = slice(n * FFN_NC, (n + 1) * FFN_NC)
        o_ref[:, cols] += _dot(act, wo_ref[:, cols].astype(BF16))

    @pl.when(j == nf - 1)
    def _():
        def body(r, c):
            rows = pl.ds(pl.multiple_of(r * ROW_CHUNK, ROW_CHUNK), ROW_CHUNK)
            y = x_ref[rows, :] + 0.5 * mod_ref[mi + 2:mi + 3, :] * o_ref[rows, :]
            if final:
                y = y * lax.rsqrt(jnp.mean(y * y, axis=-1, keepdims=True) + EPS) * fg_ref[...]
            o_ref[rows, :] = y
            return c
        lax.fori_loop(0, tm // ROW_CHUNK, body, 0)


def _ffn(x, mod_l, norm_g, final_g, w_in, w_out, l, k, *, mi, row_base, rows_per_cond, final, h_pre=None):
    m, d = x.shape
    f = w_out.shape[2]
    tm = min(FFN_TM, rows_per_cond)
    tf = FFN_TF
    nf = f // tf
    row = lambda i, j: (row_base + (i * tm) // rows_per_cond, 0, 0)
    once = dict(pipeline_mode=pl.Buffered(1))
    in_specs = [pl.BlockSpec((tm, d), lambda i, j: (i, 0), **once),
                pl.BlockSpec((None, N_MOD, d), row),
                pl.BlockSpec((1, d), lambda i, j: (0, 0)),
                pl.BlockSpec((1, d), lambda i, j: (0, 0)),
                pl.BlockSpec((None, None, d, tf), lambda i, j: (l, k, 0, j)),
                pl.BlockSpec((None, None, d, tf), lambda i, j: (l, k, 0, j + nf)),
                pl.BlockSpec((None, None, tf, d), lambda i, j: (l, k, j, 0))]
    args = [x, mod_l, norm_g, final_g, w_in, w_in, w_out]
    if h_pre is not None:
        in_specs.append(pl.BlockSpec((tm, d), lambda i, j: (i, 0), **once))
        args.append(h_pre)
    return pl.pallas_call(
        functools.partial(_ffn_kernel, mi=mi, nf=nf, final=final, has_h=h_pre is not None),
        grid=(m // tm, nf), in_specs=in_specs,
        out_specs=pl.BlockSpec((tm, d), lambda i, j: (i, 0)),
        out_shape=jax.ShapeDtypeStruct((m, d), F32),
        scratch_shapes=[] if h_pre is not None else [pltpu.VMEM((tm, d), BF16)],
        compiler_params=_cparams("arbitrary", "arbitrary"), name="ffn",
    )(*args)


PROJ_TM = 1024
PROJ_TN = 1024


def _norm_proj_kernel(x_ref, mod_ref, g_ref, w_ref, *rest, mi, nmain):
    if len(rest) == 4:
        wx_ref, o_ref, ox_ref, h_sc = rest
    else:
        (o_ref, h_sc), wx_ref, ox_ref = rest, None, None
    j = pl.program_id(1)

    @pl.when(j == 0)
    def _():
        _norm_rows(x_ref, mod_ref, g_ref, h_sc, mi)

    @pl.when(j < nmain)
    def _():
        o_ref[...] = _dot(h_sc[...], w_ref[...].astype(BF16))

    if wx_ref is not None:
        @pl.when(j == nmain)
        def _():
            ox_ref[...] = _dot(h_sc[...], wx_ref[...])


def _norm_proj(x, mod_l, norm_g, w, e, w_extra, *, mi, row_base, rows_per_cond):
    m, d = x.shape
    tm = min(PROJ_TM, rows_per_cond)
    tn = PROJ_TN
    nmain = w.shape[2] // tn
    nj = nmain + (0 if w_extra is None else 1)
    jm = lambda j: jnp.minimum(j, nmain - 1)
    in_specs = [pl.BlockSpec((tm, d), lambda i, j: (i, 0)),
                pl.BlockSpec((None, N_MOD, d), lambda i, j: (row_base + (i * tm) // rows_per_cond, 0, 0)),
                pl.BlockSpec((1, d), lambda i, j: (0, 0)),
                pl.BlockSpec((None, d, tn), lambda i, j: (e, 0, jm(j)))]
    args = [x, mod_l, norm_g, w]
    out_shape = [jax.ShapeDtypeStruct((m, nmain * tn), F32)]
    out_specs = [pl.BlockSpec((tm, tn), lambda i, j: (i, jm(j)))]
    if w_extra is not None:
        in_specs.append(pl.BlockSpec(w_extra.shape, lambda i, j: (0, 0)))
        args.append(w_extra)
        out_shape.append(jax.ShapeDtypeStruct((m, w_extra.shape[1]), F32))
        out_specs.append(pl.BlockSpec((tm, w_extra.shape[1]), lambda i, j: (i, 0)))
    res = pl.pallas_call(
        functools.partial(_norm_proj_kernel, mi=mi, nmain=nmain), grid=(m // tm, nj),
        in_specs=in_specs, out_specs=out_specs, out_shape=out_shape,
        scratch_shapes=[pltpu.VMEM((tm, d), BF16)],
        compiler_params=_cparams("arbitrary", "arbitrary"), name="norm_proj",
    )(*args)
    return res if w_extra is not None else (res[0], None)


S5_S = 8
S5_GP = 2
S5_PPG = 4
S5_UNROLL = 4


def _s5_kernel(u_ref, pc_ref, w1_ref, w2t_ref, w3_ref, a_ref, at_ref, h0_ref, y_ref, *rest, t, nseg, segmented):
    if segmented:
        g_sc, v_sc, st_sc = rest
    else:
        stout_ref, g_sc, v_sc, st_sc = rest
    nblk = t // S5_S
    npair = w1_ref.shape[1]
    cw = w1_ref.shape[2]
    hw = cw // 2
    lt = u_ref.shape[1]

    def gather(k, c):
        r = pl.ds(pl.multiple_of(k * NPS, NPS), NPS)
        for j in range(S5_S):
            g_sc[r, j * lt:(j + 1) * lt] = u_ref[pl.ds(k * S5_S + j, NPS, stride=t), :]
        return c
    lax.fori_loop(0, nblk, gather, 0)
    u = _dot(g_sc[...].astype(BF16), pc_ref[...]).astype(BF16)

    for d in range(2):
        for q in range(npair):
            v_sc[d, :, q * cw:(q + 1) * cw] = _dot(u[:, q * cw:(q + 1) * cw], w1_ref[d, q])

        def advance(state, v):
            outs = []
            for q in range(npair):
                re = slice(q * cw, q * cw + hw)
                im = slice(q * cw + hw, (q + 1) * cw)
                ar, ai = a_ref[d, :, re], a_ref[d, :, im]
                hr, hi = state[:, re], state[:, im]
                outs.append(ar * hr - ai * hi + v[:, re])
                outs.append(ar * hi + ai * hr + v[:, im])
            return jnp.concatenate(outs, axis=1)

        def rows(kk):
            k = kk if d == 0 else nblk - 1 - kk
            return pl.ds(pl.multiple_of(k * NPS, NPS), NPS)

        if segmented:
            st_sc[...] = lax.fori_loop(0, nblk, lambda kk, s: advance(s, v_sc[d, rows(kk), :]),
                                       jnp.zeros(st_sc.shape, F32), unroll=S5_UNROLL)
            nb = NPS // nseg
            for b in range(nb):
                for q in range(npair):
                    re = slice(q * cw, q * cw + hw)
                    im = slice(q * cw + hw, (q + 1) * cw)
                    pr, pi = h0_ref[d, b:b + 1, re], h0_ref[d, b:b + 1, im]
                    ar, ai = at_ref[d, :, re], at_ref[d, :, im]
                    for k in (range(nseg) if d == 0 else range(nseg - 1, -1, -1)):
                        j = b * nseg + k
                        er, ei = st_sc[j:j + 1, re], st_sc[j:j + 1, im]
                        st_sc[j:j + 1, re] = pr
                        st_sc[j:j + 1, im] = pi
                        pr, pi = ar * pr - ai * pi + er, ar * pi + ai * pr + ei
            init = st_sc[...]
        else:
            init = h0_ref[d]

        def body(kk, s):
            r = rows(kk)
            v = v_sc[d, r, :]
            v_sc[d, r, :] = s
            return advance(s, v)
        fin = lax.fori_loop(0, nblk, body, init, unroll=S5_UNROLL)
        if not segmented:
            stout_ref[d] = fin

    for q in range(npair):
        cols = slice(q * cw, (q + 1) * cw)
        hcat = jnp.concatenate([v_sc[0, :, cols], v_sc[1, :, cols]], axis=1).astype(BF16)
        g_sc[:, cols] = _dot_nt(hcat, w2t_ref[q]) + _dot(u[:, cols], w3_ref[q])

    y = g_sc[...]
    hi = y.astype(BF16)
    lo = (y - hi.astype(F32)).astype(BF16)
    g_sc[...] = _dot_nt(hi, pc_ref[...]) + _dot_nt(lo, pc_ref[...])

    def scatter(k, c):
        r = pl.ds(pl.multiple_of(k * NPS, NPS), NPS)
        for j in range(S5_S):
            y_ref[pl.ds(k * S5_S + j, NPS, stride=t), :] = g_sc[r, j * lt:(j + 1) * lt]
        return c
    lax.fori_loop(0, nblk, scatter, 0)


def _s5_col_perm(lt, gw):
    r = jnp.arange(S5_S * lt)
    j, lane = r // lt, r % lt
    dst = (lane // gw) * (S5_S * gw) + j * gw + lane % gw
    return (dst[:, None] == jnp.arange(S5_S * lt)[None, :]).astype(BF16)


def _s5(proj, width, w1, w2t, w3, a_bc, a_t, h0, *, t, nseg):
    segmented = nseg > 1
    m = proj.shape[0]
    npair, cw = w1.shape[1], w1.shape[2]
    sl = S5_PPG * cw
    lt = sl // S5_S
    r = (t // S5_S) * NPS
    pc = _s5_col_perm(lt, cw // S5_S)
    col2 = lambda g: (0, g)
    col3 = lambda g: (0, 0, g)
    out_shape = [jax.ShapeDtypeStruct((m, width), F32)]
    out_specs = [pl.BlockSpec((m, lt), col2)]
    if not segmented:
        out_shape.append(jax.ShapeDtypeStruct((2, NPS, npair * cw), F32))
        out_specs.append(pl.BlockSpec((2, NPS, sl), col3))
    res = pl.pallas_call(
        functools.partial(_s5_kernel, t=t, nseg=nseg, segmented=segmented),
        grid=(npair // S5_PPG,),
        in_specs=[pl.BlockSpec((m, lt), col2),
                  pl.BlockSpec(pc.shape, lambda g: (0, 0)),
                  pl.BlockSpec((2, S5_PPG, cw, cw), lambda g: (0, g, 0, 0)),
                  pl.BlockSpec((S5_PPG, cw, 2 * cw), lambda g: (g, 0, 0)),
                  pl.BlockSpec((S5_PPG, cw, cw), lambda g: (g, 0, 0)),
                  pl.BlockSpec((2, NPS, sl), col3),
                  pl.BlockSpec((2, 1, sl), col3),
                  pl.BlockSpec((2, h0.shape[1], sl), col3)],
        out_specs=out_specs, out_shape=out_shape,
        scratch_shapes=[pltpu.VMEM((r, sl), F32), pltpu.VMEM((2, r, sl), F32), pltpu.VMEM((NPS, sl), F32)],
        compiler_params=_cparams("arbitrary"), name="s5",
    )(proj, pc, w1, w2t, w3, a_bc, a_t, h0)
    return (res[0], res[1]) if not segmented else (res[0], None)


def _s5_params(lam_re, lam_im, log_step, b_re, b_im, c_re, c_im, t_segs):
    nd, g, p = lam_re.shape
    n = b_re.shape[-1]
    npair = g // S5_GP
    hi = lax.Precision.HIGHEST
    dt = jnp.exp(log_step)[..., None]
    z_re, z_im = lam_re * dt, lam_im * dt
    mag = jnp.exp(z_re)
    ab_re, ab_im = mag * jnp.cos(z_im), mag * jnp.sin(z_im)
    den = lam_re * lam_re + lam_im * lam_im
    n_re = ab_re - 1.0
    f_re = (n_re * lam_re + ab_im * lam_im) / den
    f_im = (ab_im * lam_re - n_re * lam_im) / den
    bb_re = f_re[..., None] * b_re - f_im[..., None] * b_im
    bb_im = f_re[..., None] * b_im + f_im[..., None] * b_re

    cw = 2 * S5_GP * p
    col = jnp.arange(cw)
    is_re = col < cw // 2

    def dup(x):
        return jnp.concatenate([x, x], axis=-1)

    zr = dup(z_re.reshape(nd, npair, S5_GP * p))
    zi = dup(z_im.reshape(nd, npair, S5_GP * p))

    def power(e):
        m = jnp.exp(zr[:, :, None, :] * e[:, None, :, None])
        ang = zi[:, :, None, :] * e[:, None, :, None]
        return m * jnp.cos(ang), m * jnp.sin(ang)

    steps = jnp.arange(S5_S, dtype=F32)
    p1r, p1i = power(jnp.stack([S5_S - 1 - steps, steps]))
    p2r, p2i = power(jnp.stack([steps + 1, S5_S - steps]))
    br, bi = [dup(x.reshape(nd, npair, S5_GP, p, n).transpose(0, 1, 4, 2, 3).reshape(nd, npair, n, S5_GP * p))
              for x in (bb_re, bb_im)]
    cr, ci = [dup(x.reshape(nd, npair, S5_GP, n, p).transpose(0, 1, 3, 2, 4).reshape(nd, npair, n, S5_GP * p))
              for x in (c_re, c_im)]
    powers = jnp.stack([p1r, p1i, p2r, p2i], axis=2)
    bc = jnp.stack([br, bi, cr, ci], axis=2)

    mag = jnp.exp(z_re[..., None] * steps)
    lr, li = mag * jnp.cos(z_im[..., None] * steps), mag * jnp.sin(z_im[..., None] * steps)
    abt_re = lr[:, :, :, None, :] * bb_re[..., None] - li[:, :, :, None, :] * bb_im[..., None]
    abt_im = lr[:, :, :, None, :] * bb_im[..., None] + li[:, :, :, None, :] * bb_re[..., None]
    kt = (jnp.einsum('dgnp,dgpmt->dgtmn', c_re, abt_re, precision=hi)
          - jnp.einsum('dgnp,dgpmt->dgtmn', c_im, abt_im, precision=hi))
    kx = kt.reshape(nd, npair, S5_GP, S5_S, n, n).transpose(0, 3, 1, 2, 4, 5).reshape(nd, S5_S, npair, S5_GP * n, n)

    def state_cols(e):
        er, ei = power(jnp.full((nd, 1), e, F32))
        return jnp.where(is_re, er, ei).reshape(nd, npair * cw)

    a_s = state_cols(float(S5_S))
    a_bc = jnp.broadcast_to(a_s[:, None, :], (nd, NPS, a_s.shape[-1]))
    a_ts = [state_cols(float(ts))[:, None, :] for ts in t_segs]
    w1, w2t, w3 = _s5_maps(powers, bc, kx)
    return w1, w2t, w3, a_bc, a_ts


def _s5_maps_kernel(p_ref, bc_ref, kx_ref, w1_ref, w2t_ref, w3_ref):
    nd, ppg = p_ref.shape[0], p_ref.shape[1]
    n, cw = bc_ref.shape[3], bc_ref.shape[4]
    sw = cw // S5_S
    col = lax.broadcasted_iota(jnp.int32, (n, cw), 1)
    is_re = col < cw // 2
    sgrp = (col // (cw // (2 * S5_GP))) % S5_GP
    ocol = lax.broadcasted_iota(jnp.int32, (sw, cw), 1)
    jcol = ocol // sw
    same_grp = (ocol // n) % S5_GP == lax.broadcasted_iota(jnp.int32, (sw, cw), 0) // n
    tile = jnp.where(lax.broadcasted_iota(jnp.int32, (n, cw), 1) % n == lax.broadcasted_iota(jnp.int32, (n, cw), 0),
                     1.0, 0.0).astype(BF16)
    for q in range(ppg):
        for d in range(nd):
            br, bi, cr, ci = [bc_ref[d, q, k] for k in range(4)]
            for j in range(S5_S):
                p1r, p1i, p2r, p2i = [p_ref[d, q, k, j:j + 1, :] for k in range(4)]
                m1 = jnp.where(is_re, p1r * br - p1i * bi, p1r * bi + p1i * br)
                m2 = jnp.where(is_re, p2r * cr - p2i * ci, -(p2r * ci + p2i * cr))
                for g in range(S5_GP):
                    rows = slice((j * S5_GP + g) * n, (j * S5_GP + g + 1) * n)
                    w1_ref[d, q, rows, :] = jnp.where(sgrp == g, m1, 0.0).astype(BF16)
                    w2t_ref[q, rows, d * cw:(d + 1) * cw] = jnp.where(sgrp == g, m2, 0.0).astype(BF16)
        taps = [[jnp.where(same_grp, _dot(kx_ref[d, lag, q].astype(BF16), tile), 0.0) for lag in range(S5_S)]
                for d in range(nd)]
        for i in range(S5_S):
            acc = jnp.zeros((sw, cw), F32)
            for lag in range(S5_S - i):
                acc = acc + jnp.where(jcol == i + lag, taps[0][lag], 0.0)
            for lag in range(i + 1):
                acc = acc + jnp.where(jcol == i - lag, taps[1][lag], 0.0)
            w3_ref[q, i * sw:(i + 1) * sw, :] = acc.astype(BF16)


def _s5_maps(powers, bc, kx):
    nd, npair, _, _, cw = powers.shape
    n = bc.shape[3]
    rows = S5_S * S5_GP * n
    blk5 = lambda a: pl.BlockSpec((nd, S5_PPG) + a.shape[2:], lambda g: (0, g, 0, 0, 0))
    return pl.pallas_call(
        _s5_maps_kernel, grid=(npair // S5_PPG,),
        in_specs=[blk5(powers), blk5(bc),
                  pl.BlockSpec((nd, S5_S, S5_PPG) + kx.shape[3:], lambda g: (0, 0, g, 0, 0))],
        out_specs=[pl.BlockSpec((nd, S5_PPG, rows, cw), lambda g: (0, g, 0, 0)),
                   pl.BlockSpec((S5_PPG, rows, nd * cw), lambda g: (g, 0, 0)),
                   pl.BlockSpec((S5_PPG, rows, cw), lambda g: (g, 0, 0))],
        out_shape=[jax.ShapeDtypeStruct((nd, npair, rows, cw), BF16),
                   jax.ShapeDtypeStruct((npair, rows, nd * cw), BF16),
                   jax.ShapeDtypeStruct((npair, rows, cw), BF16)],
        compiler_params=_cparams("arbitrary"), name="s5_maps",
    )(powers, bc, kx)


def _s5_state_to_cols(s_re, s_im):
    b, nd, g, p = s_re.shape
    npair = g // S5_GP
    st = jnp.stack([s_re.reshape(b, nd, npair, S5_GP * p), s_im.reshape(b, nd, npair, S5_GP * p)], axis=3)
    return st.reshape(b, nd, -1).transpose(1, 0, 2)


def _s5_cols_to_state(st, g, p):
    nd, b, _ = st.shape
    npair = g // S5_GP
    st = st.reshape(nd, b, npair, 2, S5_GP, p).transpose(3, 1, 0, 2, 4, 5).reshape(2, b, nd, g, p)
    return st[0], st[1]


GLA_RB = 256


GLA_HPS = 4


def _gla_kernel(q_ref, k_ref, v_ref, g_ref, glr_ref, w2_ref, gb_ref, ng_ref, *rest, seq, zero_init, want_state):
    rest = list(rest)
    s0_ref = None if zero_init else rest.pop(0)
    o_ref = rest.pop(0)
    sfin_ref = rest.pop(0) if want_state else None
    dk = q_ref.shape[1] // GLA_HPS
    dv = v_ref.shape[1] // GLA_HPS
    for hh in range(GLA_HPS):
        lk, lv, l2 = pl.ds(hh * dk, dk), pl.ds(hh * dv, dv), pl.ds(hh * 2 * dk, 2 * dk)
        args = [q_ref.at[:, lk], k_ref.at[:, lk], v_ref.at[:, lv], g_ref.at[:, lv], glr_ref,
                w2_ref.at[:, l2], gb_ref.at[:, l2], ng_ref]
        if not zero_init:
            args.append(s0_ref.at[:, hh])
        args.append(o_ref.at[:, lv])
        if want_state:
            args.append(sfin_ref.at[:, hh])
        _gla_head(*args, *[sc.at[hh] for sc in rest], seq=seq, zero_init=zero_init, want_state=want_state)


def _gla_head(q_ref, k_ref, v_ref, g_ref, glr_ref, w2_ref, gb_ref, ng_ref, *rest, seq, zero_init, want_state):
    rest = list(rest)
    s0_ref = None if zero_init else rest.pop(0)
    o_ref = rest.pop(0)
    sfin_ref = rest.pop(0) if want_state else None
    qt_sc, ke_sc, dec_sc, osum_sc = rest
    dk = q_ref.shape[1]
    dv = v_ref.shape[1]
    n = seq // GLA_CHUNK
    scale = dk ** -0.5
    ri = lax.broadcasted_iota(jnp.int32, (GLA_RB, GLA_RB), 0)
    ci = lax.broadcasted_iota(jnp.int32, (GLA_RB, GLA_RB), 1)
    same = (ri // GLA_CHUNK) == (ci // GLA_CHUNK)
    lower = same & (ci <= ri)
    upper = same & (ci >= ri)
    ones_blk = jnp.where(same, 1.0, 0.0).astype(BF16)
    tri = jnp.where(lower, 1.0, 0.0).astype(BF16)

    for rb in range(seq // GLA_RB):
        rows = slice(rb * GLA_RB, (rb + 1) * GLA_RB)
        x = _dot(glr_ref[rows, :].astype(BF16), w2_ref[...]) + gb_ref[...]
        la = (jnp.minimum(x, 0.0) - jnp.log1p(jnp.exp(-jnp.abs(x)))) * (1.0 / GLA_TAU)
        la_hi = la.astype(BF16)
        la_lo = (la - la_hi.astype(F32)).astype(BF16)
        pre = _dot(tri, la_hi) + _dot(tri, la_lo)
        tot = _dot(ones_blk, la_hi) + _dot(ones_blk, la_lo)
        cum = (pre[:, :dk], tot[:, dk:] - pre[:, dk:] + la[:, dk:])
        qf = q_ref[rows, :] * scale
        kf = k_ref[rows, :]
        att = None
        for d in range(2):
            td = tot[:, d * dk:(d + 1) * dk]
            q_t = (qf * jnp.exp(cum[d])).astype(BF16)
            k_t = (kf * jnp.exp(-cum[d])).astype(BF16)
            qt_sc[d, rows, :] = q_t
            ke_sc[d, rows, :] = (kf * jnp.exp(td - cum[d])).astype(BF16)
            dec_sc[d, rows, :] = jnp.exp(td)
            a_d = jnp.where(lower if d == 0 else upper, _dot_nt(q_t, k_t), 0.0)
            att = a_d if att is None else att + a_d
        osum_sc[rows, :] = _dot(att.astype(BF16), v_ref[rows, :].astype(BF16))

    st = [jnp.zeros((dv, dk), F32) if zero_init else s0_ref[d].T for d in range(2)]
    for c in range(n):
        for d in range(2):
            cc = c if d == 0 else n - 1 - c
            rows = slice(cc * GLA_CHUNK, (cc + 1) * GLA_CHUNK)
            osum_sc[rows, :] += _dot_nt(qt_sc[d, rows, :], st[d].astype(BF16))
            kv = _dot_tn(v_ref[rows, :].astype(BF16), ke_sc[d, rows, :])
            st[d] = st[d] * dec_sc[d, cc * GLA_CHUNK:cc * GLA_CHUNK + 1, :] + kv
    if want_state:
        for d in range(2):
            sfin_ref[d] = st[d].T

    for rb in range(seq // GLA_RB):
        rows = slice(rb * GLA_RB, (rb + 1) * GLA_RB)
        o = osum_sc[rows, :]
        o = o * lax.rsqrt(jnp.mean(o * o, axis=-1, keepdims=True) + EPS) * ng_ref[...]
        o_ref[rows, :] = (o * _silu(g_ref[rows, :])).astype(BF16)


def _gla(proj, q_col, glr, w2p, gate_b, norm_g, s0, *, nb, seq, heads, want_state):
    qk = w2p.shape[-1] // 2
    dk = qk // heads
    dv = norm_g.shape[1]
    vdim = dv * heads
    zero_init = s0 is None
    hp = GLA_HPS
    bk, bv = hp * dk, hp * dv
    assert heads % hp == 0 and q_col % bk == 0 and qk % bk == 0 and (q_col + 2 * qk) % bv == 0 and vdim % bv == 0
    in_specs = [pl.BlockSpec((seq, bk), lambda b, h: (b, q_col // bk + h)),
                pl.BlockSpec((seq, bk), lambda b, h: (b, (q_col + qk) // bk + h)),
                pl.BlockSpec((seq, bv), lambda b, h: (b, (q_col + 2 * qk) // bv + h)),
                pl.BlockSpec((seq, bv), lambda b, h: (b, (q_col + 2 * qk + vdim) // bv + h)),
                pl.BlockSpec((seq, glr.shape[1]), lambda b, h: (b, 0)),
                pl.BlockSpec((w2p.shape[0], 2 * bk), lambda b, h: (0, h)),
                pl.BlockSpec((1, 2 * bk), lambda b, h: (0, h)),
                pl.BlockSpec((1, dv), lambda b, h: (0, 0))]
    args = [proj, proj, proj, proj, glr, w2p, gate_b, norm_g]
    if not zero_init:
        in_specs.append(pl.BlockSpec((None, 2, hp, dk, dv), lambda b, h: (b, 0, h, 0, 0)))
        args.append(s0)
    out_shape = [jax.ShapeDtypeStruct((nb * seq, vdim), BF16)]
    out_specs = [pl.BlockSpec((seq, bv), lambda b, h: (b, h))]
    if want_state:
        out_shape.append(jax.ShapeDtypeStruct((nb, 2, heads, dk, dv), F32))
        out_specs.append(pl.BlockSpec((None, 2, hp, dk, dv), lambda b, h: (b, 0, h, 0, 0)))
    res = pl.pallas_call(
        functools.partial(_gla_kernel, seq=seq, zero_init=zero_init, want_state=want_state),
        grid=(nb, heads // hp), in_specs=in_specs, out_specs=out_specs, out_shape=out_shape,
        scratch_shapes=[pltpu.VMEM((hp, 2, seq, dk), BF16), pltpu.VMEM((hp, 2, seq, dk), BF16),
                        pltpu.VMEM((hp, 2, seq, dk), F32), pltpu.VMEM((hp, seq, dv), F32)],
        compiler_params=_cparams("arbitrary", "arbitrary"), name="gla",
    )(*args)
    return (res[0], res[1]) if want_state else (res[0], None)


def _mixer_residual(x_ref, mod_ref, ng_ref, y, out_ref, h_ref):
    xn = x_ref[...] + mod_ref[5:6, :] * y
    out_ref[...] = xn
    h_ref[...] = _rms_mod(xn, ng_ref[...], mod_ref[7:8, :], mod_ref[6:7, :]).astype(BF16)


def _even_out_kernel(y_ref, u_ref, o_ref, x_ref, mod_ref, sd_ref, gw_ref, gb_ref, wo_ref, ng_ref, out_ref, h_ref):
    sw = u_ref.shape[1]
    ys = _gelu(y_ref[...] + sd_ref[...] * u_ref[...])
    ys = ys * _sigmoid(_dot(ys.astype(BF16), gw_ref[...]) + gb_ref[...])
    y = _dot(ys.astype(BF16), wo_ref[:sw, :]) + _dot(o_ref[...], wo_ref[sw:, :])
    _mixer_residual(x_ref, mod_ref, ng_ref, y, out_ref, h_ref)


def _even_out(y_s5, proj, o_gla, x, mod_l, s5_d, glu_w, glu_b, w_out, norm_next, *, t, row_base, pseq_per_cond):
    m, d = x.shape
    sw = y_s5.shape[1]
    return pl.pallas_call(
        _even_out_kernel, grid=(NPS,),
        in_specs=[pl.BlockSpec((t, sw), lambda i: (i, 0)),
                  pl.BlockSpec((t, sw), lambda i: (i, 0)),
                  pl.BlockSpec((t, o_gla.shape[1]), lambda i: (i, 0)),
                  pl.BlockSpec((t, d), lambda i: (i, 0)),
                  pl.BlockSpec((None, N_MOD, d), lambda i: (row_base + i // pseq_per_cond, 0, 0)),
                  pl.BlockSpec((1, sw), lambda i: (0, 0)),
                  pl.BlockSpec((sw, sw), lambda i: (0, 0)),
                  pl.BlockSpec((1, sw), lambda i: (0, 0)),
                  pl.BlockSpec(w_out.shape, lambda i: (0, 0)),
                  pl.BlockSpec((1, d), lambda i: (0, 0))],
        out_specs=[pl.BlockSpec((t, d), lambda i: (i, 0)), pl.BlockSpec((t, d), lambda i: (i, 0))],
        out_shape=[jax.ShapeDtypeStruct((m, d), F32), jax.ShapeDtypeStruct((m, d), BF16)],
        compiler_params=_cparams("arbitrary"), name="even_out",
    )(y_s5, proj, o_gla, x, mod_l, s5_d, glu_w, glu_b, w_out, norm_next)


LRU_TB = 16
CONV_W = 4
CONV_LEFT = 2
LRU_UNROLL = 8


def _lru_kernel(x_ref, cw_ref, cb_ref, wa_ref, ba_ref, wx_ref, bx_ref, lam_ref, h0_ref, hs_ref, *rest,
                t, glen, nseg, segmented):
    if segmented:
        xp_sc, a_sc, b_sc, hs_sc, e_sc, p_sc, s_sc = rest
    else:
        stout_ref, xp_sc, a_sc, b_sc, hs_sc = rest
    r, w = x_ref.shape
    rb = LRU_TB * NPS
    pad = CONV_LEFT * NPS
    perm = _perm_matrix(NPS, LRU_TB)
    perm_back = _perm_matrix(LRU_TB, NPS)

    for g in range(t // glen + 1):
        xp_sc[g * (glen * NPS + pad):g * (glen * NPS + pad) + pad, :] = jnp.zeros((pad, w), F32)

    def xp_row(i):
        return pl.multiple_of(pad * (1 + (i * LRU_TB) // glen) + i * rb, NPS)

    nbt = t // LRU_TB

    def load_tm(i):
        xin = jnp.concatenate([x_ref[pl.ds(pl.multiple_of(b * t + i * LRU_TB, LRU_TB), LRU_TB), :]
                               for b in range(NPS)], axis=0)
        xp_sc[pl.ds(xp_row(i), rb), :] = _permute_rows_f32(perm, xin)

    c2 = [(-0.25 * LRU_C) * _softplus(-lam_ref[d]) for d in range(2)]

    def conv_gates(i):
        xc = jnp.zeros((rb, w), F32) + cb_ref[...]
        for kk in range(CONV_W):
            xs = xp_sc[pl.ds(pl.multiple_of(xp_row(i) + (kk - CONV_LEFT) * NPS, NPS), rb), :]
            xc = xc + xs * cw_ref[kk:kk + 1, :]
        rows = pl.ds(pl.multiple_of(i * rb, rb), rb)
        xb = xc.astype(BF16)
        for d in range(2):
            tr = jnp.tanh(_dot(xb, wa_ref[d]) + ba_ref[d])
            ti = jnp.tanh(_dot(xb, wx_ref[d]) + bx_ref[d])
            th = jnp.tanh(c2[d] + c2[d] * tr)
            rcp = 1.0 / (1.0 - th)
            a_sc[d, rows, :] = (1.0 + th) * rcp
            b_sc[d, rows, :] = rcp * jnp.sqrt(-th) * ((1.0 + ti) * xc)

    load_tm(0)
    load_tm(min(1, nbt - 1))

    def pre(i, c):
        conv_gates(i)
        load_tm(jnp.minimum(i + 2, nbt - 1))
        return c
    lax.fori_loop(0, nbt, pre, 0)

    def store_bm(i):
        hb = _permute_rows_f32(perm_back, hs_sc[pl.ds(pl.multiple_of(i * rb, rb), rb), :])
        for b in range(NPS):
            hs_ref[pl.ds(pl.multiple_of(b * t + i * LRU_TB, LRU_TB), LRU_TB), :] = hb[b * LRU_TB:(b + 1) * LRU_TB]

    for d in range(2):
        def trow(s):
            return pl.ds(pl.multiple_of((s if d == 0 else t - 1 - s) * NPS, NPS), NPS)

        if segmented:
            def sweep(s, carry):
                h, p = carry
                rows = trow(s)
                a = a_sc[d, rows, :]
                return a * h + b_sc[d, rows, :], a * p
            e, p = lax.fori_loop(0, t, sweep, (jnp.zeros((NPS, w), F32), jnp.ones((NPS, w), F32)),
                                 unroll=LRU_UNROLL)
            e_sc[...] = e
            p_sc[...] = p
            nb = NPS // nseg
            for b in range(nb):
                prev = h0_ref[d, b:b + 1, :]
                for k in (range(nseg) if d == 0 else range(nseg - 1, -1, -1)):
                    j = b * nseg + k
                    s_sc[j:j + 1, :] = prev
                    prev = p_sc[j:j + 1, :] * prev + e_sc[j:j + 1, :]
            h_init = s_sc[...]
        else:
            h_init = h0_ref[d]

        def scan(s, h):
            rows = trow(s)
            h = a_sc[d, rows, :] * h + b_sc[d, rows, :]
            if d == 0:
                hs_sc[rows, :] = h
            else:
                hs_sc[rows, :] += h
            return h
        if d == 0:
            h_fin = lax.fori_loop(0, t, scan, h_init, unroll=LRU_UNROLL)
        else:
            def sweep_block(k, h):
                store_bm(jnp.minimum(nbt - k, nbt - 1))
                for s in range(LRU_TB):
                    h = scan(k * LRU_TB + s, h)
                return h
            h_fin = lax.fori_loop(0, nbt, sweep_block, h_init)
            store_bm(0)
        if not segmented:
            stout_ref[d] = h_fin


def _lru(proj, x_col, width, conv_w, conv_b, wa, ba, wx, bx, lam, h0, *, t, glen, nseg):
    segmented = nseg > 1
    r = proj.shape[0]
    heads, blk = wa.shape[1], wa.shape[2]
    col = lambda hd: (0, hd)
    col3 = lambda hd: (0, 0, hd)
    out_shape = [jax.ShapeDtypeStruct((r, width), F32)]
    out_specs = [pl.BlockSpec((r, blk), col)]
    scratch = [pltpu.VMEM((r + (t // glen + 1) * CONV_LEFT * NPS, blk), F32),
               pltpu.VMEM((2, r, blk), F32), pltpu.VMEM((2, r, blk), F32), pltpu.VMEM((r, blk), F32)]
    if segmented:
        scratch += [pltpu.VMEM((NPS, blk), F32)] * 3
    else:
        out_shape.append(jax.ShapeDtypeStruct((2, NPS, width), F32))
        out_specs.append(pl.BlockSpec((2, NPS, blk), col3))
    res = pl.pallas_call(
        functools.partial(_lru_kernel, t=t, glen=glen, nseg=nseg, segmented=segmented),
        grid=(heads,),
        in_specs=[pl.BlockSpec((r, blk), lambda hd: (0, x_col // blk + hd)),
                  pl.BlockSpec((CONV_W, blk), col),
                  pl.BlockSpec((1, blk), col),
                  pl.BlockSpec((2, None, blk, blk), lambda hd: (0, hd, 0, 0)),
                  pl.BlockSpec((2, 1, blk), col3),
                  pl.BlockSpec((2, None, blk, blk), lambda hd: (0, hd, 0, 0)),
                  pl.BlockSpec((2, 1, blk), col3),
                  pl.BlockSpec((2, 1, blk), col3),
                  pl.BlockSpec((2, h0.shape[1], blk), col3)],
        out_specs=out_specs, out_shape=out_shape, scratch_shapes=scratch,
        compiler_params=_cparams("arbitrary"), name="lru",
    )(proj, conv_w, conv_b, wa, ba, wx, bx, lam, h0)
    return (res[0], res[1]) if not segmented else (res[0], None)


def _odd_out_kernel(hs_ref, gate_ref, x_ref, mod_ref, wo_ref, ng_ref, out_ref, h_ref):
    y = _dot((hs_ref[...] * _gelu(gate_ref[...])).astype(BF16), wo_ref[...])
    _mixer_residual(x_ref, mod_ref, ng_ref, y, out_ref, h_ref)


def _odd_out(hs, proj, x, mod_l, w_out, norm_next, *, t, row_base, pseq_per_cond):
    m, d = x.shape
    w = hs.shape[1]
    return pl.pallas_call(
        _odd_out_kernel, grid=(NPS,),
        in_specs=[pl.BlockSpec((t, w), lambda i: (i, 0)),
                  pl.BlockSpec((t, w), lambda i: (i, 0)),
                  pl.BlockSpec((t, d), lambda i: (i, 0)),
                  pl.BlockSpec((None, N_MOD, d), lambda i: (row_base + i // pseq_per_cond, 0, 0)),
                  pl.BlockSpec(w_out.shape, lambda i: (0, 0)),
                  pl.BlockSpec((1, d), lambda i: (0, 0))],
        out_specs=[pl.BlockSpec((t, d), lambda i: (i, 0)), pl.BlockSpec((t, d), lambda i: (i, 0))],
        out_shape=[jax.ShapeDtypeStruct((m, d), F32), jax.ShapeDtypeStruct((m, d), BF16)],
        compiler_params=_cparams("arbitrary"), name="odd_out",
    )(hs, proj, x, mod_l, w_out, norm_next)


def kernel(x_prompt, x_sample, state_s5_re, state_s5_im, state_gla, state_lru, c, c_ctx, norm_g, ada_w, ada_b, ffn_w_in, ffn_w_out, final_norm_g, ev_w_in, ev_w_out, s5_lam_re, s5_lam_im, s5_log_step, s5_b_re, s5_b_im, s5_c_re, s5_c_im, s5_d, s5_glu_w, s5_glu_b, gla_gate_w2, gla_gate_b, gla_norm_g, od_w_in, od_w_out, lru_conv_w, lru_conv_b, lru_wa, lru_ba, lru_wx, lru_bx, lru_lam):
    nbc, seq, d = x_prompt.shape
    nbl, dseq, _ = x_sample.shape
    depth = norm_g.shape[0]
    assert nbc == NPS and NPS % nbl == 0
    nseg = NPS // nbl
    tl = dseq // nseg
    grid_w = 64
    assert tl % grid_w == 0 and tl % LRU_TB == 0 and seq % LRU_TB == 0 and seq % GLA_RB == 0 and dseq % GLA_RB == 0
    sw = s5_d.shape[1]
    qk = gla_gate_w2.shape[-1]
    heads = state_gla.shape[3]
    rank = gla_gate_w2.shape[2]
    g5, p5 = s5_lam_re.shape[2], s5_lam_re.shape[3]
    main = ev_w_in.shape[2] - 2 * rank
    assert main % PROJ_TN == 0 and (g5 // S5_GP) % S5_PPG == 0

    passes = [dict(x=x_prompt.reshape(nbc * seq, d), base=0, rpc=nbc * seq, ppc=NPS, t=seq, nseg=1,
                   glen=seq, nb=nbc, seq=seq),
              dict(x=x_sample.reshape(nbl * dseq, d), base=1, rpc=dseq, ppc=nseg, t=tl, nseg=nseg,
                   glen=grid_w, nb=nbl, seq=dseq)]

    cond8 = jnp.concatenate([c_ctx[None, :], c, jnp.zeros((8 - 1 - nbl, d), F32)], axis=0)
    mod = _ada(cond8, ada_w, ada_b).reshape(depth, 8, N_MOD, d)
    fg = final_norm_g.reshape(1, d)

    new_s5_re, new_s5_im, new_gla, new_lru = [], [], [], []
    for l in range(depth):
        mod_l = mod[l]
        ng = lambda s: norm_g[l, s].reshape(1, d)
        for ps in passes:
            ps['x'] = _ffn(ps['x'], mod_l, ng(0), fg, ffn_w_in, ffn_w_out, l, 0, mi=0,
                           row_base=ps['base'], rows_per_cond=ps['rpc'], final=False)
        if l % 2 == 0:
            e = l // 2
            w_glr = jnp.pad(ev_w_in[e][:, main:], ((0, 0), (0, 128 - 2 * rank))).astype(BF16)
            w_o = ev_w_out[e].astype(BF16)
            glu_w = s5_glu_w[e].astype(BF16)
            dkh = qk // heads
            w2p = jnp.stack([jnp.pad(gla_gate_w2[e, dd], ((dd * rank, 128 - (dd + 1) * rank), (0, 0)))
                             for dd in range(2)])
            w2p = w2p.reshape(2, 128, heads, dkh).transpose(1, 2, 0, 3).reshape(128, 2 * qk).astype(BF16)
            gate_b = gla_gate_b[e].reshape(2, heads, dkh).transpose(1, 0, 2).reshape(1, 2 * qk)
            w1, w2t, w3, a_bc, a_ts = _s5_params(s5_lam_re[e], s5_lam_im[e], s5_log_step[e], s5_b_re[e], s5_b_im[e],
                                                 s5_c_re[e], s5_c_im[e], [ps['t'] for ps in passes])
            for pi, ps in enumerate(passes):
                t = ps['t']
                proj, glr = _norm_proj(ps['x'], mod_l, ng(1), ev_w_in, e, w_glr, mi=3,
                                       row_base=ps['base'], rows_per_cond=ps['rpc'])
                if pi == 0:
                    h0 = jnp.zeros((2, NPS, a_bc.shape[-1]), F32)
                    s0 = None
                else:
                    h0 = _s5_state_to_cols(state_s5_re[:, e], state_s5_im[:, e])
                    s0 = state_gla[:, e]
                y_s5, s5_fin = _s5(proj, sw, w1, w2t, w3, a_bc, a_ts[pi], h0, t=t, nseg=ps['nseg'])
                o_gla, gla_fin = _gla(proj, sw, glr, w2p, gate_b, gla_norm_g[e].reshape(1, -1), s0,
                                      nb=ps['nb'], seq=ps['seq'], heads=heads, want_state=(pi == 0))
                if pi == 0:
                    sr, si = _s5_cols_to_state(s5_fin, g5, p5)
                    new_s5_re.append(sr)
                    new_s5_im.append(si)
                    new_gla.append(gla_fin)
                ps['x'], ps['h'] = _even_out(y_s5, proj, o_gla, ps['x'], mod_l, s5_d[e].reshape(1, sw), glu_w,
                                             s5_glu_b[e].reshape(1, sw), w_o, ng(2), t=t, row_base=ps['base'],
                                             pseq_per_cond=ps['ppc'])
        else:
            o = l // 2
            lw = od_w_in.shape[2] // 2
            w_o = od_w_out[o].astype(BF16)
            wa16 = (0.5 * lru_wa[o]).astype(BF16)
            wx16 = (0.5 * lru_wx[o]).astype(BF16)
            for pi, ps in enumerate(passes):
                t = ps['t']
                proj, _ = _norm_proj(ps['x'], mod_l, ng(1), od_w_in, o, None, mi=3,
                                     row_base=ps['base'], rows_per_cond=ps['rpc'])
                if pi == 0:
                    h0 = jnp.zeros((2, NPS, lw), F32)
                else:
                    h0 = state_lru[:, o].transpose(1, 0, 2)
                hs, lru_fin = _lru(proj, lw, lw, lru_conv_w[o], lru_conv_b[o].reshape(1, lw), wa16,
                                   0.5 * lru_ba[o].reshape(2, 1, lw), wx16, 0.5 * lru_bx[o].reshape(2, 1, lw),
                                   lru_lam[o].reshape(2, 1, lw), h0, t=t, glen=ps['glen'], nseg=ps['nseg'])
                if pi == 0:
                    new_lru.append(lru_fin.transpose(1, 0, 2))
                ps['x'], ps['h'] = _odd_out(hs, proj, ps['x'], mod_l, w_o, ng(2), t=t, row_base=ps['base'],
                                            pseq_per_cond=ps['ppc'])
        for ps in passes:
            ps['x'] = _ffn(ps['x'], mod_l, ng(2), fg, ffn_w_in, ffn_w_out, l, 1, mi=6, row_base=ps['base'],
                           rows_per_cond=ps['rpc'], final=(l == depth - 1), h_pre=ps['h'])

    y_prompt = passes[0]['x'].reshape(nbc, seq, d)
    y_sample = passes[1]['x'].reshape(nbl, dseq, d)
    return (y_prompt, y_sample, jnp.stack(new_s5_re, 1), jnp.stack(new_s5_im, 1),
            jnp.stack(new_gla, 1), jnp.stack(new_lru, 1))
```

```python
import functools
import math

import jax
import jax.numpy as jnp
from jax import lax
from jax.experimental import pallas as pl
from jax.experimental.pallas import tpu as pltpu

F32 = jnp.float32
BF16 = jnp.bfloat16
EPS = 1e-6
NPS = 16
N_MOD = 9
GLA_CHUNK = 64
GLA_TAU = 16.0
LRU_C = 8.0
VMEM_LIMIT = 58 * 1024 * 1024
ROW_CHUNK = 64
NORM_CHUNK = 64
NORM_UNROLL = 2


def _cparams(*sem):
    return pltpu.CompilerParams(dimension_semantics=sem, vmem_limit_bytes=VMEM_LIMIT)


def _dot(a, b):
    return jnp.dot(a, b, preferred_element_type=F32)


def _dot_nt(a, b):
    return lax.dot_general(a, b, (((1,), (1,)), ((), ())), preferred_element_type=F32)


def _dot_tn(a, b):
    return lax.dot_general(a, b, (((0,), (0,)), ((), ())), preferred_element_type=F32)


def _sigmoid(x):
    return 0.5 * (1.0 + jnp.tanh(0.5 * x))


def _silu(x):
    return x * _sigmoid(x)


def _gelu(x):
    return 0.5 * x * (1.0 + jnp.tanh(math.sqrt(2.0 / math.pi) * (x + 0.044715 * (x * x * x))))


def _softplus(x):
    return jnp.maximum(x, 0.0) + jnp.log1p(jnp.exp(-jnp.abs(x)))


def _rms_mod(x, g, scale, shift):
    return x * lax.rsqrt(jnp.mean(x * x, axis=-1, keepdims=True) + EPS) * (g * (1.0 + scale)) + shift


def _perm_matrix(n_outer, n_inner):
    n = n_outer * n_inner
    ro = lax.broadcasted_iota(jnp.int32, (n, n), 0)
    ci = lax.broadcasted_iota(jnp.int32, (n, n), 1)
    return jnp.where(ci == (ro % n_outer) * n_inner + ro // n_outer, 1.0, 0.0).astype(BF16)


def _permute_rows_f32(perm, x):
    hi = x.astype(BF16)
    r1 = x - hi.astype(F32)
    mid = r1.astype(BF16)
    lo = (r1 - mid.astype(F32)).astype(BF16)
    return _dot(perm, hi) + _dot(perm, mid) + _dot(perm, lo)


def _norm_rows(x_ref, mod_ref, g_ref, h_sc, mi):
    def body(r, c):
        rows = pl.ds(pl.multiple_of(r * NORM_CHUNK, NORM_CHUNK), NORM_CHUNK)
        h = _rms_mod(x_ref[rows, :], g_ref[...], mod_ref[mi + 1:mi + 2, :], mod_ref[mi:mi + 1, :])
        h_sc[rows, :] = h.astype(BF16)
        return c
    lax.fori_loop(0, x_ref.shape[0] // NORM_CHUNK, body, 0, unroll=NORM_UNROLL)


def _ada_kernel(c_ref, w_ref, b_ref, o_ref):
    ca = _silu(c_ref[...])
    o_ref[...] = _dot(ca.astype(BF16), w_ref[...].astype(BF16)) + b_ref[...]


def _ada(cond8, ada_w, ada_b):
    depth, d, n = ada_w.shape
    tn = 1024
    return pl.pallas_call(
        _ada_kernel, grid=(depth, n // tn),
        in_specs=[pl.BlockSpec((8, d), lambda l, j: (0, 0)),
                  pl.BlockSpec((None, d, tn), lambda l, j: (l, 0, j)),
                  pl.BlockSpec((None, 1, tn), lambda l, j: (l, 0, j))],
        out_specs=pl.BlockSpec((None, 8, tn), lambda l, j: (l, 0, j)),
        out_shape=jax.ShapeDtypeStruct((depth, 8, n), F32),
        compiler_params=_cparams("arbitrary", "arbitrary"), name="ada",
    )(cond8, ada_w, ada_b.reshape(depth, 1, n))


FFN_TM = 1024
FFN_TF = 256
FFN_NC = 512


def _ffn_kernel(x_ref, mod_ref, g_ref, fg_ref, wa_ref, wb_ref, wo_ref, *rest, mi, nf, final, has_h):
    h_ref, o_ref = rest if has_h else rest[::-1]
    j = pl.program_id(1)
    tm, d = x_ref.shape

    @pl.when(j == 0)
    def _():
        if not has_h:
            _norm_rows(x_ref, mod_ref, g_ref, h_ref, mi)
        o_ref[...] = jnp.zeros_like(o_ref)

    h = h_ref[...]
    a = _dot(h, wa_ref[...].astype(BF16))
    b = _dot(h, wb_ref[...].astype(BF16))
    act = (_silu(a) * b).astype(BF16)
    for n in range(d // FFN_NC):
        cols = slice(n * FFN_NC, (n + 1) * FFN_NC)
        o_ref[:, cols] += _dot(act, wo_ref[:, cols].astype(BF16))

    @pl.when(j == nf - 1)
    def _():
        def body(r, c):
            rows = pl.ds(pl.multiple_of(r * ROW_CHUNK, ROW_CHUNK), ROW_CHUNK)
            y = x_ref[rows, :] + 0.5 * mod_ref[mi + 2:mi + 3, :] * o_ref[rows, :]
            if final:
                y = y * lax.rsqrt(jnp.mean(y * y, axis=-1, keepdims=True) + EPS) * fg_ref[...]
            o_ref[rows, :] = y
            return c
        lax.fori_loop(0, tm // ROW_CHUNK, body, 0)


def _ffn(x, mod_l, norm_g, final_g, w_in, w_out, l, k, *, mi, row_base, rows_per_cond, final, h_pre=None):
    m, d = x.shape
    f = w_out.shape[2]
    tm = min(FFN_TM, rows_per_cond)
    tf = FFN_TF
    nf = f // tf
    row = lambda i, j: (row_base + (i * tm) // rows_per_cond, 0, 0)
    once = dict(pipeline_mode=pl.Buffered(1))
    in_specs = [pl.BlockSpec((tm, d), lambda i, j: (i, 0), **once),
                pl.BlockSpec((None, N_MOD, d), row),
                pl.BlockSpec((1, d), lambda i, j: (0, 0)),
                pl.BlockSpec((1, d), lambda i, j: (0, 0)),
                pl.BlockSpec((None, None, d, tf), lambda i, j: (l, k, 0, j)),
                pl.BlockSpec((None, None, d, tf), lambda i, j: (l, k, 0, j + nf)),
                pl.BlockSpec((None, None, tf, d), lambda i, j: (l, k, j, 0))]
    args = [x, mod_l, norm_g, final_g, w_in, w_in, w_out]
    if h_pre is not None:
        in_specs.append(pl.BlockSpec((tm, d), lambda i, j: (i, 0), **once))
        args.append(h_pre)
    return pl.pallas_call(
        functools.partial(_ffn_kernel, mi=mi, nf=nf, final=final, has_h=h_pre is not None),
        grid=(m // tm, nf), in_specs=in_specs,
        out_specs=pl.BlockSpec((tm, d), lambda i, j: (i, 0)),
        out_shape=jax.ShapeDtypeStruct((m, d), F32),
        scratch_shapes=[] if h_pre is not None else [pltpu.VMEM((tm, d), BF16)],
        compiler_params=_cparams("arbitrary", "arbitrary"), name="ffn",
    )(*args)


PROJ_TM = 1024
PROJ_TN = 1024


def _norm_proj_kernel(x_ref, mod_ref, g_ref, w_ref, *rest, mi, nmain, w_rows_out):
    if len(rest) == 4:
        wx_ref, o_ref, ox_ref, h_sc = rest
    else:
        (o_ref, h_sc), wx_ref, ox_ref = rest, None, None
    j = pl.program_id(1)
    mm = _dot_nt if w_rows_out else _dot

    @pl.when(j == 0)
    def _():
        _norm_rows(x_ref, mod_ref, g_ref, h_sc, mi)

    @pl.when(j < nmain)
    def _():
        o_ref[...] = mm(h_sc[...], w_ref[...].astype(BF16))

    if wx_ref is not None:
        @pl.when(j == nmain)
        def _():
            ox_ref[...] = mm(h_sc[...], wx_ref[...].astype(BF16))


def _norm_proj(x, mod_l, norm_g, w, e, n_extra, *, mi, row_base, rows_per_cond, w_rows_out):
    m, d = x.shape
    tm = min(PROJ_TM, rows_per_cond)
    tn = PROJ_TN
    nout = w.shape[1] if w_rows_out else w.shape[2]
    nmain = nout // tn
    assert nout == nmain * tn + n_extra and (n_extra == 0 or (w_rows_out and (nmain * tn) % n_extra == 0))
    nj = nmain + (1 if n_extra else 0)
    jm = lambda j: jnp.minimum(j, nmain - 1)
    wblk = (lambda rows, idx: pl.BlockSpec((None, rows, d), lambda i, j: (e, idx(j), 0))) if w_rows_out else \
           (lambda cols, idx: pl.BlockSpec((None, d, cols), lambda i, j: (e, 0, idx(j))))
    in_specs = [pl.BlockSpec((tm, d), lambda i, j: (i, 0)),
                pl.BlockSpec((None, N_MOD, d), lambda i, j: (row_base + (i * tm) // rows_per_cond, 0, 0)),
                pl.BlockSpec((1, d), lambda i, j: (0, 0)),
                wblk(tn, jm)]
    args = [x, mod_l, norm_g, w]
    out_shape = [jax.ShapeDtypeStruct((m, nmain * tn), F32)]
    out_specs = [pl.BlockSpec((tm, tn), lambda i, j: (i, jm(j)))]
    if n_extra:
        in_specs.append(wblk(n_extra, lambda j: (nmain * tn) // n_extra))
        args.append(w)
        out_shape.append(jax.ShapeDtypeStruct((m, n_extra), F32))
        out_specs.append(pl.BlockSpec((tm, n_extra), lambda i, j: (i, 0)))
    res = pl.pallas_call(
        functools.partial(_norm_proj_kernel, mi=mi, nmain=nmain, w_rows_out=w_rows_out), grid=(m // tm, nj),
        in_specs=in_specs, out_specs=out_specs, out_shape=out_shape,
        scratch_shapes=[pltpu.VMEM((tm, d), BF16)],
        compiler_params=_cparams("arbitrary", "arbitrary"), name="norm_proj",
    )(*args)
    return res if n_extra else (res[0], None)


S5_S = 8
S5_GP = 2
S5_PPG = 4
S5_UNROLL = 4


def _s5_kernel(u_ref, pc_ref, w1_ref, w2t_ref, w3_ref, a_ref, at_ref, h0_ref, y_ref, *rest, t, nseg, segmented):
    if segmented:
        g_sc, v_sc, st_sc = rest
    else:
        stout_ref, g_sc, v_sc, st_sc = rest
    nblk = t // S5_S
    npair = w1_ref.shape[1]
    cw = w1_ref.shape[2]
    hw = cw // 2
    lt = u_ref.shape[1]

    def gather(k, c):
        r = pl.ds(pl.multiple_of(k * NPS, NPS), NPS)
        for j in range(S5_S):
            g_sc[r, j * lt:(j + 1) * lt] = u_ref[pl.ds(k * S5_S + j, NPS, stride=t), :]
        return c
    lax.fori_loop(0, nblk, gather, 0)
    u = _dot(g_sc[...].astype(BF16), pc_ref[...]).astype(BF16)

    for d in range(2):
        for q in range(npair):
            v_sc[d, :, q * cw:(q + 1) * cw] = _dot(u[:, q * cw:(q + 1) * cw], w1_ref[d, q])

        def advance(state, v):
            outs = []
            for q in range(npair):
                re = slice(q * cw, q * cw + hw)
                im = slice(q * cw + hw, (q + 1) * cw)
                ar, ai = a_ref[d, :, re], a_ref[d, :, im]
                hr, hi = state[:, re], state[:, im]
                outs.append(ar * hr - ai * hi + v[:, re])
                outs.append(ar * hi + ai * hr + v[:, im])
            return jnp.concatenate(outs, axis=1)

        def rows(kk):
            k = kk if d == 0 else nblk - 1 - kk
            return pl.ds(pl.multiple_of(k * NPS, NPS), NPS)

        if segmented:
            st_sc[...] = lax.fori_loop(0, nblk, lambda kk, s: advance(s, v_sc[d, rows(kk), :]),
                                       jnp.zeros(st_sc.shape, F32), unroll=S5_UNROLL)
            nb = NPS // nseg
            for b in range(nb):
                for q in range(npair):
                    re = slice(q * cw, q * cw + hw)
                    im = slice(q * cw + hw, (q + 1) * cw)
                    pr, pi = h0_ref[d, b:b + 1, re], h0_ref[d, b:b + 1, im]
                    ar, ai = at_ref[d, :, re], at_ref[d, :, im]
                    for k in (range(nseg) if d == 0 else range(nseg - 1, -1, -1)):
                        j = b * nseg + k
                        er, ei = st_sc[j:j + 1, re], st_sc[j:j + 1, im]
                        st_sc[j:j + 1, re] = pr
                        st_sc[j:j + 1, im] = pi
                        pr, pi = ar * pr - ai * pi + er, ar * pi + ai * pr + ei
            init = st_sc[...]
        else:
            init = h0_ref[d]

        def body(kk, s):
            r = rows(kk)
            v = v_sc[d, r, :]
            v_sc[d, r, :] = s
            return advance(s, v)
        fin = lax.fori_loop(0, nblk, body, init, unroll=S5_UNROLL)
        if not segmented:
            stout_ref[d] = fin

    for q in range(npair):
        cols = slice(q * cw, (q + 1) * cw)
        hcat = jnp.concatenate([v_sc[0, :, cols], v_sc[1, :, cols]], axis=1).astype(BF16)
        g_sc[:, cols] = _dot_nt(hcat, w2t_ref[q]) + _dot(u[:, cols], w3_ref[q])

    y = g_sc[...]
    hi = y.astype(BF16)
    lo = (y - hi.astype(F32)).astype(BF16)
    g_sc[...] = _dot_nt(hi, pc_ref[...]) + _dot_nt(lo, pc_ref[...])

    def scatter(k, c):
        r = pl.ds(pl.multiple_of(k * NPS, NPS), NPS)
        for j in range(S5_S):
            y_ref[pl.ds(k * S5_S + j, NPS, stride=t), :] = g_sc[r, j * lt:(j + 1) * lt]
        return c
    lax.fori_loop(0, nblk, scatter, 0)


def _s5_col_perm(lt, gw):
    r = jnp.arange(S5_S * lt)
    j, lane = r // lt, r % lt
    dst = (lane // gw) * (S5_S * gw) + j * gw + lane % gw
    return (dst[:, None] == jnp.arange(S5_S * lt)[None, :]).astype(BF16)


def _s5(proj, width, w1, w2t, w3, a_bc, a_t, h0, *, t, nseg):
    segmented = nseg > 1
    m = proj.shape[0]
    npair, cw = w1.shape[1], w1.shape[2]
    sl = S5_PPG * cw
    lt = sl // S5_S
    r = (t // S5_S) * NPS
    pc = _s5_col_perm(lt, cw // S5_S)
    col2 = lambda g: (0, g)
    col3 = lambda g: (0, 0, g)
    out_shape = [jax.ShapeDtypeStruct((m, width), F32)]
    out_specs = [pl.BlockSpec((m, lt), col2)]
    if not segmented:
        out_shape.append(jax.ShapeDtypeStruct((2, NPS, npair * cw), F32))
        out_specs.append(pl.BlockSpec((2, NPS, sl), col3))
    res = pl.pallas_call(
        functools.partial(_s5_kernel, t=t, nseg=nseg, segmented=segmented),
        grid=(npair // S5_PPG,),
        in_specs=[pl.BlockSpec((m, lt), col2),
                  pl.BlockSpec(pc.shape, lambda g: (0, 0)),
                  pl.BlockSpec((2, S5_PPG, cw, cw), lambda g: (0, g, 0, 0)),
                  pl.BlockSpec((S5_PPG, cw, 2 * cw), lambda g: (g, 0, 0)),
                  pl.BlockSpec((S5_PPG, cw, cw), lambda g: (g, 0, 0)),
                  pl.BlockSpec((2, NPS, sl), col3),
                  pl.BlockSpec((2, 1, sl), col3),
                  pl.BlockSpec((2, h0.shape[1], sl), col3)],
        out_specs=out_specs, out_shape=out_shape,
        scratch_shapes=[pltpu.VMEM((r, sl), F32), pltpu.VMEM((2, r, sl), F32), pltpu.VMEM((NPS, sl), F32)],
        compiler_params=_cparams("arbitrary"), name="s5",
    )(proj, pc, w1, w2t, w3, a_bc, a_t, h0)
    return (res[0], res[1]) if not segmented else (res[0], None)


def _s5_params(lam_re, lam_im, log_step, b_re, b_im, c_re, c_im, t_segs):
    nd, g, p = lam_re.shape
    n = b_re.shape[-1]
    npair = g // S5_GP
    hi = lax.Precision.HIGHEST
    dt = jnp.exp(log_step)[..., None]
    z_re, z_im = lam_re * dt, lam_im * dt
    mag = jnp.exp(z_re)
    ab_re, ab_im = mag * jnp.cos(z_im), mag * jnp.sin(z_im)
    den = lam_re * lam_re + lam_im * lam_im
    n_re = ab_re - 1.0
    f_re = (n_re * lam_re + ab_im * lam_im) / den
    f_im = (ab_im * lam_re - n_re * lam_im) / den
    bb_re = f_re[..., None] * b_re - f_im[..., None] * b_im
    bb_im = f_re[..., None] * b_im + f_im[..., None] * b_re

    cw = 2 * S5_GP * p
    col = jnp.arange(cw)
    is_re = col < cw // 2

    def dup(x):
        return jnp.concatenate([x, x], axis=-1)

    zr = dup(z_re.reshape(nd, npair, S5_GP * p))
    zi = dup(z_im.reshape(nd, npair, S5_GP * p))

    def power(e):
        m = jnp.exp(zr[:, :, None, :] * e[:, None, :, None])
        ang = zi[:, :, None, :] * e[:, None, :, None]
        return m * jnp.cos(ang), m * jnp.sin(ang)

    steps = jnp.arange(S5_S, dtype=F32)
    p1r, p1i = power(jnp.stack([S5_S - 1 - steps, steps]))
    p2r, p2i = power(jnp.stack([steps + 1, S5_S - steps]))
    br, bi = [dup(x.reshape(nd, npair, S5_GP, p, n).transpose(0, 1, 4, 2, 3).reshape(nd, npair, n, S5_GP * p))
              for x in (bb_re, bb_im)]
    cr, ci = [dup(x.reshape(nd, npair, S5_GP, n, p).transpose(0, 1, 3, 2, 4).reshape(nd, npair, n, S5_GP * p))
              for x in (c_re, c_im)]
    powers = jnp.stack([p1r, p1i, p2r, p2i], axis=2)
    bc = jnp.stack([br, bi, cr, ci], axis=2)

    mag = jnp.exp(z_re[..., None] * steps)
    lr, li = mag * jnp.cos(z_im[..., None] * steps), mag * jnp.sin(z_im[..., None] * steps)
    abt_re = lr[:, :, :, None, :] * bb_re[..., None] - li[:, :, :, None, :] * bb_im[..., None]
    abt_im = lr[:, :, :, None, :] * bb_im[..., None] + li[:, :, :, None, :] * bb_re[..., None]
    kt = (jnp.einsum('dgnp,dgpmt->dgtmn', c_re, abt_re, precision=hi)
          - jnp.einsum('dgnp,dgpmt->dgtmn', c_im, abt_im, precision=hi))
    kx = kt.reshape(nd, npair, S5_GP, S5_S, n, n).transpose(0, 3, 1, 2, 4, 5).reshape(nd, S5_S, npair, S5_GP * n, n)

    def state_cols(e):
        er, ei = power(jnp.full((nd, 1), e, F32))
        return jnp.where(is_re, er, ei).reshape(nd, npair * cw)

    a_s = state_cols(float(S5_S))
    a_bc = jnp.broadcast_to(a_s[:, None, :], (nd, NPS, a_s.shape[-1]))
    a_ts = [state_cols(float(ts))[:, None, :] for ts in t_segs]
    w1, w2t, w3 = _s5_maps(powers, bc, kx)
    return w1, w2t, w3, a_bc, a_ts


def _s5_maps_kernel(p_ref, bc_ref, kx_ref, w1_ref, w2t_ref, w3_ref):
    nd, ppg = p_ref.shape[0], p_ref.shape[1]
    n, cw = bc_ref.shape[3], bc_ref.shape[4]
    sw = cw // S5_S
    col = lax.broadcasted_iota(jnp.int32, (n, cw), 1)
    is_re = col < cw // 2
    sgrp = (col // (cw // (2 * S5_GP))) % S5_GP
    ocol = lax.broadcasted_iota(jnp.int32, (sw, cw), 1)
    jcol = ocol // sw
    same_grp = (ocol // n) % S5_GP == lax.broadcasted_iota(jnp.int32, (sw, cw), 0) // n
    tile = jnp.where(lax.broadcasted_iota(jnp.int32, (n, cw), 1) % n == lax.broadcasted_iota(jnp.int32, (n, cw), 0),
                     1.0, 0.0).astype(BF16)
    for q in range(ppg):
        for d in range(nd):
            br, bi, cr, ci = [bc_ref[d, q, k] for k in range(4)]
            for j in range(S5_S):
                p1r, p1i, p2r, p2i = [p_ref[d, q, k, j:j + 1, :] for k in range(4)]
                m1 = jnp.where(is_re, p1r * br - p1i * bi, p1r * bi + p1i * br)
                m2 = jnp.where(is_re, p2r * cr - p2i * ci, -(p2r * ci + p2i * cr))
                for g in range(S5_GP):
                    rows = slice((j * S5_GP + g) * n, (j * S5_GP + g + 1) * n)
                    w1_ref[d, q, rows, :] = jnp.where(sgrp == g, m1, 0.0).astype(BF16)
                    w2t_ref[q, rows, d * cw:(d + 1) * cw] = jnp.where(sgrp == g, m2, 0.0).astype(BF16)
        taps = [[jnp.where(same_grp, _dot(kx_ref[d, lag, q].astype(BF16), tile), 0.0) for lag in range(S5_S)]
                for d in range(nd)]
        for i in range(S5_S):
            acc = jnp.zeros((sw, cw), F32)
            for lag in range(S5_S - i):
                acc = acc + jnp.where(jcol == i + lag, taps[0][lag], 0.0)
            for lag in range(i + 1):
                acc = acc + jnp.where(jcol == i - lag, taps[1][lag], 0.0)
            w3_ref[q, i * sw:(i + 1) * sw, :] = acc.astype(BF16)


def _s5_maps(powers, bc, kx):
    nd, npair, _, _, cw = powers.shape
    n = bc.shape[3]
    rows = S5_S * S5_GP * n
    blk5 = lambda a: pl.BlockSpec((nd, S5_PPG) + a.shape[2:], lambda g: (0, g, 0, 0, 0))
    return pl.pallas_call(
        _s5_maps_kernel, grid=(npair // S5_PPG,),
        in_specs=[blk5(powers), blk5(bc),
                  pl.BlockSpec((nd, S5_S, S5_PPG) + kx.shape[3:], lambda g: (0, 0, g, 0, 0))],
        out_specs=[pl.BlockSpec((nd, S5_PPG, rows, cw), lambda g: (0, g, 0, 0)),
                   pl.BlockSpec((S5_PPG, rows, nd * cw), lambda g: (g, 0, 0)),
                   pl.BlockSpec((S5_PPG, rows, cw), lambda g: (g, 0, 0))],
        out_shape=[jax.ShapeDtypeStruct((nd, npair, rows, cw), BF16),
                   jax.ShapeDtypeStruct((npair, rows, nd * cw), BF16),
                   jax.ShapeDtypeStruct((npair, rows, cw), BF16)],
        compiler_params=_cparams("arbitrary"), name="s5_maps",
    )(powers, bc, kx)


def _s5_state_to_cols(s_re, s_im):
    b, nd, g, p = s_re.shape
    npair = g // S5_GP
    st = jnp.stack([s_re.reshape(b, nd, npair, S5_GP * p), s_im.reshape(b, nd, npair, S5_GP * p)], axis=3)
    return st.reshape(b, nd, -1).transpose(1, 0, 2)


def _s5_cols_to_state(st, g, p):
    nd, b, _ = st.shape
    npair = g // S5_GP
    st = st.reshape(nd, b, npair, 2, S5_GP, p).transpose(3, 1, 0, 2, 4, 5).reshape(2, b, nd, g, p)
    return st[0], st[1]


GLA_RB = 256


GLA_HPS = 4


def _gla_kernel(q_ref, k_ref, v_ref, g_ref, glr_ref, w2_ref, gb_ref, ng_ref, *rest, seq, zero_init, want_state):
    rest = list(rest)
    s0_ref = None if zero_init else rest.pop(0)
    o_ref = rest.pop(0)
    sfin_ref = rest.pop(0) if want_state else None
    dk = q_ref.shape[1] // GLA_HPS
    dv = v_ref.shape[1] // GLA_HPS
    for hh in range(GLA_HPS):
        lk, lv, l2 = pl.ds(hh * dk, dk), pl.ds(hh * dv, dv), pl.ds(hh * 2 * dk, 2 * dk)
        args = [q_ref.at[:, lk], k_ref.at[:, lk], v_ref.at[:, lv], g_ref.at[:, lv], glr_ref,
                w2_ref.at[:, l2], gb_ref.at[:, l2], ng_ref]
        if not zero_init:
            args.append(s0_ref.at[:, hh])
        args.append(o_ref.at[:, lv])
        if want_state:
            args.append(sfin_ref.at[:, hh])
        _gla_head(*args, *[sc.at[hh] for sc in rest], seq=seq, zero_init=zero_init, want_state=want_state)


def _gla_head(q_ref, k_ref, v_ref, g_ref, glr_ref, w2_ref, gb_ref, ng_ref, *rest, seq, zero_init, want_state):
    rest = list(rest)
    s0_ref = None if zero_init else rest.pop(0)
    o_ref = rest.pop(0)
    sfin_ref = rest.pop(0) if want_state else None
    qt_sc, ke_sc, dec_sc, osum_sc = rest
    dk = q_ref.shape[1]
    dv = v_ref.shape[1]
    n = seq // GLA_CHUNK
    scale = dk ** -0.5
    ri = lax.broadcasted_iota(jnp.int32, (GLA_RB, GLA_RB), 0)
    ci = lax.broadcasted_iota(jnp.int32, (GLA_RB, GLA_RB), 1)
    same = (ri // GLA_CHUNK) == (ci // GLA_CHUNK)
    lower = same & (ci <= ri)
    upper = same & (ci >= ri)
    ones_blk = jnp.where(same, 1.0, 0.0).astype(BF16)
    tri = jnp.where(lower, 1.0, 0.0).astype(BF16)

    for rb in range(seq // GLA_RB):
        rows = slice(rb * GLA_RB, (rb + 1) * GLA_RB)
        x = _dot(glr_ref[rows, :].astype(BF16), w2_ref[...]) + gb_ref[...]
        la = (jnp.minimum(x, 0.0) - jnp.log1p(jnp.exp(-jnp.abs(x)))) * (1.0 / GLA_TAU)
        la_hi = la.astype(BF16)
        la_lo = (la - la_hi.astype(F32)).astype(BF16)
        pre = _dot(tri, la_hi) + _dot(tri, la_lo)
        tot = _dot(ones_blk, la_hi) + _dot(ones_blk, la_lo)
        cum = (pre[:, :dk], tot[:, dk:] - pre[:, dk:] + la[:, dk:])
        qf = q_ref[rows, :] * scale
        kf = k_ref[rows, :]
        att = None
        for d in range(2):
            td = tot[:, d * dk:(d + 1) * dk]
            q_t = (qf * jnp.exp(cum[d])).astype(BF16)
            k_t = (kf * jnp.exp(-cum[d])).astype(BF16)
            qt_sc[d, rows, :] = q_t
            ke_sc[d, rows, :] = (kf * jnp.exp(td - cum[d])).astype(BF16)
            dec_sc[d, rows, :] = jnp.exp(td)
            a_d = jnp.where(lower if d == 0 else upper, _dot_nt(q_t, k_t), 0.0)
            att = a_d if att is None else att + a_d
        osum_sc[rows, :] = _dot(att.astype(BF16), v_ref[rows, :].astype(BF16))

    st = [jnp.zeros((dv, dk), F32) if zero_init else s0_ref[d].T for d in range(2)]
    for c in range(n):
        for d in range(2):
            cc = c if d == 0 else n - 1 - c
            rows = slice(cc * GLA_CHUNK, (cc + 1) * GLA_CHUNK)
            osum_sc[rows, :] += _dot_nt(qt_sc[d, rows, :], st[d].astype(BF16))
            kv = _dot_tn(v_ref[rows, :].astype(BF16), ke_sc[d, rows, :])
            st[d] = st[d] * dec_sc[d, cc * GLA_CHUNK:cc * GLA_CHUNK + 1, :] + kv
    if want_state:
        for d in range(2):
            sfin_ref[d] = st[d].T

    for rb in range(seq // GLA_RB):
        rows = slice(rb * GLA_RB, (rb + 1) * GLA_RB)
        o = osum_sc[rows, :]
        o = o * lax.rsqrt(jnp.mean(o * o, axis=-1, keepdims=True) + EPS) * ng_ref[...]
        o_ref[rows, :] = (o * _silu(g_ref[rows, :])).astype(BF16)


def _gla(proj, q_col, glr, w2p, gate_b, norm_g, s0, *, nb, seq, heads, want_state):
    qk = w2p.shape[-1] // 2
    dk = qk // heads
    dv = norm_g.shape[1]
    vdim = dv * heads
    zero_init = s0 is None
    hp = GLA_HPS
    bk, bv = hp * dk, hp * dv
    assert heads % hp == 0 and q_col % bk == 0 and qk % bk == 0 and (q_col + 2 * qk) % bv == 0 and vdim % bv == 0
    in_specs = [pl.BlockSpec((seq, bk), lambda b, h: (b, q_col // bk + h)),
                pl.BlockSpec((seq, bk), lambda b, h: (b, (q_col + qk) // bk + h)),
                pl.BlockSpec((seq, bv), lambda b, h: (b, (q_col + 2 * qk) // bv + h)),
                pl.BlockSpec((seq, bv), lambda b, h: (b, (q_col + 2 * qk + vdim) // bv + h)),
                pl.BlockSpec((seq, glr.shape[1]), lambda b, h: (b, 0)),
                pl.BlockSpec((w2p.shape[0], 2 * bk), lambda b, h: (0, h)),
                pl.BlockSpec((1, 2 * bk), lambda b, h: (0, h)),
                pl.BlockSpec((1, dv), lambda b, h: (0, 0))]
    args = [proj, proj, proj, proj, glr, w2p, gate_b, norm_g]
    if not zero_init:
        in_specs.append(pl.BlockSpec((None, 2, hp, dk, dv), lambda b, h: (b, 0, h, 0, 0)))
        args.append(s0)
    out_shape = [jax.ShapeDtypeStruct((nb * seq, vdim), BF16)]
    out_specs = [pl.BlockSpec((seq, bv), lambda b, h: (b, h))]
    if want_state:
        out_shape.append(jax.ShapeDtypeStruct((nb, 2, heads, dk, dv), F32))
        out_specs.append(pl.BlockSpec((None, 2, hp, dk, dv), lambda b, h: (b, 0, h, 0, 0)))
    res = pl.pallas_call(
        functools.partial(_gla_kernel, seq=seq, zero_init=zero_init, want_state=want_state),
        grid=(nb, heads // hp), in_specs=in_specs, out_specs=out_specs, out_shape=out_shape,
        scratch_shapes=[pltpu.VMEM((hp, 2, seq, dk), BF16), pltpu.VMEM((hp, 2, seq, dk), BF16),
                        pltpu.VMEM((hp, 2, seq, dk), F32), pltpu.VMEM((hp, seq, dv), F32)],
        compiler_params=_cparams("arbitrary", "arbitrary"), name="gla",
    )(*args)
    return (res[0], res[1]) if want_state else (res[0], None)


def _mixer_residual(x_ref, mod_ref, ng_ref, y, out_ref, h_ref):
    xn = x_ref[...] + mod_ref[5:6, :] * y
    out_ref[...] = xn
    h_ref[...] = _rms_mod(xn, ng_ref[...], mod_ref[7:8, :], mod_ref[6:7, :]).astype(BF16)


def _even_out_kernel(y_ref, u_ref, o_ref, x_ref, mod_ref, sd_ref, gw_ref, gb_ref, wo_ref, ng_ref, out_ref, h_ref):
    sw = u_ref.shape[1]
    ys = _gelu(y_ref[...] + sd_ref[...] * u_ref[...])
    ys = ys * _sigmoid(_dot(ys.astype(BF16), gw_ref[...]) + gb_ref[...])
    y = _dot(ys.astype(BF16), wo_ref[:sw, :]) + _dot(o_ref[...], wo_ref[sw:, :])
    _mixer_residual(x_ref, mod_ref, ng_ref, y, out_ref, h_ref)


def _even_out(y_s5, proj, o_gla, x, mod_l, s5_d, glu_w, glu_b, w_out, norm_next, *, t, row_base, pseq_per_cond):
    m, d = x.shape
    sw = y_s5.shape[1]
    return pl.pallas_call(
        _even_out_kernel, grid=(NPS,),
        in_specs=[pl.BlockSpec((t, sw), lambda i: (i, 0)),
                  pl.BlockSpec((t, sw), lambda i: (i, 0)),
                  pl.BlockSpec((t, o_gla.shape[1]), lambda i: (i, 0)),
                  pl.BlockSpec((t, d), lambda i: (i, 0)),
                  pl.BlockSpec((None, N_MOD, d), lambda i: (row_base + i // pseq_per_cond, 0, 0)),
                  pl.BlockSpec((1, sw), lambda i: (0, 0)),
                  pl.BlockSpec((sw, sw), lambda i: (0, 0)),
                  pl.BlockSpec((1, sw), lambda i: (0, 0)),
                  pl.BlockSpec(w_out.shape, lambda i: (0, 0)),
                  pl.BlockSpec((1, d), lambda i: (0, 0))],
        out_specs=[pl.BlockSpec((t, d), lambda i: (i, 0)), pl.BlockSpec((t, d), lambda i: (i, 0))],
        out_shape=[jax.ShapeDtypeStruct((m, d), F32), jax.ShapeDtypeStruct((m, d), BF16)],
        compiler_params=_cparams("arbitrary"), name="even_out",
    )(y_s5, proj, o_gla, x, mod_l, s5_d, glu_w, glu_b, w_out, norm_next)


LRU_TB = 16
CONV_W = 4
CONV_LEFT = 2
LRU_UNROLL = 8


def _lru_kernel(x_ref, cw_ref, cb_ref, wa_ref, ba_ref, wx_ref, bx_ref, lam_ref, h0_ref, hs_ref, *rest,
                t, glen, nseg, segmented):
    if segmented:
        xp_sc, a_sc, b_sc, hs_sc, e_sc, p_sc, s_sc = rest
    else:
        stout_ref, xp_sc, a_sc, b_sc, hs_sc = rest
    r, w = x_ref.shape
    rb = LRU_TB * NPS
    pad = CONV_LEFT * NPS
    perm = _perm_matrix(NPS, LRU_TB)
    perm_back = _perm_matrix(LRU_TB, NPS)

    for g in range(t // glen + 1):
        xp_sc[g * (glen * NPS + pad):g * (glen * NPS + pad) + pad, :] = jnp.zeros((pad, w), F32)

    def xp_row(i):
        return pl.multiple_of(pad * (1 + (i * LRU_TB) // glen) + i * rb, NPS)

    nbt = t // LRU_TB

    def load_tm(i):
        xin = jnp.concatenate([x_ref[pl.ds(pl.multiple_of(b * t + i * LRU_TB, LRU_TB), LRU_TB), :]
                               for b in range(NPS)], axis=0)
        xp_sc[pl.ds(xp_row(i), rb), :] = _permute_rows_f32(perm, xin)

    c2 = [(-0.25 * LRU_C) * _softplus(-lam_ref[d]) for d in range(2)]

    def conv_gates(i):
        xc = jnp.zeros((rb, w), F32) + cb_ref[...]
        for kk in range(CONV_W):
            xs = xp_sc[pl.ds(pl.multiple_of(xp_row(i) + (kk - CONV_LEFT) * NPS, NPS), rb), :]
            xc = xc + xs * cw_ref[kk:kk + 1, :]
        rows = pl.ds(pl.multiple_of(i * rb, rb), rb)
        xb = xc.astype(BF16)
        for d in range(2):
            tr = jnp.tanh(_dot(xb, wa_ref[d]) + ba_ref[d])
            ti = jnp.tanh(_dot(xb, wx_ref[d]) + bx_ref[d])
            th = jnp.tanh(c2[d] + c2[d] * tr)
            rcp = 1.0 / (1.0 - th)
            a_sc[d, rows, :] = (1.0 + th) * rcp
            b_sc[d, rows, :] = rcp * jnp.sqrt(-th) * ((1.0 + ti) * xc)

    load_tm(0)
    load_tm(min(1, nbt - 1))

    def pre(i, c):
        conv_gates(i)
        load_tm(jnp.minimum(i + 2, nbt - 1))
        return c
    lax.fori_loop(0, nbt, pre, 0)

    def store_bm(i):
        hb = _permute_rows_f32(perm_back, hs_sc[pl.ds(pl.multiple_of(i * rb, rb), rb), :])
        for b in range(NPS):
            hs_ref[pl.ds(pl.multiple_of(b * t + i * LRU_TB, LRU_TB), LRU_TB), :] = hb[b * LRU_TB:(b + 1) * LRU_TB]

    for d in range(2):
        def trow(s):
            return pl.ds(pl.multiple_of((s if d == 0 else t - 1 - s) * NPS, NPS), NPS)

        if segmented:
            def sweep(s, carry):
                h, p = carry
                rows = trow(s)
                a = a_sc[d, rows, :]
                return a * h + b_sc[d, rows, :], a * p
            e, p = lax.fori_loop(0, t, sweep, (jnp.zeros((NPS, w), F32), jnp.ones((NPS, w), F32)),
                                 unroll=LRU_UNROLL)
            e_sc[...] = e
            p_sc[...] = p
            nb = NPS // nseg
            for b in range(nb):
                prev = h0_ref[d, b:b + 1, :]
                for k in (range(nseg) if d == 0 else range(nseg - 1, -1, -1)):
                    j = b * nseg + k
                    s_sc[j:j + 1, :] = prev
                    prev = p_sc[j:j + 1, :] * prev + e_sc[j:j + 1, :]
            h_init = s_sc[...]
        else:
            h_init = h0_ref[d]

        def scan(s, h):
            rows = trow(s)
            h = a_sc[d, rows, :] * h + b_sc[d, rows, :]
            if d == 0:
                hs_sc[rows, :] = h
            else:
                hs_sc[rows, :] += h
            return h
        if d == 0:
            h_fin = lax.fori_loop(0, t, scan, h_init, unroll=LRU_UNROLL)
        else:
            def sweep_block(k, h):
                store_bm(jnp.minimum(nbt - k, nbt - 1))
                for s in range(LRU_TB):
                    h = scan(k * LRU_TB + s, h)
                return h
            h_fin = lax.fori_loop(0, nbt, sweep_block, h_init)
            store_bm(0)
        if not segmented:
            stout_ref[d] = h_fin


def _lru(proj, x_col, width, conv_w, conv_b, wa, ba, wx, bx, lam, h0, *, t, glen, nseg):
    segmented = nseg > 1
    r = proj.shape[0]
    heads, blk = wa.shape[1], wa.shape[2]
    col = lambda hd: (0, hd)
    col3 = lambda hd: (0, 0, hd)
    out_shape = [jax.ShapeDtypeStruct((r, width), F32)]
    out_specs = [pl.BlockSpec((r, blk), col)]
    scratch = [pltpu.VMEM((r + (t // glen + 1) * CONV_LEFT * NPS, blk), F32),
               pltpu.VMEM((2, r, blk), F32), pltpu.VMEM((2, r, blk), F32), pltpu.VMEM((r, blk), F32)]
    if segmented:
        scratch += [pltpu.VMEM((NPS, blk), F32)] * 3
    else:
        out_shape.append(jax.ShapeDtypeStruct((2, NPS, width), F32))
        out_specs.append(pl.BlockSpec((2, NPS, blk), col3))
    res = pl.pallas_call(
        functools.partial(_lru_kernel, t=t, glen=glen, nseg=nseg, segmented=segmented),
        grid=(heads,),
        in_specs=[pl.BlockSpec((r, blk), lambda hd: (0, x_col // blk + hd)),
                  pl.BlockSpec((CONV_W, blk), col),
                  pl.BlockSpec((1, blk), col),
                  pl.BlockSpec((2, None, blk, blk), lambda hd: (0, hd, 0, 0)),
                  pl.BlockSpec((2, 1, blk), col3),
                  pl.BlockSpec((2, None, blk, blk), lambda hd: (0, hd, 0, 0)),
                  pl.BlockSpec((2, 1, blk), col3),
                  pl.BlockSpec((2, 1, blk), col3),
                  pl.BlockSpec((2, h0.shape[1], blk), col3)],
        out_specs=out_specs, out_shape=out_shape, scratch_shapes=scratch,
        compiler_params=_cparams("arbitrary"), name="lru",
    )(proj, conv_w, conv_b, wa, ba, wx, bx, lam, h0)
    return (res[0], res[1]) if not segmented else (res[0], None)


def _odd_out_kernel(hs_ref, gate_ref, x_ref, mod_ref, wo_ref, ng_ref, out_ref, h_ref):
    y = _dot((hs_ref[...] * _gelu(gate_ref[...])).astype(BF16), wo_ref[...])
    _mixer_residual(x_ref, mod_ref, ng_ref, y, out_ref, h_ref)


def _odd_out(hs, proj, x, mod_l, w_out, norm_next, *, t, row_base, pseq_per_cond):
    m, d = x.shape
    w = hs.shape[1]
    return pl.pallas_call(
        _odd_out_kernel, grid=(NPS,),
        in_specs=[pl.BlockSpec((t, w), lambda i: (i, 0)),
                  pl.BlockSpec((t, w), lambda i: (i, 0)),
                  pl.BlockSpec((t, d), lambda i: (i, 0)),
                  pl.BlockSpec((None, N_MOD, d), lambda i: (row_base + i // pseq_per_cond, 0, 0)),
                  pl.BlockSpec(w_out.shape, lambda i: (0, 0)),
                  pl.BlockSpec((1, d), lambda i: (0, 0))],
        out_specs=[pl.BlockSpec((t, d), lambda i: (i, 0)), pl.BlockSpec((t, d), lambda i: (i, 0))],
        out_shape=[jax.ShapeDtypeStruct((m, d), F32), jax.ShapeDtypeStruct((m, d), BF16)],
        compiler_params=_cparams("arbitrary"), name="odd_out",
    )(hs, proj, x, mod_l, w_out, norm_next)


def kernel(x_prompt, x_sample, state_s5_re, state_s5_im, state_gla, state_lru, c, c_ctx, norm_g, ada_w, ada_b, ffn_w_in, ffn_w_out, final_norm_g, ev_w_in, ev_w_out, s5_lam_re, s5_lam_im, s5_log_step, s5_b_re, s5_b_im, s5_c_re, s5_c_im, s5_d, s5_glu_w, s5_glu_b, gla_gate_w2, gla_gate_b, gla_norm_g, od_w_in, od_w_out, lru_conv_w, lru_conv_b, lru_wa, lru_ba, lru_wx, lru_bx, lru_lam):
    nbc, seq, d = x_prompt.shape
    nbl, dseq, _ = x_sample.shape
    depth = norm_g.shape[0]
    assert nbc == NPS and NPS % nbl == 0
    nseg = NPS // nbl
    tl = dseq // nseg
    grid_w = 64
    assert tl % grid_w == 0 and tl % LRU_TB == 0 and seq % LRU_TB == 0 and seq % GLA_RB == 0 and dseq % GLA_RB == 0
    sw = s5_d.shape[1]
    qk = gla_gate_w2.shape[-1]
    heads = state_gla.shape[3]
    rank = gla_gate_w2.shape[2]
    g5, p5 = s5_lam_re.shape[2], s5_lam_re.shape[3]
    main = ev_w_in.shape[2] - 2 * rank
    assert main % PROJ_TN == 0 and (g5 // S5_GP) % S5_PPG == 0

    passes = [dict(x=x_prompt.reshape(nbc * seq, d), base=0, rpc=nbc * seq, ppc=NPS, t=seq, nseg=1,
                   glen=seq, nb=nbc, seq=seq),
              dict(x=x_sample.reshape(nbl * dseq, d), base=1, rpc=dseq, ppc=nseg, t=tl, nseg=nseg,
                   glen=grid_w, nb=nbl, seq=dseq)]

    cond8 = jnp.concatenate([c_ctx[None, :], c, jnp.zeros((8 - 1 - nbl, d), F32)], axis=0)
    mod = _ada(cond8, ada_w, ada_b).reshape(depth, 8, N_MOD, d)
    fg = final_norm_g.reshape(1, d)

    new_s5_re, new_s5_im, new_gla, new_lru = [], [], [], []
    for l in range(depth):
        mod_l = mod[l]
        ng = lambda s: norm_g[l, s].reshape(1, d)
        for ps in passes:
            ps['x'] = _ffn(ps['x'], mod_l, ng(0), fg, ffn_w_in, ffn_w_out, l, 0, mi=0,
                           row_base=ps['base'], rows_per_cond=ps['rpc'], final=False)
        if l % 2 == 0:
            e = l // 2
            ev_w_in_t = jnp.swapaxes(ev_w_in, 1, 2)
            w_o = ev_w_out[e].astype(BF16)
            glu_w = s5_glu_w[e].astype(BF16)
            dkh = qk // heads
            w2p = jnp.stack([jnp.pad(gla_gate_w2[e, dd], ((dd * rank, (1 - dd) * rank), (0, 0))) for dd in range(2)])
            w2p = w2p.reshape(2, 2 * rank, heads, dkh).transpose(1, 2, 0, 3).reshape(2 * rank, 2 * qk).astype(BF16)
            gate_b = gla_gate_b[e].reshape(2, heads, dkh).transpose(1, 0, 2).reshape(1, 2 * qk)
            w1, w2t, w3, a_bc, a_ts = _s5_params(s5_lam_re[e], s5_lam_im[e], s5_log_step[e], s5_b_re[e], s5_b_im[e],
                                                 s5_c_re[e], s5_c_im[e], [ps['t'] for ps in passes])
            for pi, ps in enumerate(passes):
                t = ps['t']
                proj, glr = _norm_proj(ps['x'], mod_l, ng(1), ev_w_in_t, e, 2 * rank, mi=3,
                                       row_base=ps['base'], rows_per_cond=ps['rpc'], w_rows_out=True)
                if pi == 0:
                    h0 = jnp.zeros((2, NPS, a_bc.shape[-1]), F32)
                    s0 = None
                else:
                    h0 = _s5_state_to_cols(state_s5_re[:, e], state_s5_im[:, e])
                    s0 = state_gla[:, e]
                y_s5, s5_fin = _s5(proj, sw, w1, w2t, w3, a_bc, a_ts[pi], h0, t=t, nseg=ps['nseg'])
                o_gla, gla_fin = _gla(proj, sw, glr, w2p, gate_b, gla_norm_g[e].reshape(1, -1), s0,
                                      nb=ps['nb'], seq=ps['seq'], heads=heads, want_state=(pi == 0))
                if pi == 0:
                    sr, si = _s5_cols_to_state(s5_fin, g5, p5)
                    new_s5_re.append(sr)
                    new_s5_im.append(si)
                    new_gla.append(gla_fin)
                ps['x'], ps['h'] = _even_out(y_s5, proj, o_gla, ps['x'], mod_l, s5_d[e].reshape(1, sw), glu_w,
                                             s5_glu_b[e].reshape(1, sw), w_o, ng(2), t=t, row_base=ps['base'],
                                             pseq_per_cond=ps['ppc'])
        else:
            o = l // 2
            lw = od_w_in.shape[2] // 2
            w_o = od_w_out[o].astype(BF16)
            wa16 = (0.5 * lru_wa[o]).astype(BF16)
            wx16 = (0.5 * lru_wx[o]).astype(BF16)
            for pi, ps in enumerate(passes):
                t = ps['t']
                proj, _ = _norm_proj(ps['x'], mod_l, ng(1), od_w_in, o, 0, mi=3,
                                     row_base=ps['base'], rows_per_cond=ps['rpc'], w_rows_out=False)
                if pi == 0:
                    h0 = jnp.zeros((2, NPS, lw), F32)
                else:
                    h0 = state_lru[:, o].transpose(1, 0, 2)
                hs, lru_fin = _lru(proj, lw, lw, lru_conv_w[o], lru_conv_b[o].reshape(1, lw), wa16,
                                   0.5 * lru_ba[o].reshape(2, 1, lw), wx16, 0.5 * lru_bx[o].reshape(2, 1, lw),
                                   lru_lam[o].reshape(2, 1, lw), h0, t=t, glen=ps['glen'], nseg=ps['nseg'])
                if pi == 0:
                    new_lru.append(lru_fin.transpose(1, 0, 2))
                ps['x'], ps['h'] = _odd_out(hs, proj, ps['x'], mod_l, w_o, ng(2), t=t, row_base=ps['base'],
                                            pseq_per_cond=ps['ppc'])
        for ps in passes:
            ps['x'] = _ffn(ps['x'], mod_l, ng(2), fg, ffn_w_in, ffn_w_out, l, 1, mi=6, row_base=ps['base'],
                           rows_per_cond=ps['rpc'], final=(l == depth - 1), h_pre=ps['h'])

    y_prompt = passes[0]['x'].reshape(nbc, seq, d)
    y_sample = passes[1]['x'].reshape(nbl, dseq, d)
    return (y_prompt, y_sample, jnp.stack(new_s5_re, 1), jnp.stack(new_s5_im, 1),
            jnp.stack(new_gla, 1), jnp.stack(new_lru, 1))
```

```python
import functools
import math

import jax
import jax.numpy as jnp
from jax import lax
from jax.experimental import pallas as pl
from jax.experimental.pallas import tpu as pltpu

F32 = jnp.float32
BF16 = jnp.bfloat16
EPS = 1e-6
NPS = 16
N_MOD = 9
GLA_CHUNK = 64
GLA_TAU = 16.0
LRU_C = 8.0
VMEM_LIMIT = 58 * 1024 * 1024
ROW_CHUNK = 64
NORM_CHUNK = 64
NORM_UNROLL = 2


def _cparams(*sem):
    return pltpu.CompilerParams(dimension_semantics=sem, vmem_limit_bytes=VMEM_LIMIT)


def _dot(a, b):
    return jnp.dot(a, b, preferred_element_type=F32)


def _dot_nt(a, b):
    return lax.dot_general(a, b, (((1,), (1,)), ((), ())), preferred_element_type=F32)


def _dot_tn(a, b):
    return lax.dot_general(a, b, (((0,), (0,)), ((), ())), preferred_element_type=F32)


def _sigmoid(x):
    return 0.5 * (1.0 + jnp.tanh(0.5 * x))


def _silu(x):
    return x * _sigmoid(x)


def _gelu(x):
    return 0.5 * x * (1.0 + jnp.tanh(math.sqrt(2.0 / math.pi) * (x + 0.044715 * (x * x * x))))


def _softplus(x):
    return jnp.maximum(x, 0.0) + jnp.log1p(jnp.exp(-jnp.abs(x)))


def _rms_mod(x, g, scale, shift):
    return x * lax.rsqrt(jnp.mean(x * x, axis=-1, keepdims=True) + EPS) * (g * (1.0 + scale)) + shift


def _perm_matrix(n_outer, n_inner):
    n = n_outer * n_inner
    ro = lax.broadcasted_iota(jnp.int32, (n, n), 0)
    ci = lax.broadcasted_iota(jnp.int32, (n, n), 1)
    return jnp.where(ci == (ro % n_outer) * n_inner + ro // n_outer, 1.0, 0.0).astype(BF16)


def _permute_rows_f32(perm, x):
    hi = x.astype(BF16)
    r1 = x - hi.astype(F32)
    mid = r1.astype(BF16)
    lo = (r1 - mid.astype(F32)).astype(BF16)
    return _dot(perm, hi) + _dot(perm, mid) + _dot(perm, lo)


def _norm_rows(x_ref, mod_ref, g_ref, h_sc, mi):
    def body(r, c):
        rows = pl.ds(pl.multiple_of(r * NORM_CHUNK, NORM_CHUNK), NORM_CHUNK)
        h = _rms_mod(x_ref[rows, :], g_ref[...], mod_ref[mi + 1:mi + 2, :], mod_ref[mi:mi + 1, :])
        h_sc[rows, :] = h.astype(BF16)
        return c
    lax.fori_loop(0, x_ref.shape[0] // NORM_CHUNK, body, 0, unroll=NORM_UNROLL)


def _ada_kernel(c_ref, w_ref, b_ref, o_ref):
    ca = _silu(c_ref[...])
    o_ref[...] = _dot(ca.astype(BF16), w_ref[...].astype(BF16)) + b_ref[...]


def _ada(cond8, ada_w, ada_b3, nlayers):
    _, d, n = ada_w.shape
    tn = 1024
    return pl.pallas_call(
        _ada_kernel, grid=(nlayers, n // tn),
        in_specs=[pl.BlockSpec((8, d), lambda l, j: (0, 0)),
                  pl.BlockSpec((None, d, tn), lambda l, j: (l, 0, j)),
                  pl.BlockSpec((None, 1, tn), lambda l, j: (l, 0, j))],
        out_specs=pl.BlockSpec((None, 8, tn), lambda l, j: (l, 0, j)),
        out_shape=jax.ShapeDtypeStruct((nlayers, 8, n), F32),
        compiler_params=_cparams("arbitrary", "arbitrary"), name="ada",
    )(cond8, ada_w, ada_b3)


FFN_TM = 1024
FFN_TF = 256
FFN_NC = 512


def _ffn_kernel(x_ref, mod_ref, g_ref, fg_ref, wa_ref, wb_ref, wo_ref, *rest, mi, nf, final, has_h, has_side):
    rest = list(rest)
    h_ref = rest.pop(0) if has_h else None
    side_in = [rest.pop(0) for _ in range(3)] if has_side else None
    o_ref = rest.pop(0)
    side_ref = rest.pop(0) if has_side else None
    if not has_h:
        h_ref = rest.pop(0)
    j = pl.program_id(1)
    tm, d = x_ref.shape

    @pl.when(j == 0)
    def _():
        if not has_h:
            _norm_rows(x_ref, mod_ref, g_ref, h_ref, mi)
        o_ref[...] = jnp.zeros_like(o_ref)

    if has_side:
        c_ref, aw_ref, ab_ref = side_in
        side_ref[...] = _dot(_silu(c_ref[...]).astype(BF16), aw_ref[...].astype(BF16)) + ab_ref[...]

    h = h_ref[...]
    a = _dot(h, wa_ref[...].astype(BF16))
    b = _dot(h, wb_ref[...].astype(BF16))
    act = (_silu(a) * b).astype(BF16)
    for n in range(d // FFN_NC):
        cols = slice(n * FFN_NC, (n + 1) * FFN_NC)
        o_ref[:, cols] += _dot(act, wo_ref[:, cols].astype(BF16))

    @pl.when(j == nf - 1)
    def _():
        def body(r, c):
            rows = pl.ds(pl.multiple_of(r * ROW_CHUNK, ROW_CHUNK), ROW_CHUNK)
            y = x_ref[rows, :] + 0.5 * mod_ref[mi + 2:mi + 3, :] * o_ref[rows, :]
            if final:
                y = y * lax.rsqrt(jnp.mean(y * y, axis=-1, keepdims=True) + EPS) * fg_ref[...]
            o_ref[rows, :] = y
            return c
        lax.fori_loop(0, tm // ROW_CHUNK, body, 0)


ADA_TS = 256


def _ffn(x, mod_l, norm_g, final_g, w_in, w_out, l, k, *, mi, row_base, rows_per_cond, final, h_pre=None, side=None):
    m, d = x.shape
    f = w_out.shape[2]
    tm = min(FFN_TM, rows_per_cond)
    tf = FFN_TF
    nf = f // tf
    row = lambda i, j: (row_base + (i * tm) // rows_per_cond, 0, 0)
    once = dict(pipeline_mode=pl.Buffered(1))
    in_specs = [pl.BlockSpec((tm, d), lambda i, j: (i, 0), **once),
                pl.BlockSpec((None, N_MOD, d), row),
                pl.BlockSpec((1, d), lambda i, j: (0, 0)),
                pl.BlockSpec((1, d), lambda i, j: (0, 0)),
                pl.BlockSpec((None, None, d, tf), lambda i, j: (l, k, 0, j)),
                pl.BlockSpec((None, None, d, tf), lambda i, j: (l, k, 0, j + nf)),
                pl.BlockSpec((None, None, tf, d), lambda i, j: (l, k, j, 0))]
    args = [x, mod_l, norm_g, final_g, w_in, w_in, w_out]
    if h_pre is not None:
        in_specs.append(pl.BlockSpec((tm, d), lambda i, j: (i, 0), **once))
        args.append(h_pre)
    out_specs = [pl.BlockSpec((tm, d), lambda i, j: (i, 0))]
    out_shape = [jax.ShapeDtypeStruct((m, d), F32)]
    if side is not None:
        cond8, ada_w, ada_b3, ls = side
        n = ada_w.shape[2]
        nts = n // ADA_TS
        assert (m // tm) * nf >= nts
        tile = lambda i, j: jnp.minimum(i * nf + j, nts - 1)
        in_specs += [pl.BlockSpec((8, d), lambda i, j: (0, 0)),
                     pl.BlockSpec((None, d, ADA_TS), lambda i, j: (ls, 0, tile(i, j))),
                     pl.BlockSpec((None, 1, ADA_TS), lambda i, j: (ls, 0, tile(i, j)))]
        args += [cond8, ada_w, ada_b3]
        out_specs.append(pl.BlockSpec((8, ADA_TS), lambda i, j: (0, tile(i, j))))
        out_shape.append(jax.ShapeDtypeStruct((8, n), F32))
    res = pl.pallas_call(
        functools.partial(_ffn_kernel, mi=mi, nf=nf, final=final, has_h=h_pre is not None,
                          has_side=side is not None),
        grid=(m // tm, nf), in_specs=in_specs, out_specs=out_specs, out_shape=out_shape,
        scratch_shapes=[] if h_pre is not None else [pltpu.VMEM((tm, d), BF16)],
        compiler_params=_cparams("arbitrary", "arbitrary"), name="ffn",
    )(*args)
    return res if side is not None else res[0]


PROJ_TM = 1024
PROJ_TN = 1024


def _norm_proj_kernel(x_ref, mod_ref, g_ref, w_ref, *rest, mi, nmain, w_rows_out):
    if len(rest) == 4:
        wx_ref, o_ref, ox_ref, h_sc = rest
    else:
        (o_ref, h_sc), wx_ref, ox_ref = rest, None, None
    j = pl.program_id(1)
    mm = _dot_nt if w_rows_out else _dot

    @pl.when(j == 0)
    def _():
        _norm_rows(x_ref, mod_ref, g_ref, h_sc, mi)

    @pl.when(j < nmain)
    def _():
        o_ref[...] = mm(h_sc[...], w_ref[...].astype(BF16))

    if wx_ref is not None:
        @pl.when(j == nmain)
        def _():
            ox_ref[...] = mm(h_sc[...], wx_ref[...].astype(BF16))


def _norm_proj(x, mod_l, norm_g, w, e, n_extra, *, mi, row_base, rows_per_cond, w_rows_out):
    m, d = x.shape
    tm = min(PROJ_TM, rows_per_cond)
    tn = PROJ_TN
    nout = w.shape[1] if w_rows_out else w.shape[2]
    nmain = nout // tn
    assert nout == nmain * tn + n_extra and (n_extra == 0 or (w_rows_out and (nmain * tn) % n_extra == 0))
    nj = nmain + (1 if n_extra else 0)
    jm = lambda j: jnp.minimum(j, nmain - 1)
    wblk = (lambda rows, idx: pl.BlockSpec((None, rows, d), lambda i, j: (e, idx(j), 0))) if w_rows_out else \
           (lambda cols, idx: pl.BlockSpec((None, d, cols), lambda i, j: (e, 0, idx(j))))
    in_specs = [pl.BlockSpec((tm, d), lambda i, j: (i, 0)),
                pl.BlockSpec((None, N_MOD, d), lambda i, j: (row_base + (i * tm) // rows_per_cond, 0, 0)),
                pl.BlockSpec((1, d), lambda i, j: (0, 0)),
                wblk(tn, jm)]
    args = [x, mod_l, norm_g, w]
    out_shape = [jax.ShapeDtypeStruct((m, nmain * tn), F32)]
    out_specs = [pl.BlockSpec((tm, tn), lambda i, j: (i, jm(j)))]
    if n_extra:
        in_specs.append(wblk(n_extra, lambda j: (nmain * tn) // n_extra))
        args.append(w)
        out_shape.append(jax.ShapeDtypeStruct((m, n_extra), F32))
        out_specs.append(pl.BlockSpec((tm, n_extra), lambda i, j: (i, 0)))
    res = pl.pallas_call(
        functools.partial(_norm_proj_kernel, mi=mi, nmain=nmain, w_rows_out=w_rows_out), grid=(m // tm, nj),
        in_specs=in_specs, out_specs=out_specs, out_shape=out_shape,
        scratch_shapes=[pltpu.VMEM((tm, d), BF16)],
        compiler_params=_cparams("arbitrary", "arbitrary"), name="norm_proj",
    )(*args)
    return res if n_extra else (res[0], None)


S5_S = 8
S5_GP = 2
S5_PPG = 4
S5_UNROLL = 4


def _s5_kernel(u_ref, pc_ref, w1_ref, w2t_ref, w3_ref, a_ref, at_ref, h0_ref, y_ref, *rest, t, nseg, segmented):
    if segmented:
        g_sc, v_sc, st_sc = rest
    else:
        stout_ref, g_sc, v_sc, st_sc = rest
    nblk = t // S5_S
    npair = w1_ref.shape[1]
    cw = w1_ref.shape[2]
    hw = cw // 2
    lt = u_ref.shape[1]

    def gather(k, c):
        r = pl.ds(pl.multiple_of(k * NPS, NPS), NPS)
        for j in range(S5_S):
            g_sc[r, j * lt:(j + 1) * lt] = u_ref[pl.ds(k * S5_S + j, NPS, stride=t), :]
        return c
    lax.fori_loop(0, nblk, gather, 0)
    u = _dot(g_sc[...].astype(BF16), pc_ref[...]).astype(BF16)

    for d in range(2):
        for q in range(npair):
            v_sc[d, :, q * cw:(q + 1) * cw] = _dot(u[:, q * cw:(q + 1) * cw], w1_ref[d, q])

        def advance(state, v):
            outs = []
            for q in range(npair):
                re = slice(q * cw, q * cw + hw)
                im = slice(q * cw + hw, (q + 1) * cw)
                ar, ai = a_ref[d, :, re], a_ref[d, :, im]
                hr, hi = state[:, re], state[:, im]
                outs.append(ar * hr - ai * hi + v[:, re])
                outs.append(ar * hi + ai * hr + v[:, im])
            return jnp.concatenate(outs, axis=1)

        def rows(kk):
            k = kk if d == 0 else nblk - 1 - kk
            return pl.ds(pl.multiple_of(k * NPS, NPS), NPS)

        if segmented:
            st_sc[...] = lax.fori_loop(0, nblk, lambda kk, s: advance(s, v_sc[d, rows(kk), :]),
                                       jnp.zeros(st_sc.shape, F32), unroll=S5_UNROLL)
            nb = NPS // nseg
            for b in range(nb):
                for q in range(npair):
                    re = slice(q * cw, q * cw + hw)
                    im = slice(q * cw + hw, (q + 1) * cw)
                    pr, pi = h0_ref[d, b:b + 1, re], h0_ref[d, b:b + 1, im]
                    ar, ai = at_ref[d, :, re], at_ref[d, :, im]
                    for k in (range(nseg) if d == 0 else range(nseg - 1, -1, -1)):
                        j = b * nseg + k
                        er, ei = st_sc[j:j + 1, re], st_sc[j:j + 1, im]
                        st_sc[j:j + 1, re] = pr
                        st_sc[j:j + 1, im] = pi
                        pr, pi = ar * pr - ai * pi + er, ar * pi + ai * pr + ei
            init = st_sc[...]
        else:
            init = h0_ref[d]

        def body(kk, s):
            r = rows(kk)
            v = v_sc[d, r, :]
            v_sc[d, r, :] = s
            return advance(s, v)
        fin = lax.fori_loop(0, nblk, body, init, unroll=S5_UNROLL)
        if not segmented:
            stout_ref[d] = fin

    for q in range(npair):
        cols = slice(q * cw, (q + 1) * cw)
        hcat = jnp.concatenate([v_sc[0, :, cols], v_sc[1, :, cols]], axis=1).astype(BF16)
        g_sc[:, cols] = _dot_nt(hcat, w2t_ref[q]) + _dot(u[:, cols], w3_ref[q])

    y = g_sc[...]
    hi = y.astype(BF16)
    lo = (y - hi.astype(F32)).astype(BF16)
    g_sc[...] = _dot_nt(hi, pc_ref[...]) + _dot_nt(lo, pc_ref[...])

    def scatter(k, c):
        r = pl.ds(pl.multiple_of(k * NPS, NPS), NPS)
        for j in range(S5_S):
            y_ref[pl.ds(k * S5_S + j, NPS, stride=t), :] = g_sc[r, j * lt:(j + 1) * lt]
        return c
    lax.fori_loop(0, nblk, scatter, 0)


def _s5_col_perm(lt, gw):
    r = jnp.arange(S5_S * lt)
    j, lane = r // lt, r % lt
    dst = (lane // gw) * (S5_S * gw) + j * gw + lane % gw
    return (dst[:, None] == jnp.arange(S5_S * lt)[None, :]).astype(BF16)


def _s5(proj, width, w1, w2t, w3, a_bc, a_t, h0, *, t, nseg):
    segmented = nseg > 1
    m = proj.shape[0]
    npair, cw = w1.shape[1], w1.shape[2]
    sl = S5_PPG * cw
    lt = sl // S5_S
    r = (t // S5_S) * NPS
    pc = _s5_col_perm(lt, cw // S5_S)
    col2 = lambda g: (0, g)
    col3 = lambda g: (0, 0, g)
    out_shape = [jax.ShapeDtypeStruct((m, width), F32)]
    out_specs = [pl.BlockSpec((m, lt), col2)]
    if not segmented:
        out_shape.append(jax.ShapeDtypeStruct((2, NPS, npair * cw), F32))
        out_specs.append(pl.BlockSpec((2, NPS, sl), col3))
    res = pl.pallas_call(
        functools.partial(_s5_kernel, t=t, nseg=nseg, segmented=segmented),
        grid=(npair // S5_PPG,),
        in_specs=[pl.BlockSpec((m, lt), col2),
                  pl.BlockSpec(pc.shape, lambda g: (0, 0)),
                  pl.BlockSpec((2, S5_PPG, cw, cw), lambda g: (0, g, 0, 0)),
                  pl.BlockSpec((S5_PPG, cw, 2 * cw), lambda g: (g, 0, 0)),
                  pl.BlockSpec((S5_PPG, cw, cw), lambda g: (g, 0, 0)),
                  pl.BlockSpec((2, NPS, sl), col3),
                  pl.BlockSpec((2, 1, sl), col3),
                  pl.BlockSpec((2, h0.shape[1], sl), col3)],
        out_specs=out_specs, out_shape=out_shape,
        scratch_shapes=[pltpu.VMEM((r, sl), F32), pltpu.VMEM((2, r, sl), F32), pltpu.VMEM((NPS, sl), F32)],
        compiler_params=_cparams("arbitrary"), name="s5",
    )(proj, pc, w1, w2t, w3, a_bc, a_t, h0)
    return (res[0], res[1]) if not segmented else (res[0], None)


def _s5_params(lam_re, lam_im, log_step, b_re, b_im, c_re, c_im, t_segs):
    nd, g, p = lam_re.shape
    n = b_re.shape[-1]
    npair = g // S5_GP
    hi = lax.Precision.HIGHEST
    dt = jnp.exp(log_step)[..., None]
    z_re, z_im = lam_re * dt, lam_im * dt
    mag = jnp.exp(z_re)
    ab_re, ab_im = mag * jnp.cos(z_im), mag * jnp.sin(z_im)
    den = lam_re * lam_re + lam_im * lam_im
    n_re = ab_re - 1.0
    f_re = (n_re * lam_re + ab_im * lam_im) / den
    f_im = (ab_im * lam_re - n_re * lam_im) / den
    bb_re = f_re[..., None] * b_re - f_im[..., None] * b_im
    bb_im = f_re[..., None] * b_im + f_im[..., None] * b_re

    cw = 2 * S5_GP * p
    col = jnp.arange(cw)
    is_re = col < cw // 2

    def dup(x):
        return jnp.concatenate([x, x], axis=-1)

    zr = dup(z_re.reshape(nd, npair, S5_GP * p))
    zi = dup(z_im.reshape(nd, npair, S5_GP * p))

    def power(e):
        m = jnp.exp(zr[:, :, None, :] * e[:, None, :, None])
        ang = zi[:, :, None, :] * e[:, None, :, None]
        return m * jnp.cos(ang), m * jnp.sin(ang)

    steps = jnp.arange(S5_S, dtype=F32)
    p1r, p1i = power(jnp.stack([S5_S - 1 - steps, steps]))
    p2r, p2i = power(jnp.stack([steps + 1, S5_S - steps]))
    br, bi = [dup(x.reshape(nd, npair, S5_GP, p, n).transpose(0, 1, 4, 2, 3).reshape(nd, npair, n, S5_GP * p))
              for x in (bb_re, bb_im)]
    cr, ci = [dup(x.reshape(nd, npair, S5_GP, n, p).transpose(0, 1, 3, 2, 4).reshape(nd, npair, n, S5_GP * p))
              for x in (c_re, c_im)]
    powers = jnp.stack([p1r, p1i, p2r, p2i], axis=2)
    bc = jnp.stack([br, bi, cr, ci], axis=2)

    mag = jnp.exp(z_re[..., None] * steps)
    lr, li = mag * jnp.cos(z_im[..., None] * steps), mag * jnp.sin(z_im[..., None] * steps)
    abt_re = lr[:, :, :, None, :] * bb_re[..., None] - li[:, :, :, None, :] * bb_im[..., None]
    abt_im = lr[:, :, :, None, :] * bb_im[..., None] + li[:, :, :, None, :] * bb_re[..., None]
    kt = (jnp.einsum('dgnp,dgpmt->dgtmn', c_re, abt_re, precision=hi)
          - jnp.einsum('dgnp,dgpmt->dgtmn', c_im, abt_im, precision=hi))
    kx = kt.reshape(nd, npair, S5_GP, S5_S, n, n).transpose(0, 3, 1, 2, 4, 5).reshape(nd, S5_S, npair, S5_GP * n, n)

    def state_cols(e):
        er, ei = power(jnp.full((nd, 1), e, F32))
        return jnp.where(is_re, er, ei).reshape(nd, npair * cw)

    a_s = state_cols(float(S5_S))
    a_bc = jnp.broadcast_to(a_s[:, None, :], (nd, NPS, a_s.shape[-1]))
    a_ts = [state_cols(float(ts))[:, None, :] for ts in t_segs]
    w1, w2t, w3 = _s5_maps(powers, bc, kx)
    return w1, w2t, w3, a_bc, a_ts


def _s5_maps_kernel(p_ref, bc_ref, kx_ref, w1_ref, w2t_ref, w3_ref):
    nd, ppg = p_ref.shape[0], p_ref.shape[1]
    n, cw = bc_ref.shape[3], bc_ref.shape[4]
    sw = cw // S5_S
    col = lax.broadcasted_iota(jnp.int32, (n, cw), 1)
    is_re = col < cw // 2
    sgrp = (col // (cw // (2 * S5_GP))) % S5_GP
    ocol = lax.broadcasted_iota(jnp.int32, (sw, cw), 1)
    jcol = ocol // sw
    same_grp = (ocol // n) % S5_GP == lax.broadcasted_iota(jnp.int32, (sw, cw), 0) // n
    tile = jnp.where(lax.broadcasted_iota(jnp.int32, (n, cw), 1) % n == lax.broadcasted_iota(jnp.int32, (n, cw), 0),
                     1.0, 0.0).astype(BF16)
    for q in range(ppg):
        for d in range(nd):
            br, bi, cr, ci = [bc_ref[d, q, k] for k in range(4)]
            for j in range(S5_S):
                p1r, p1i, p2r, p2i = [p_ref[d, q, k, j:j + 1, :] for k in range(4)]
                m1 = jnp.where(is_re, p1r * br - p1i * bi, p1r * bi + p1i * br)
                m2 = jnp.where(is_re, p2r * cr - p2i * ci, -(p2r * ci + p2i * cr))
                for g in range(S5_GP):
                    rows = slice((j * S5_GP + g) * n, (j * S5_GP + g + 1) * n)
                    w1_ref[d, q, rows, :] = jnp.where(sgrp == g, m1, 0.0).astype(BF16)
                    w2t_ref[q, rows, d * cw:(d + 1) * cw] = jnp.where(sgrp == g, m2, 0.0).astype(BF16)
        taps = [[jnp.where(same_grp, _dot(kx_ref[d, lag, q].astype(BF16), tile), 0.0) for lag in range(S5_S)]
                for d in range(nd)]
        for i in range(S5_S):
            acc = jnp.zeros((sw, cw), F32)
            for lag in range(S5_S - i):
                acc = acc + jnp.where(jcol == i + lag, taps[0][lag], 0.0)
            for lag in range(i + 1):
                acc = acc + jnp.where(jcol == i - lag, taps[1][lag], 0.0)
            w3_ref[q, i * sw:(i + 1) * sw, :] = acc.astype(BF16)


def _s5_maps(powers, bc, kx):
    nd, npair, _, _, cw = powers.shape
    n = bc.shape[3]
    rows = S5_S * S5_GP * n
    blk5 = lambda a: pl.BlockSpec((nd, S5_PPG) + a.shape[2:], lambda g: (0, g, 0, 0, 0))
    return pl.pallas_call(
        _s5_maps_kernel, grid=(npair // S5_PPG,),
        in_specs=[blk5(powers), blk5(bc),
                  pl.BlockSpec((nd, S5_S, S5_PPG) + kx.shape[3:], lambda g: (0, 0, g, 0, 0))],
        out_specs=[pl.BlockSpec((nd, S5_PPG, rows, cw), lambda g: (0, g, 0, 0)),
                   pl.BlockSpec((S5_PPG, rows, nd * cw), lambda g: (g, 0, 0)),
                   pl.BlockSpec((S5_PPG, rows, cw), lambda g: (g, 0, 0))],
        out_shape=[jax.ShapeDtypeStruct((nd, npair, rows, cw), BF16),
                   jax.ShapeDtypeStruct((npair, rows, nd * cw), BF16),
                   jax.ShapeDtypeStruct((npair, rows, cw), BF16)],
        compiler_params=_cparams("arbitrary"), name="s5_maps",
    )(powers, bc, kx)


def _s5_state_to_cols(s_re, s_im):
    b, nd, g, p = s_re.shape
    npair = g // S5_GP
    st = jnp.stack([s_re.reshape(b, nd, npair, S5_GP * p), s_im.reshape(b, nd, npair, S5_GP * p)], axis=3)
    return st.reshape(b, nd, -1).transpose(1, 0, 2)


def _s5_cols_to_state(st, g, p):
    nd, b, _ = st.shape
    npair = g // S5_GP
    st = st.reshape(nd, b, npair, 2, S5_GP, p).transpose(3, 1, 0, 2, 4, 5).reshape(2, b, nd, g, p)
    return st[0], st[1]


GLA_RB = 256


GLA_HPS = 4


def _gla_kernel(q_ref, k_ref, v_ref, g_ref, glr_ref, w2_ref, gb_ref, ng_ref, *rest, seq, zero_init, want_state):
    rest = list(rest)
    s0_ref = None if zero_init else rest.pop(0)
    o_ref = rest.pop(0)
    sfin_ref = rest.pop(0) if want_state else None
    dk = q_ref.shape[1] // GLA_HPS
    dv = v_ref.shape[1] // GLA_HPS
    for hh in range(GLA_HPS):
        lk, lv, l2 = pl.ds(hh * dk, dk), pl.ds(hh * dv, dv), pl.ds(hh * 2 * dk, 2 * dk)
        args = [q_ref.at[:, lk], k_ref.at[:, lk], v_ref.at[:, lv], g_ref.at[:, lv], glr_ref,
                w2_ref.at[:, l2], gb_ref.at[:, l2], ng_ref]
        if not zero_init:
            args.append(s0_ref.at[:, hh])
        args.append(o_ref.at[:, lv])
        if want_state:
            args.append(sfin_ref.at[:, hh])
        _gla_head(*args, *[sc.at[hh] for sc in rest], seq=seq, zero_init=zero_init, want_state=want_state)


def _gla_head(q_ref, k_ref, v_ref, g_ref, glr_ref, w2_ref, gb_ref, ng_ref, *rest, seq, zero_init, want_state):
    rest = list(rest)
    s0_ref = None if zero_init else rest.pop(0)
    o_ref = rest.pop(0)
    sfin_ref = rest.pop(0) if want_state else None
    qt_sc, ke_sc, dec_sc, osum_sc = rest
    dk = q_ref.shape[1]
    dv = v_ref.shape[1]
    n = seq // GLA_CHUNK
    scale = dk ** -0.5
    ri = lax.broadcasted_iota(jnp.int32, (GLA_RB, GLA_RB), 0)
    ci = lax.broadcasted_iota(jnp.int32, (GLA_RB, GLA_RB), 1)
    same = (ri // GLA_CHUNK) == (ci // GLA_CHUNK)
    lower = same & (ci <= ri)
    upper = same & (ci >= ri)
    ones_blk = jnp.where(same, 1.0, 0.0).astype(BF16)
    tri = jnp.where(lower, 1.0, 0.0).astype(BF16)

    for rb in range(seq // GLA_RB):
        rows = slice(rb * GLA_RB, (rb + 1) * GLA_RB)
        x = _dot(glr_ref[rows, :].astype(BF16), w2_ref[...]) + gb_ref[...]
        la = (jnp.minimum(x, 0.0) - jnp.log1p(jnp.exp(-jnp.abs(x)))) * (1.0 / GLA_TAU)
        la_hi = la.astype(BF16)
        la_lo = (la - la_hi.astype(F32)).astype(BF16)
        pre = _dot(tri, la_hi) + _dot(tri, la_lo)
        tot = _dot(ones_blk, la_hi) + _dot(ones_blk, la_lo)
        cum = (pre[:, :dk], tot[:, dk:] - pre[:, dk:] + la[:, dk:])
        qf = q_ref[rows, :] * scale
        kf = k_ref[rows, :]
        att = None
        for d in range(2):
            td = tot[:, d * dk:(d + 1) * dk]
            q_t = (qf * jnp.exp(cum[d])).astype(BF16)
            k_t = (kf * jnp.exp(-cum[d])).astype(BF16)
            qt_sc[d, rows, :] = q_t
            ke_sc[d, rows, :] = (kf * jnp.exp(td - cum[d])).astype(BF16)
            dec_sc[d, rows, :] = jnp.exp(td)
            a_d = jnp.where(lower if d == 0 else upper, _dot_nt(q_t, k_t), 0.0)
            att = a_d if att is None else att + a_d
        osum_sc[rows, :] = _dot(att.astype(BF16), v_ref[rows, :].astype(BF16))

    st = [jnp.zeros((dv, dk), F32) if zero_init else s0_ref[d].T for d in range(2)]
    for c in range(n):
        for d in range(2):
            cc = c if d == 0 else n - 1 - c
            rows = slice(cc * GLA_CHUNK, (cc + 1) * GLA_CHUNK)
            osum_sc[rows, :] += _dot_nt(qt_sc[d, rows, :], st[d].astype(BF16))
            kv = _dot_tn(v_ref[rows, :].astype(BF16), ke_sc[d, rows, :])
            st[d] = st[d] * dec_sc[d, cc * GLA_CHUNK:cc * GLA_CHUNK + 1, :] + kv
    if want_state:
        for d in range(2):
            sfin_ref[d] = st[d].T

    for rb in range(seq // GLA_RB):
        rows = slice(rb * GLA_RB, (rb + 1) * GLA_RB)
        o = osum_sc[rows, :]
        o = o * lax.rsqrt(jnp.mean(o * o, axis=-1, keepdims=True) + EPS) * ng_ref[...]
        o_ref[rows, :] = (o * _silu(g_ref[rows, :])).astype(BF16)


def _gla(proj, q_col, glr, w2p, gate_b, norm_g, s0, *, nb, seq, heads, want_state):
    qk = w2p.shape[-1] // 2
    dk = qk // heads
    dv = norm_g.shape[1]
    vdim = dv * heads
    zero_init = s0 is None
    hp = GLA_HPS
    bk, bv = hp * dk, hp * dv
    assert heads % hp == 0 and q_col % bk == 0 and qk % bk == 0 and (q_col + 2 * qk) % bv == 0 and vdim % bv == 0
    in_specs = [pl.BlockSpec((seq, bk), lambda b, h: (b, q_col // bk + h)),
                pl.BlockSpec((seq, bk), lambda b, h: (b, (q_col + qk) // bk + h)),
                pl.BlockSpec((seq, bv), lambda b, h: (b, (q_col + 2 * qk) // bv + h)),
                pl.BlockSpec((seq, bv), lambda b, h: (b, (q_col + 2 * qk + vdim) // bv + h)),
                pl.BlockSpec((seq, glr.shape[1]), lambda b, h: (b, 0)),
                pl.BlockSpec((w2p.shape[0], 2 * bk), lambda b, h: (0, h)),
                pl.BlockSpec((1, 2 * bk), lambda b, h: (0, h)),
                pl.BlockSpec((1, dv), lambda b, h: (0, 0))]
    args = [proj, proj, proj, proj, glr, w2p, gate_b, norm_g]
    if not zero_init:
        in_specs.append(pl.BlockSpec((None, 2, hp, dk, dv), lambda b, h: (b, 0, h, 0, 0)))
        args.append(s0)
    out_shape = [jax.ShapeDtypeStruct((nb * seq, vdim), BF16)]
    out_specs = [pl.BlockSpec((seq, bv), lambda b, h: (b, h))]
    if want_state:
        out_shape.append(jax.ShapeDtypeStruct((nb, 2, heads, dk, dv), F32))
        out_specs.append(pl.BlockSpec((None, 2, hp, dk, dv), lambda b, h: (b, 0, h, 0, 0)))
    res = pl.pallas_call(
        functools.partial(_gla_kernel, seq=seq, zero_init=zero_init, want_state=want_state),
        grid=(nb, heads // hp), in_specs=in_specs, out_specs=out_specs, out_shape=out_shape,
        scratch_shapes=[pltpu.VMEM((hp, 2, seq, dk), BF16), pltpu.VMEM((hp, 2, seq, dk), BF16),
                        pltpu.VMEM((hp, 2, seq, dk), F32), pltpu.VMEM((hp, seq, dv), F32)],
        compiler_params=_cparams("arbitrary", "arbitrary"), name="gla",
    )(*args)
    return (res[0], res[1]) if want_state else (res[0], None)


def _mixer_residual(x_ref, mod_ref, ng_ref, y, out_ref, h_ref):
    xn = x_ref[...] + mod_ref[5:6, :] * y
    out_ref[...] = xn
    h_ref[...] = _rms_mod(xn, ng_ref[...], mod_ref[7:8, :], mod_ref[6:7, :]).astype(BF16)


def _even_out_kernel(y_ref, u_ref, o_ref, x_ref, mod_ref, sd_ref, gw_ref, gb_ref, wo_ref, ng_ref, out_ref, h_ref):
    sw = u_ref.shape[1]
    ys = _gelu(y_ref[...] + sd_ref[...] * u_ref[...])
    ys = ys * _sigmoid(_dot(ys.astype(BF16), gw_ref[...]) + gb_ref[...])
    y = _dot(ys.astype(BF16), wo_ref[:sw, :]) + _dot(o_ref[...], wo_ref[sw:, :])
    _mixer_residual(x_ref, mod_ref, ng_ref, y, out_ref, h_ref)


def _even_out(y_s5, proj, o_gla, x, mod_l, s5_d, glu_w, glu_b, w_out, norm_next, *, t, row_base, pseq_per_cond):
    m, d = x.shape
    sw = y_s5.shape[1]
    return pl.pallas_call(
        _even_out_kernel, grid=(NPS,),
        in_specs=[pl.BlockSpec((t, sw), lambda i: (i, 0)),
                  pl.BlockSpec((t, sw), lambda i: (i, 0)),
                  pl.BlockSpec((t, o_gla.shape[1]), lambda i: (i, 0)),
                  pl.BlockSpec((t, d), lambda i: (i, 0)),
                  pl.BlockSpec((None, N_MOD, d), lambda i: (row_base + i // pseq_per_cond, 0, 0)),
                  pl.BlockSpec((1, sw), lambda i: (0, 0)),
                  pl.BlockSpec((sw, sw), lambda i: (0, 0)),
                  pl.BlockSpec((1, sw), lambda i: (0, 0)),
                  pl.BlockSpec(w_out.shape, lambda i: (0, 0)),
                  pl.BlockSpec((1, d), lambda i: (0, 0))],
        out_specs=[pl.BlockSpec((t, d), lambda i: (i, 0)), pl.BlockSpec((t, d), lambda i: (i, 0))],
        out_shape=[jax.ShapeDtypeStruct((m, d), F32), jax.ShapeDtypeStruct((m, d), BF16)],
        compiler_params=_cparams("arbitrary"), name="even_out",
    )(y_s5, proj, o_gla, x, mod_l, s5_d, glu_w, glu_b, w_out, norm_next)


LRU_TB = 16
CONV_W = 4
CONV_LEFT = 2
LRU_UNROLL = 8


def _lru_kernel(x_ref, cw_ref, cb_ref, wa_ref, ba_ref, wx_ref, bx_ref, lam_ref, h0_ref, hs_ref, *rest,
                t, glen, nseg, segmented):
    if segmented:
        xp_sc, a_sc, b_sc, hs_sc, e_sc, p_sc, s_sc = rest
    else:
        stout_ref, xp_sc, a_sc, b_sc, hs_sc = rest
    r, w = x_ref.shape
    rb = LRU_TB * NPS
    pad = CONV_LEFT * NPS
    perm = _perm_matrix(NPS, LRU_TB)
    perm_back = _perm_matrix(LRU_TB, NPS)

    for g in range(t // glen + 1):
        xp_sc[g * (glen * NPS + pad):g * (glen * NPS + pad) + pad, :] = jnp.zeros((pad, w), F32)

    def xp_row(i):
        return pl.multiple_of(pad * (1 + (i * LRU_TB) // glen) + i * rb, NPS)

    nbt = t // LRU_TB

    def load_tm(i):
        xin = jnp.concatenate([x_ref[pl.ds(pl.multiple_of(b * t + i * LRU_TB, LRU_TB), LRU_TB), :]
                               for b in range(NPS)], axis=0)
        xp_sc[pl.ds(xp_row(i), rb), :] = _permute_rows_f32(perm, xin)

    c2 = [(-0.25 * LRU_C) * _softplus(-lam_ref[d]) for d in range(2)]

    def conv_gates(i):
        xc = jnp.zeros((rb, w), F32) + cb_ref[...]
        for kk in range(CONV_W):
            xs = xp_sc[pl.ds(pl.multiple_of(xp_row(i) + (kk - CONV_LEFT) * NPS, NPS), rb), :]
            xc = xc + xs * cw_ref[kk:kk + 1, :]
        rows = pl.ds(pl.multiple_of(i * rb, rb), rb)
        xb = xc.astype(BF16)
        for d in range(2):
            tr = jnp.tanh(_dot(xb, wa_ref[d]) + ba_ref[d])
            ti = jnp.tanh(_dot(xb, wx_ref[d]) + bx_ref[d])
            th = jnp.tanh(c2[d] + c2[d] * tr)
            rcp = 1.0 / (1.0 - th)
            a_sc[d, rows, :] = (1.0 + th) * rcp
            b_sc[d, rows, :] = rcp * jnp.sqrt(-th) * ((1.0 + ti) * xc)

    load_tm(0)
    load_tm(min(1, nbt - 1))

    def pre(i, c):
        conv_gates(i)
        load_tm(jnp.minimum(i + 2, nbt - 1))
        return c
    lax.fori_loop(0, nbt, pre, 0)

    def store_bm(i):
        hb = _permute_rows_f32(perm_back, hs_sc[pl.ds(pl.multiple_of(i * rb, rb), rb), :])
        for b in range(NPS):
            hs_ref[pl.ds(pl.multiple_of(b * t + i * LRU_TB, LRU_TB), LRU_TB), :] = hb[b * LRU_TB:(b + 1) * LRU_TB]

    for d in range(2):
        def trow(s):
            return pl.ds(pl.multiple_of((s if d == 0 else t - 1 - s) * NPS, NPS), NPS)

        if segmented:
            def sweep(s, carry):
                h, p = carry
                rows = trow(s)
                a = a_sc[d, rows, :]
                return a * h + b_sc[d, rows, :], a * p
            e, p = lax.fori_loop(0, t, sweep, (jnp.zeros((NPS, w), F32), jnp.ones((NPS, w), F32)),
                                 unroll=LRU_UNROLL)
            e_sc[...] = e
            p_sc[...] = p
            nb = NPS // nseg
            for b in range(nb):
                prev = h0_ref[d, b:b + 1, :]
                for k in (range(nseg) if d == 0 else range(nseg - 1, -1, -1)):
                    j = b * nseg + k
                    s_sc[j:j + 1, :] = prev
                    prev = p_sc[j:j + 1, :] * prev + e_sc[j:j + 1, :]
            h_init = s_sc[...]
        else:
            h_init = h0_ref[d]

        def scan(s, h):
            rows = trow(s)
            h = a_sc[d, rows, :] * h + b_sc[d, rows, :]
            if d == 0:
                hs_sc[rows, :] = h
            else:
                hs_sc[rows, :] += h
            return h
        if d == 0:
            h_fin = lax.fori_loop(0, t, scan, h_init, unroll=LRU_UNROLL)
        else:
            def sweep_block(k, h):
                store_bm(jnp.minimum(nbt - k, nbt - 1))
                for s in range(LRU_TB):
                    h = scan(k * LRU_TB + s, h)
                return h
            h_fin = lax.fori_loop(0, nbt, sweep_block, h_init)
            store_bm(0)
        if not segmented:
            stout_ref[d] = h_fin


def _lru(proj, x_col, width, conv_w, conv_b, wa, ba, wx, bx, lam, h0, *, t, glen, nseg):
    segmented = nseg > 1
    r = proj.shape[0]
    heads, blk = wa.shape[1], wa.shape[2]
    col = lambda hd: (0, hd)
    col3 = lambda hd: (0, 0, hd)
    out_shape = [jax.ShapeDtypeStruct((r, width), F32)]
    out_specs = [pl.BlockSpec((r, blk), col)]
    scratch = [pltpu.VMEM((r + (t // glen + 1) * CONV_LEFT * NPS, blk), F32),
               pltpu.VMEM((2, r, blk), F32), pltpu.VMEM((2, r, blk), F32), pltpu.VMEM((r, blk), F32)]
    if segmented:
        scratch += [pltpu.VMEM((NPS, blk), F32)] * 3
    else:
        out_shape.append(jax.ShapeDtypeStruct((2, NPS, width), F32))
        out_specs.append(pl.BlockSpec((2, NPS, blk), col3))
    res = pl.pallas_call(
        functools.partial(_lru_kernel, t=t, glen=glen, nseg=nseg, segmented=segmented),
        grid=(heads,),
        in_specs=[pl.BlockSpec((r, blk), lambda hd: (0, x_col // blk + hd)),
                  pl.BlockSpec((CONV_W, blk), col),
                  pl.BlockSpec((1, blk), col),
                  pl.BlockSpec((2, None, blk, blk), lambda hd: (0, hd, 0, 0)),
                  pl.BlockSpec((2, 1, blk), col3),
                  pl.BlockSpec((2, None, blk, blk), lambda hd: (0, hd, 0, 0)),
                  pl.BlockSpec((2, 1, blk), col3),
                  pl.BlockSpec((2, 1, blk), col3),
                  pl.BlockSpec((2, h0.shape[1], blk), col3)],
        out_specs=out_specs, out_shape=out_shape, scratch_shapes=scratch,
        compiler_params=_cparams("arbitrary"), name="lru",
    )(proj, conv_w, conv_b, wa, ba, wx, bx, lam, h0)
    return (res[0], res[1]) if not segmented else (res[0], None)


def _odd_out_kernel(hs_ref, gate_ref, x_ref, mod_ref, wo_ref, ng_ref, out_ref, h_ref):
    y = _dot((hs_ref[...] * _gelu(gate_ref[...])).astype(BF16), wo_ref[...])
    _mixer_residual(x_ref, mod_ref, ng_ref, y, out_ref, h_ref)


def _odd_out(hs, proj, x, mod_l, w_out, norm_next, *, t, row_base, pseq_per_cond):
    m, d = x.shape
    w = hs.shape[1]
    return pl.pallas_call(
        _odd_out_kernel, grid=(NPS,),
        in_specs=[pl.BlockSpec((t, w), lambda i: (i, 0)),
                  pl.BlockSpec((t, w), lambda i: (i, 0)),
                  pl.BlockSpec((t, d), lambda i: (i, 0)),
                  pl.BlockSpec((None, N_MOD, d), lambda i: (row_base + i // pseq_per_cond, 0, 0)),
                  pl.BlockSpec(w_out.shape, lambda i: (0, 0)),
                  pl.BlockSpec((1, d), lambda i: (0, 0))],
        out_specs=[pl.BlockSpec((t, d), lambda i: (i, 0)), pl.BlockSpec((t, d), lambda i: (i, 0))],
        out_shape=[jax.ShapeDtypeStruct((m, d), F32), jax.ShapeDtypeStruct((m, d), BF16)],
        compiler_params=_cparams("arbitrary"), name="odd_out",
    )(hs, proj, x, mod_l, w_out, norm_next)


def kernel(x_prompt, x_sample, state_s5_re, state_s5_im, state_gla, state_lru, c, c_ctx, norm_g, ada_w, ada_b, ffn_w_in, ffn_w_out, final_norm_g, ev_w_in, ev_w_out, s5_lam_re, s5_lam_im, s5_log_step, s5_b_re, s5_b_im, s5_c_re, s5_c_im, s5_d, s5_glu_w, s5_glu_b, gla_gate_w2, gla_gate_b, gla_norm_g, od_w_in, od_w_out, lru_conv_w, lru_conv_b, lru_wa, lru_ba, lru_wx, lru_bx, lru_lam):
    nbc, seq, d = x_prompt.shape
    nbl, dseq, _ = x_sample.shape
    depth = norm_g.shape[0]
    assert nbc == NPS and NPS % nbl == 0
    nseg = NPS // nbl
    tl = dseq // nseg
    grid_w = 64
    assert tl % grid_w == 0 and tl % LRU_TB == 0 and seq % LRU_TB == 0 and seq % GLA_RB == 0 and dseq % GLA_RB == 0
    sw = s5_d.shape[1]
    qk = gla_gate_w2.shape[-1]
    heads = state_gla.shape[3]
    rank = gla_gate_w2.shape[2]
    g5, p5 = s5_lam_re.shape[2], s5_lam_re.shape[3]
    main = ev_w_in.shape[2] - 2 * rank
    assert main % PROJ_TN == 0 and (g5 // S5_GP) % S5_PPG == 0

    passes = [dict(x=x_prompt.reshape(nbc * seq, d), base=0, rpc=nbc * seq, ppc=NPS, t=seq, nseg=1,
                   glen=seq, nb=nbc, seq=seq),
              dict(x=x_sample.reshape(nbl * dseq, d), base=1, rpc=dseq, ppc=nseg, t=tl, nseg=nseg,
                   glen=grid_w, nb=nbl, seq=dseq)]

    cond8 = jnp.concatenate([c_ctx[None, :], c, jnp.zeros((8 - 1 - nbl, d), F32)], axis=0)
    ada_b3 = ada_b.reshape(depth, 1, -1)
    mod_l = _ada(cond8, ada_w, ada_b3, 1)[0].reshape(8, N_MOD, d)
    fg = final_norm_g.reshape(1, d)

    new_s5_re, new_s5_im, new_gla, new_lru = [], [], [], []
    for l in range(depth):
        ng = lambda s: norm_g[l, s].reshape(1, d)
        mod_next = None
        for pi, ps in enumerate(passes):
            side = (cond8, ada_w, ada_b3, l + 1) if (pi == 0 and l + 1 < depth) else None
            res = _ffn(ps['x'], mod_l, ng(0), fg, ffn_w_in, ffn_w_out, l, 0, mi=0,
                       row_base=ps['base'], rows_per_cond=ps['rpc'], final=False, side=side)
            if side is not None:
                res, mod_next = res[0], res[1].reshape(8, N_MOD, d)
            ps['x'] = res
        if l % 2 == 0:
            e = l // 2
            ev_w_in_t = jnp.swapaxes(ev_w_in, 1, 2)
            w_o = ev_w_out[e].astype(BF16)
            glu_w = s5_glu_w[e].astype(BF16)
            dkh = qk // heads
            w2p = jnp.stack([jnp.pad(gla_gate_w2[e, dd], ((dd * rank, (1 - dd) * rank), (0, 0))) for dd in range(2)])
            w2p = w2p.reshape(2, 2 * rank, heads, dkh).transpose(1, 2, 0, 3).reshape(2 * rank, 2 * qk).astype(BF16)
            gate_b = gla_gate_b[e].reshape(2, heads, dkh).transpose(1, 0, 2).reshape(1, 2 * qk)
            w1, w2t, w3, a_bc, a_ts = _s5_params(s5_lam_re[e], s5_lam_im[e], s5_log_step[e], s5_b_re[e], s5_b_im[e],
                                                 s5_c_re[e], s5_c_im[e], [ps['t'] for ps in passes])
            for pi, ps in enumerate(passes):
                t = ps['t']
                proj, glr = _norm_proj(ps['x'], mod_l, ng(1), ev_w_in_t, e, 2 * rank, mi=3,
                                       row_base=ps['base'], rows_per_cond=ps['rpc'], w_rows_out=True)
                if pi == 0:
                    h0 = jnp.zeros((2, NPS, a_bc.shape[-1]), F32)
                    s0 = None
                else:
                    h0 = _s5_state_to_cols(state_s5_re[:, e], state_s5_im[:, e])
                    s0 = state_gla[:, e]
                y_s5, s5_fin = _s5(proj, sw, w1, w2t, w3, a_bc, a_ts[pi], h0, t=t, nseg=ps['nseg'])
                o_gla, gla_fin = _gla(proj, sw, glr, w2p, gate_b, gla_norm_g[e].reshape(1, -1), s0,
                                      nb=ps['nb'], seq=ps['seq'], heads=heads, want_state=(pi == 0))
                if pi == 0:
                    sr, si = _s5_cols_to_state(s5_fin, g5, p5)
                    new_s5_re.append(sr)
                    new_s5_im.append(si)
                    new_gla.append(gla_fin)
                ps['x'], ps['h'] = _even_out(y_s5, proj, o_gla, ps['x'], mod_l, s5_d[e].reshape(1, sw), glu_w,
                                             s5_glu_b[e].reshape(1, sw), w_o, ng(2), t=t, row_base=ps['base'],
                                             pseq_per_cond=ps['ppc'])
        else:
            o = l // 2
            lw = od_w_in.shape[2] // 2
            w_o = od_w_out[o].astype(BF16)
            wa16 = (0.5 * lru_wa[o]).astype(BF16)
            wx16 = (0.5 * lru_wx[o]).astype(BF16)
            for pi, ps in enumerate(passes):
                t = ps['t']
                proj, _ = _norm_proj(ps['x'], mod_l, ng(1), od_w_in, o, 0, mi=3,
                                     row_base=ps['base'], rows_per_cond=ps['rpc'], w_rows_out=False)
                if pi == 0:
                    h0 = jnp.zeros((2, NPS, lw), F32)
                else:
                    h0 = state_lru[:, o].transpose(1, 0, 2)
                hs, lru_fin = _lru(proj, lw, lw, lru_conv_w[o], lru_conv_b[o].reshape(1, lw), wa16,
                                   0.5 * lru_ba[o].reshape(2, 1, lw), wx16, 0.5 * lru_bx[o].reshape(2, 1, lw),
                                   lru_lam[o].reshape(2, 1, lw), h0, t=t, glen=ps['glen'], nseg=ps['nseg'])
                if pi == 0:
                    new_lru.append(lru_fin.transpose(1, 0, 2))
                ps['x'], ps['h'] = _odd_out(hs, proj, ps['x'], mod_l, w_o, ng(2), t=t, row_base=ps['base'],
                                            pseq_per_cond=ps['ppc'])
        for ps in passes:
            ps['x'] = _ffn(ps['x'], mod_l, ng(2), fg, ffn_w_in, ffn_w_out, l, 1, mi=6, row_base=ps['base'],
                           rows_per_cond=ps['rpc'], final=(l == depth - 1), h_pre=ps['h'])
        mod_l = mod_next

    y_prompt = passes[0]['x'].reshape(nbc, seq, d)
    y_sample = passes[1]['x'].reshape(nbl, dseq, d)
    return (y_prompt, y_sample, jnp.stack(new_s5_re, 1), jnp.stack(new_s5_im, 1),
            jnp.stack(new_gla, 1), jnp.stack(new_lru, 1))
```

```python
import functools
import math

import jax
import jax.numpy as jnp
from jax import lax
from jax.experimental import pallas as pl
from jax.experimental.pallas import tpu as pltpu

F32 = jnp.float32
BF16 = jnp.bfloat16
EPS = 1e-6
NPS = 16
N_MOD = 9
GLA_CHUNK = 64
GLA_TAU = 16.0
LRU_C = 8.0
VMEM_LIMIT = 58 * 1024 * 1024
ROW_CHUNK = 64
NORM_CHUNK = 64
NORM_UNROLL = 2


def _cparams(*sem):
    return pltpu.CompilerParams(dimension_semantics=sem, vmem_limit_bytes=VMEM_LIMIT)


def _dot(a, b):
    return jnp.dot(a, b, preferred_element_type=F32)


def _dot_nt(a, b):
    return lax.dot_general(a, b, (((1,), (1,)), ((), ())), preferred_element_type=F32)


def _dot_tn(a, b):
    return lax.dot_general(a, b, (((0,), (0,)), ((), ())), preferred_element_type=F32)


def _sigmoid(x):
    return 0.5 * (1.0 + jnp.tanh(0.5 * x))


def _silu(x):
    return x * _sigmoid(x)


def _gelu(x):
    return 0.5 * x * (1.0 + jnp.tanh(math.sqrt(2.0 / math.pi) * (x + 0.044715 * (x * x * x))))


def _softplus(x):
    return jnp.maximum(x, 0.0) + jnp.log1p(jnp.exp(-jnp.abs(x)))


def _rms_mod(x, g, scale, shift):
    return x * lax.rsqrt(jnp.mean(x * x, axis=-1, keepdims=True) + EPS) * (g * (1.0 + scale)) + shift


def _perm_matrix(n_outer, n_inner):
    n = n_outer * n_inner
    ro = lax.broadcasted_iota(jnp.int32, (n, n), 0)
    ci = lax.broadcasted_iota(jnp.int32, (n, n), 1)
    return jnp.where(ci == (ro % n_outer) * n_inner + ro // n_outer, 1.0, 0.0).astype(BF16)


def _permute_rows_f32(perm, x):
    hi = x.astype(BF16)
    r1 = x - hi.astype(F32)
    mid = r1.astype(BF16)
    lo = (r1 - mid.astype(F32)).astype(BF16)
    return _dot(perm, hi) + _dot(perm, mid) + _dot(perm, lo)


def _norm_rows(x_ref, mod_ref, g_ref, h_sc, mi):
    def body(r, c):
        rows = pl.ds(pl.multiple_of(r * NORM_CHUNK, NORM_CHUNK), NORM_CHUNK)
        h = _rms_mod(x_ref[rows, :], g_ref[...], mod_ref[mi + 1:mi + 2, :], mod_ref[mi:mi + 1, :])
        h_sc[rows, :] = h.astype(BF16)
        return c
    lax.fori_loop(0, x_ref.shape[0] // NORM_CHUNK, body, 0, unroll=NORM_UNROLL)


def _ada_kernel(c_ref, w_ref, b_ref, o_ref):
    ca = _silu(c_ref[...])
    o_ref[...] = _dot(ca.astype(BF16), w_ref[...].astype(BF16)) + b_ref[...]


def _ada(cond8, ada_w, ada_b3, ncols):
    _, d, n = ada_w.shape
    tn = 1024
    assert ncols % tn == 0
    return pl.pallas_call(
        _ada_kernel, grid=(ncols // tn,),
        in_specs=[pl.BlockSpec((8, d), lambda j: (0, 0)),
                  pl.BlockSpec((None, d, tn), lambda j: (0, 0, j)),
                  pl.BlockSpec((None, 1, tn), lambda j: (0, 0, j))],
        out_specs=pl.BlockSpec((8, tn), lambda j: (0, j)),
        out_shape=jax.ShapeDtypeStruct((8, ncols), F32),
        compiler_params=_cparams("arbitrary"), name="ada",
    )(cond8, ada_w, ada_b3)


FFN_TM = 1024
FFN_TF = 256
FFN_NC = 512


def _ffn_kernel(x_ref, mod_ref, g_ref, fg_ref, wa_ref, wb_ref, wo_ref, *rest, mi, nf, final, has_h, has_side):
    rest = list(rest)
    h_ref = rest.pop(0) if has_h else None
    side_in = [rest.pop(0) for _ in range(3)] if has_side else None
    o_ref = rest.pop(0)
    side_ref = rest.pop(0) if has_side else None
    if not has_h:
        h_ref = rest.pop(0)
    j = pl.program_id(1)
    tm, d = x_ref.shape

    @pl.when(j == 0)
    def _():
        if not has_h:
            _norm_rows(x_ref, mod_ref, g_ref, h_ref, mi)
        o_ref[...] = jnp.zeros_like(o_ref)

    if has_side:
        c_ref, aw_ref, ab_ref = side_in
        side_ref[...] = _dot(_silu(c_ref[...]).astype(BF16), aw_ref[...].astype(BF16)) + ab_ref[...]

    h = h_ref[...]
    a = _dot(h, wa_ref[...].astype(BF16))
    b = _dot(h, wb_ref[...].astype(BF16))
    act = (_silu(a) * b).astype(BF16)
    for n in range(d // FFN_NC):
        cols = slice(n * FFN_NC, (n + 1) * FFN_NC)
        o_ref[:, cols] += _dot(act, wo_ref[:, cols].astype(BF16))

    @pl.when(j == nf - 1)
    def _():
        def body(r, c):
            rows = pl.ds(pl.multiple_of(r * ROW_CHUNK, ROW_CHUNK), ROW_CHUNK)
            y = x_ref[rows, :] + 0.5 * mod_ref[mi + 2:mi + 3, :] * o_ref[rows, :]
            if final:
                y = y * lax.rsqrt(jnp.mean(y * y, axis=-1, keepdims=True) + EPS) * fg_ref[...]
            o_ref[rows, :] = y
            return c
        lax.fori_loop(0, tm // ROW_CHUNK, body, 0)


ADA_TS = 256


def _ffn(x, mod_l, norm_g, final_g, w_in, w_out, l, k, *, mi, row_base, rows_per_cond, final, h_pre=None, side=None):
    m, d = x.shape
    f = w_out.shape[2]
    tm = min(FFN_TM, rows_per_cond)
    tf = FFN_TF
    nf = f // tf
    row = lambda i, j: (row_base + (i * tm) // rows_per_cond, 0, 0)
    once = dict(pipeline_mode=pl.Buffered(1))
    in_specs = [pl.BlockSpec((tm, d), lambda i, j: (i, 0), **once),
                pl.BlockSpec((None, N_MOD, d), row),
                pl.BlockSpec((1, d), lambda i, j: (0, 0)),
                pl.BlockSpec((1, d), lambda i, j: (0, 0)),
                pl.BlockSpec((None, None, d, tf), lambda i, j: (l, k, 0, j)),
                pl.BlockSpec((None, None, d, tf), lambda i, j: (l, k, 0, j + nf)),
                pl.BlockSpec((None, None, tf, d), lambda i, j: (l, k, j, 0))]
    args = [x, mod_l, norm_g, final_g, w_in, w_in, w_out]
    if h_pre is not None:
        in_specs.append(pl.BlockSpec((tm, d), lambda i, j: (i, 0), **once))
        args.append(h_pre)
    out_specs = [pl.BlockSpec((tm, d), lambda i, j: (i, 0))]
    out_shape = [jax.ShapeDtypeStruct((m, d), F32)]
    if side is not None:
        cond8, ada_w, ada_b3, ls, tile0, nts = side
        assert (m // tm) * nf >= nts
        tile = lambda i, j: jnp.minimum(i * nf + j, nts - 1)
        in_specs += [pl.BlockSpec((8, d), lambda i, j: (0, 0)),
                     pl.BlockSpec((None, d, ADA_TS), lambda i, j: (ls, 0, tile0 + tile(i, j))),
                     pl.BlockSpec((None, 1, ADA_TS), lambda i, j: (ls, 0, tile0 + tile(i, j)))]
        args += [cond8, ada_w, ada_b3]
        out_specs.append(pl.BlockSpec((8, ADA_TS), lambda i, j: (0, tile(i, j))))
        out_shape.append(jax.ShapeDtypeStruct((8, nts * ADA_TS), F32))
    res = pl.pallas_call(
        functools.partial(_ffn_kernel, mi=mi, nf=nf, final=final, has_h=h_pre is not None,
                          has_side=side is not None),
        grid=(m // tm, nf), in_specs=in_specs, out_specs=out_specs, out_shape=out_shape,
        scratch_shapes=[] if h_pre is not None else [pltpu.VMEM((tm, d), BF16)],
        compiler_params=_cparams("arbitrary", "arbitrary"), name="ffn",
    )(*args)
    return res if side is not None else res[0]


PROJ_TM = 1024
PROJ_TN = 1024


def _norm_proj_kernel(x_ref, mod_ref, g_ref, w_ref, *rest, mi, nmain, w_rows_out):
    if len(rest) == 4:
        wx_ref, o_ref, ox_ref, h_sc = rest
    else:
        (o_ref, h_sc), wx_ref, ox_ref = rest, None, None
    j = pl.program_id(1)
    mm = _dot_nt if w_rows_out else _dot

    @pl.when(j == 0)
    def _():
        _norm_rows(x_ref, mod_ref, g_ref, h_sc, mi)

    @pl.when(j < nmain)
    def _():
        o_ref[...] = mm(h_sc[...], w_ref[...].astype(BF16))

    if wx_ref is not None:
        @pl.when(j == nmain)
        def _():
            ox_ref[...] = mm(h_sc[...], wx_ref[...].astype(BF16))


def _norm_proj(x, mod_l, norm_g, w, e, n_extra, *, mi, row_base, rows_per_cond, w_rows_out):
    m, d = x.shape
    tm = min(PROJ_TM, rows_per_cond)
    tn = PROJ_TN
    nout = w.shape[1] if w_rows_out else w.shape[2]
    nmain = nout // tn
    assert nout == nmain * tn + n_extra and (n_extra == 0 or (w_rows_out and (nmain * tn) % n_extra == 0))
    nj = nmain + (1 if n_extra else 0)
    jm = lambda j: jnp.minimum(j, nmain - 1)
    wblk = (lambda rows, idx: pl.BlockSpec((None, rows, d), lambda i, j: (e, idx(j), 0))) if w_rows_out else \
           (lambda cols, idx: pl.BlockSpec((None, d, cols), lambda i, j: (e, 0, idx(j))))
    in_specs = [pl.BlockSpec((tm, d), lambda i, j: (i, 0)),
                pl.BlockSpec((None, N_MOD, d), lambda i, j: (row_base + (i * tm) // rows_per_cond, 0, 0)),
                pl.BlockSpec((1, d), lambda i, j: (0, 0)),
                wblk(tn, jm)]
    args = [x, mod_l, norm_g, w]
    out_shape = [jax.ShapeDtypeStruct((m, nmain * tn), F32)]
    out_specs = [pl.BlockSpec((tm, tn), lambda i, j: (i, jm(j)))]
    if n_extra:
        in_specs.append(wblk(n_extra, lambda j: (nmain * tn) // n_extra))
        args.append(w)
        out_shape.append(jax.ShapeDtypeStruct((m, n_extra), F32))
        out_specs.append(pl.BlockSpec((tm, n_extra), lambda i, j: (i, 0)))
    res = pl.pallas_call(
        functools.partial(_norm_proj_kernel, mi=mi, nmain=nmain, w_rows_out=w_rows_out), grid=(m // tm, nj),
        in_specs=in_specs, out_specs=out_specs, out_shape=out_shape,
        scratch_shapes=[pltpu.VMEM((tm, d), BF16)],
        compiler_params=_cparams("arbitrary", "arbitrary"), name="norm_proj",
    )(*args)
    return res if n_extra else (res[0], None)


S5_S = 8
S5_GP = 2
S5_PPG = 4
S5_UNROLL = 4


def _s5_kernel(u_ref, pc_ref, w1_ref, w2t_ref, w3_ref, a_ref, at_ref, h0_ref, y_ref, *rest, t, nseg, segmented):
    if segmented:
        g_sc, v_sc, st_sc = rest
    else:
        stout_ref, g_sc, v_sc, st_sc = rest
    nblk = t // S5_S
    npair = w1_ref.shape[1]
    cw = w1_ref.shape[2]
    hw = cw // 2
    lt = u_ref.shape[1]

    def gather(k, c):
        r = pl.ds(pl.multiple_of(k * NPS, NPS), NPS)
        for j in range(S5_S):
            g_sc[r, j * lt:(j + 1) * lt] = u_ref[pl.ds(k * S5_S + j, NPS, stride=t), :]
        return c
    lax.fori_loop(0, nblk, gather, 0)
    u = _dot(g_sc[...].astype(BF16), pc_ref[...]).astype(BF16)

    for d in range(2):
        for q in range(npair):
            v_sc[d, :, q * cw:(q + 1) * cw] = _dot(u[:, q * cw:(q + 1) * cw], w1_ref[d, q])

        def advance(state, v):
            outs = []
            for q in range(npair):
                re = slice(q * cw, q * cw + hw)
                im = slice(q * cw + hw, (q + 1) * cw)
                ar, ai = a_ref[d, :, re], a_ref[d, :, im]
                hr, hi = state[:, re], state[:, im]
                outs.append(ar * hr - ai * hi + v[:, re])
                outs.append(ar * hi + ai * hr + v[:, im])
            return jnp.concatenate(outs, axis=1)

        def rows(kk):
            k = kk if d == 0 else nblk - 1 - kk
            return pl.ds(pl.multiple_of(k * NPS, NPS), NPS)

        if segmented:
            st_sc[...] = lax.fori_loop(0, nblk, lambda kk, s: advance(s, v_sc[d, rows(kk), :]),
                                       jnp.zeros(st_sc.shape, F32), unroll=S5_UNROLL)
            nb = NPS // nseg
            for b in range(nb):
                for q in range(npair):
                    re = slice(q * cw, q * cw + hw)
                    im = slice(q * cw + hw, (q + 1) * cw)
                    pr, pi = h0_ref[d, b:b + 1, re], h0_ref[d, b:b + 1, im]
                    ar, ai = at_ref[d, :, re], at_ref[d, :, im]
                    for k in (range(nseg) if d == 0 else range(nseg - 1, -1, -1)):
                        j = b * nseg + k
                        er, ei = st_sc[j:j + 1, re], st_sc[j:j + 1, im]
                        st_sc[j:j + 1, re] = pr
                        st_sc[j:j + 1, im] = pi
                        pr, pi = ar * pr - ai * pi + er, ar * pi + ai * pr + ei
            init = st_sc[...]
        else:
            init = h0_ref[d]

        def body(kk, s):
            r = rows(kk)
            v = v_sc[d, r, :]
            v_sc[d, r, :] = s
            return advance(s, v)
        fin = lax.fori_loop(0, nblk, body, init, unroll=S5_UNROLL)
        if not segmented:
            stout_ref[d] = fin

    for q in range(npair):
        cols = slice(q * cw, (q + 1) * cw)
        hcat = jnp.concatenate([v_sc[0, :, cols], v_sc[1, :, cols]], axis=1).astype(BF16)
        g_sc[:, cols] = _dot_nt(hcat, w2t_ref[q]) + _dot(u[:, cols], w3_ref[q])

    y = g_sc[...]
    hi = y.astype(BF16)
    lo = (y - hi.astype(F32)).astype(BF16)
    g_sc[...] = _dot_nt(hi, pc_ref[...]) + _dot_nt(lo, pc_ref[...])

    def scatter(k, c):
        r = pl.ds(pl.multiple_of(k * NPS, NPS), NPS)
        for j in range(S5_S):
            y_ref[pl.ds(k * S5_S + j, NPS, stride=t), :] = g_sc[r, j * lt:(j + 1) * lt]
        return c
    lax.fori_loop(0, nblk, scatter, 0)


def _s5_col_perm(lt, gw):
    r = jnp.arange(S5_S * lt)
    j, lane = r // lt, r % lt
    dst = (lane // gw) * (S5_S * gw) + j * gw + lane % gw
    return (dst[:, None] == jnp.arange(S5_S * lt)[None, :]).astype(BF16)


def _s5(proj, width, w1, w2t, w3, a_bc, a_t, h0, *, t, nseg):
    segmented = nseg > 1
    m = proj.shape[0]
    npair, cw = w1.shape[1], w1.shape[2]
    sl = S5_PPG * cw
    lt = sl // S5_S
    r = (t // S5_S) * NPS
    pc = _s5_col_perm(lt, cw // S5_S)
    col2 = lambda g: (0, g)
    col3 = lambda g: (0, 0, g)
    out_shape = [jax.ShapeDtypeStruct((m, width), F32)]
    out_specs = [pl.BlockSpec((m, lt), col2)]
    if not segmented:
        out_shape.append(jax.ShapeDtypeStruct((2, NPS, npair * cw), F32))
        out_specs.append(pl.BlockSpec((2, NPS, sl), col3))
    res = pl.pallas_call(
        functools.partial(_s5_kernel, t=t, nseg=nseg, segmented=segmented),
        grid=(npair // S5_PPG,),
        in_specs=[pl.BlockSpec((m, lt), col2),
                  pl.BlockSpec(pc.shape, lambda g: (0, 0)),
                  pl.BlockSpec((2, S5_PPG, cw, cw), lambda g: (0, g, 0, 0)),
                  pl.BlockSpec((S5_PPG, cw, 2 * cw), lambda g: (g, 0, 0)),
                  pl.BlockSpec((S5_PPG, cw, cw), lambda g: (g, 0, 0)),
                  pl.BlockSpec((2, NPS, sl), col3),
                  pl.BlockSpec((2, 1, sl), col3),
                  pl.BlockSpec((2, h0.shape[1], sl), col3)],
        out_specs=out_specs, out_shape=out_shape,
        scratch_shapes=[pltpu.VMEM((r, sl), F32), pltpu.VMEM((2, r, sl), F32), pltpu.VMEM((NPS, sl), F32)],
        compiler_params=_cparams("arbitrary"), name="s5",
    )(proj, pc, w1, w2t, w3, a_bc, a_t, h0)
    return (res[0], res[1]) if not segmented else (res[0], None)


def _s5_params(lam_re, lam_im, log_step, b_re, b_im, c_re, c_im, t_segs):
    nd, g, p = lam_re.shape
    n = b_re.shape[-1]
    npair = g // S5_GP
    hi = lax.Precision.HIGHEST
    dt = jnp.exp(log_step)[..., None]
    z_re, z_im = lam_re * dt, lam_im * dt
    mag = jnp.exp(z_re)
    ab_re, ab_im = mag * jnp.cos(z_im), mag * jnp.sin(z_im)
    den = lam_re * lam_re + lam_im * lam_im
    n_re = ab_re - 1.0
    f_re = (n_re * lam_re + ab_im * lam_im) / den
    f_im = (ab_im * lam_re - n_re * lam_im) / den
    bb_re = f_re[..., None] * b_re - f_im[..., None] * b_im
    bb_im = f_re[..., None] * b_im + f_im[..., None] * b_re

    cw = 2 * S5_GP * p
    col = jnp.arange(cw)
    is_re = col < cw // 2

    def dup(x):
        return jnp.concatenate([x, x], axis=-1)

    zr = dup(z_re.reshape(nd, npair, S5_GP * p))
    zi = dup(z_im.reshape(nd, npair, S5_GP * p))

    def power(e):
        m = jnp.exp(zr[:, :, None, :] * e[:, None, :, None])
        ang = zi[:, :, None, :] * e[:, None, :, None]
        return m * jnp.cos(ang), m * jnp.sin(ang)

    steps = jnp.arange(S5_S, dtype=F32)
    p1r, p1i = power(jnp.stack([S5_S - 1 - steps, steps]))
    p2r, p2i = power(jnp.stack([steps + 1, S5_S - steps]))
    br, bi = [dup(x.reshape(nd, npair, S5_GP, p, n).transpose(0, 1, 4, 2, 3).reshape(nd, npair, n, S5_GP * p))
              for x in (bb_re, bb_im)]
    cr, ci = [dup(x.reshape(nd, npair, S5_GP, n, p).transpose(0, 1, 3, 2, 4).reshape(nd, npair, n, S5_GP * p))
              for x in (c_re, c_im)]
    powers = jnp.stack([p1r, p1i, p2r, p2i], axis=2)
    bc = jnp.stack([br, bi, cr, ci], axis=2)

    mag = jnp.exp(z_re[..., None] * steps)
    lr, li = mag * jnp.cos(z_im[..., None] * steps), mag * jnp.sin(z_im[..., None] * steps)
    abt_re = lr[:, :, :, None, :] * bb_re[..., None] - li[:, :, :, None, :] * bb_im[..., None]
    abt_im = lr[:, :, :, None, :] * bb_im[..., None] + li[:, :, :, None, :] * bb_re[..., None]
    kt = (jnp.einsum('dgnp,dgpmt->dgtmn', c_re, abt_re, precision=hi)
          - jnp.einsum('dgnp,dgpmt->dgtmn', c_im, abt_im, precision=hi))
    kx = kt.reshape(nd, npair, S5_GP, S5_S, n, n).transpose(0, 3, 1, 2, 4, 5).reshape(nd, S5_S, npair, S5_GP * n, n)

    def state_cols(e):
        er, ei = power(jnp.full((nd, 1), e, F32))
        return jnp.where(is_re, er, ei).reshape(nd, npair * cw)

    a_s = state_cols(float(S5_S))
    a_bc = jnp.broadcast_to(a_s[:, None, :], (nd, NPS, a_s.shape[-1]))
    a_ts = [state_cols(float(ts))[:, None, :] for ts in t_segs]
    w1, w2t, w3 = _s5_maps(powers, bc, kx)
    return w1, w2t, w3, a_bc, a_ts


def _s5_maps_kernel(p_ref, bc_ref, kx_ref, w1_ref, w2t_ref, w3_ref):
    nd, ppg = p_ref.shape[0], p_ref.shape[1]
    n, cw = bc_ref.shape[3], bc_ref.shape[4]
    sw = cw // S5_S
    col = lax.broadcasted_iota(jnp.int32, (n, cw), 1)
    is_re = col < cw // 2
    sgrp = (col // (cw // (2 * S5_GP))) % S5_GP
    ocol = lax.broadcasted_iota(jnp.int32, (sw, cw), 1)
    jcol = ocol // sw
    same_grp = (ocol // n) % S5_GP == lax.broadcasted_iota(jnp.int32, (sw, cw), 0) // n
    tile = jnp.where(lax.broadcasted_iota(jnp.int32, (n, cw), 1) % n == lax.broadcasted_iota(jnp.int32, (n, cw), 0),
                     1.0, 0.0).astype(BF16)
    for q in range(ppg):
        for d in range(nd):
            br, bi, cr, ci = [bc_ref[d, q, k] for k in range(4)]
            for j in range(S5_S):
                p1r, p1i, p2r, p2i = [p_ref[d, q, k, j:j + 1, :] for k in range(4)]
                m1 = jnp.where(is_re, p1r * br - p1i * bi, p1r * bi + p1i * br)
                m2 = jnp.where(is_re, p2r * cr - p2i * ci, -(p2r * ci + p2i * cr))
                for g in range(S5_GP):
                    rows = slice((j * S5_GP + g) * n, (j * S5_GP + g + 1) * n)
                    w1_ref[d, q, rows, :] = jnp.where(sgrp == g, m1, 0.0).astype(BF16)
                    w2t_ref[q, rows, d * cw:(d + 1) * cw] = jnp.where(sgrp == g, m2, 0.0).astype(BF16)
        taps = [[jnp.where(same_grp, _dot(kx_ref[d, lag, q].astype(BF16), tile), 0.0) for lag in range(S5_S)]
                for d in range(nd)]
        for i in range(S5_S):
            acc = jnp.zeros((sw, cw), F32)
            for lag in range(S5_S - i):
                acc = acc + jnp.where(jcol == i + lag, taps[0][lag], 0.0)
            for lag in range(i + 1):
                acc = acc + jnp.where(jcol == i - lag, taps[1][lag], 0.0)
            w3_ref[q, i * sw:(i + 1) * sw, :] = acc.astype(BF16)


def _s5_maps(powers, bc, kx):
    nd, npair, _, _, cw = powers.shape
    n = bc.shape[3]
    rows = S5_S * S5_GP * n
    blk5 = lambda a: pl.BlockSpec((nd, S5_PPG) + a.shape[2:], lambda g: (0, g, 0, 0, 0))
    return pl.pallas_call(
        _s5_maps_kernel, grid=(npair // S5_PPG,),
        in_specs=[blk5(powers), blk5(bc),
                  pl.BlockSpec((nd, S5_S, S5_PPG) + kx.shape[3:], lambda g: (0, 0, g, 0, 0))],
        out_specs=[pl.BlockSpec((nd, S5_PPG, rows, cw), lambda g: (0, g, 0, 0)),
                   pl.BlockSpec((S5_PPG, rows, nd * cw), lambda g: (g, 0, 0)),
                   pl.BlockSpec((S5_PPG, rows, cw), lambda g: (g, 0, 0))],
        out_shape=[jax.ShapeDtypeStruct((nd, npair, rows, cw), BF16),
                   jax.ShapeDtypeStruct((npair, rows, nd * cw), BF16),
                   jax.ShapeDtypeStruct((npair, rows, cw), BF16)],
        compiler_params=_cparams("arbitrary"), name="s5_maps",
    )(powers, bc, kx)


def _s5_state_to_cols(s_re, s_im):
    b, nd, g, p = s_re.shape
    npair = g // S5_GP
    st = jnp.stack([s_re.reshape(b, nd, npair, S5_GP * p), s_im.reshape(b, nd, npair, S5_GP * p)], axis=3)
    return st.reshape(b, nd, -1).transpose(1, 0, 2)


def _s5_cols_to_state(st, g, p):
    nd, b, _ = st.shape
    npair = g // S5_GP
    st = st.reshape(nd, b, npair, 2, S5_GP, p).transpose(3, 1, 0, 2, 4, 5).reshape(2, b, nd, g, p)
    return st[0], st[1]


GLA_RB = 256


GLA_HPS = 4


def _gla_kernel(q_ref, k_ref, v_ref, g_ref, glr_ref, w2_ref, gb_ref, ng_ref, *rest, seq, zero_init, want_state):
    rest = list(rest)
    s0_ref = None if zero_init else rest.pop(0)
    o_ref = rest.pop(0)
    sfin_ref = rest.pop(0) if want_state else None
    dk = q_ref.shape[1] // GLA_HPS
    dv = v_ref.shape[1] // GLA_HPS
    for hh in range(GLA_HPS):
        lk, lv, l2 = pl.ds(hh * dk, dk), pl.ds(hh * dv, dv), pl.ds(hh * 2 * dk, 2 * dk)
        args = [q_ref.at[:, lk], k_ref.at[:, lk], v_ref.at[:, lv], g_ref.at[:, lv], glr_ref,
                w2_ref.at[:, l2], gb_ref.at[:, l2], ng_ref]
        if not zero_init:
            args.append(s0_ref.at[:, hh])
        args.append(o_ref.at[:, lv])
        if want_state:
            args.append(sfin_ref.at[:, hh])
        _gla_head(*args, *[sc.at[hh] for sc in rest], seq=seq, zero_init=zero_init, want_state=want_state)


def _gla_head(q_ref, k_ref, v_ref, g_ref, glr_ref, w2_ref, gb_ref, ng_ref, *rest, seq, zero_init, want_state):
    rest = list(rest)
    s0_ref = None if zero_init else rest.pop(0)
    o_ref = rest.pop(0)
    sfin_ref = rest.pop(0) if want_state else None
    qt_sc, ke_sc, dec_sc, osum_sc = rest
    dk = q_ref.shape[1]
    dv = v_ref.shape[1]
    n = seq // GLA_CHUNK
    scale = dk ** -0.5
    ri = lax.broadcasted_iota(jnp.int32, (GLA_RB, GLA_RB), 0)
    ci = lax.broadcasted_iota(jnp.int32, (GLA_RB, GLA_RB), 1)
    same = (ri // GLA_CHUNK) == (ci // GLA_CHUNK)
    lower = same & (ci <= ri)
    upper = same & (ci >= ri)
    ones_blk = jnp.where(same, 1.0, 0.0).astype(BF16)
    tri = jnp.where(lower, 1.0, 0.0).astype(BF16)

    for rb in range(seq // GLA_RB):
        rows = slice(rb * GLA_RB, (rb + 1) * GLA_RB)
        x = _dot(glr_ref[rows, :].astype(BF16), w2_ref[...]) + gb_ref[...]
        la = (jnp.minimum(x, 0.0) - jnp.log1p(jnp.exp(-jnp.abs(x)))) * (1.0 / GLA_TAU)
        la_hi = la.astype(BF16)
        la_lo = (la - la_hi.astype(F32)).astype(BF16)
        pre = _dot(tri, la_hi) + _dot(tri, la_lo)
        tot = _dot(ones_blk, la_hi) + _dot(ones_blk, la_lo)
        cum = (pre[:, :dk], tot[:, dk:] - pre[:, dk:] + la[:, dk:])
        qf = q_ref[rows, :] * scale
        kf = k_ref[rows, :]
        att = None
        for d in range(2):
            td = tot[:, d * dk:(d + 1) * dk]
            q_t = (qf * jnp.exp(cum[d])).astype(BF16)
            k_t = (kf * jnp.exp(-cum[d])).astype(BF16)
            qt_sc[d, rows, :] = q_t
            ke_sc[d, rows, :] = (kf * jnp.exp(td - cum[d])).astype(BF16)
            dec_sc[d, rows, :] = jnp.exp(td)
            a_d = jnp.where(lower if d == 0 else upper, _dot_nt(q_t, k_t), 0.0)
            att = a_d if att is None else att + a_d
        osum_sc[rows, :] = _dot(att.astype(BF16), v_ref[rows, :].astype(BF16))

    st = [jnp.zeros((dv, dk), F32) if zero_init else s0_ref[d].T for d in range(2)]
    for c in range(n):
        for d in range(2):
            cc = c if d == 0 else n - 1 - c
            rows = slice(cc * GLA_CHUNK, (cc + 1) * GLA_CHUNK)
            osum_sc[rows, :] += _dot_nt(qt_sc[d, rows, :], st[d].astype(BF16))
            kv = _dot_tn(v_ref[rows, :].astype(BF16), ke_sc[d, rows, :])
            st[d] = st[d] * dec_sc[d, cc * GLA_CHUNK:cc * GLA_CHUNK + 1, :] + kv
    if want_state:
        for d in range(2):
            sfin_ref[d] = st[d].T

    for rb in range(seq // GLA_RB):
        rows = slice(rb * GLA_RB, (rb + 1) * GLA_RB)
        o = osum_sc[rows, :]
        o = o * lax.rsqrt(jnp.mean(o * o, axis=-1, keepdims=True) + EPS) * ng_ref[...]
        o_ref[rows, :] = (o * _silu(g_ref[rows, :])).astype(BF16)


def _gla(proj, q_col, glr, w2p, gate_b, norm_g, s0, *, nb, seq, heads, want_state):
    qk = w2p.shape[-1] // 2
    dk = qk // heads
    dv = norm_g.shape[1]
    vdim = dv * heads
    zero_init = s0 is None
    hp = GLA_HPS
    bk, bv = hp * dk, hp * dv
    assert heads % hp == 0 and q_col % bk == 0 and qk % bk == 0 and (q_col + 2 * qk) % bv == 0 and vdim % bv == 0
    in_specs = [pl.BlockSpec((seq, bk), lambda b, h: (b, q_col // bk + h)),
                pl.BlockSpec((seq, bk), lambda b, h: (b, (q_col + qk) // bk + h)),
                pl.BlockSpec((seq, bv), lambda b, h: (b, (q_col + 2 * qk) // bv + h)),
                pl.BlockSpec((seq, bv), lambda b, h: (b, (q_col + 2 * qk + vdim) // bv + h)),
                pl.BlockSpec((seq, glr.shape[1]), lambda b, h: (b, 0)),
                pl.BlockSpec((w2p.shape[0], 2 * bk), lambda b, h: (0, h)),
                pl.BlockSpec((1, 2 * bk), lambda b, h: (0, h)),
                pl.BlockSpec((1, dv), lambda b, h: (0, 0))]
    args = [proj, proj, proj, proj, glr, w2p, gate_b, norm_g]
    if not zero_init:
        in_specs.append(pl.BlockSpec((None, 2, hp, dk, dv), lambda b, h: (b, 0, h, 0, 0)))
        args.append(s0)
    out_shape = [jax.ShapeDtypeStruct((nb * seq, vdim), BF16)]
    out_specs = [pl.BlockSpec((seq, bv), lambda b, h: (b, h))]
    if want_state:
        out_shape.append(jax.ShapeDtypeStruct((nb, 2, heads, dk, dv), F32))
        out_specs.append(pl.BlockSpec((None, 2, hp, dk, dv), lambda b, h: (b, 0, h, 0, 0)))
    res = pl.pallas_call(
        functools.partial(_gla_kernel, seq=seq, zero_init=zero_init, want_state=want_state),
        grid=(nb, heads // hp), in_specs=in_specs, out_specs=out_specs, out_shape=out_shape,
        scratch_shapes=[pltpu.VMEM((hp, 2, seq, dk), BF16), pltpu.VMEM((hp, 2, seq, dk), BF16),
                        pltpu.VMEM((hp, 2, seq, dk), F32), pltpu.VMEM((hp, seq, dv), F32)],
        compiler_params=_cparams("arbitrary", "arbitrary"), name="gla",
    )(*args)
    return (res[0], res[1]) if want_state else (res[0], None)


def _mixer_residual(x_ref, mod_ref, ng_ref, y, out_ref, h_ref):
    xn = x_ref[...] + mod_ref[5:6, :] * y
    out_ref[...] = xn
    h_ref[...] = _rms_mod(xn, ng_ref[...], mod_ref[7:8, :], mod_ref[6:7, :]).astype(BF16)


def _even_out_kernel(y_ref, u_ref, o_ref, x_ref, mod_ref, sd_ref, gw_ref, gb_ref, wo_ref, ng_ref, out_ref, h_ref):
    sw = u_ref.shape[1]
    ys = _gelu(y_ref[...] + sd_ref[...] * u_ref[...])
    ys = ys * _sigmoid(_dot(ys.astype(BF16), gw_ref[...]) + gb_ref[...])
    y = _dot(ys.astype(BF16), wo_ref[:sw, :]) + _dot(o_ref[...], wo_ref[sw:, :])
    _mixer_residual(x_ref, mod_ref, ng_ref, y, out_ref, h_ref)


def _even_out(y_s5, proj, o_gla, x, mod_l, s5_d, glu_w, glu_b, w_out, norm_next, *, t, row_base, pseq_per_cond):
    m, d = x.shape
    sw = y_s5.shape[1]
    return pl.pallas_call(
        _even_out_kernel, grid=(NPS,),
        in_specs=[pl.BlockSpec((t, sw), lambda i: (i, 0)),
                  pl.BlockSpec((t, sw), lambda i: (i, 0)),
                  pl.BlockSpec((t, o_gla.shape[1]), lambda i: (i, 0)),
                  pl.BlockSpec((t, d), lambda i: (i, 0)),
                  pl.BlockSpec((None, N_MOD, d), lambda i: (row_base + i // pseq_per_cond, 0, 0)),
                  pl.BlockSpec((1, sw), lambda i: (0, 0)),
                  pl.BlockSpec((sw, sw), lambda i: (0, 0)),
                  pl.BlockSpec((1, sw), lambda i: (0, 0)),
                  pl.BlockSpec(w_out.shape, lambda i: (0, 0)),
                  pl.BlockSpec((1, d), lambda i: (0, 0))],
        out_specs=[pl.BlockSpec((t, d), lambda i: (i, 0)), pl.BlockSpec((t, d), lambda i: (i, 0))],
        out_shape=[jax.ShapeDtypeStruct((m, d), F32), jax.ShapeDtypeStruct((m, d), BF16)],
        compiler_params=_cparams("arbitrary"), name="even_out",
    )(y_s5, proj, o_gla, x, mod_l, s5_d, glu_w, glu_b, w_out, norm_next)


LRU_TB = 16
CONV_W = 4
CONV_LEFT = 2
LRU_UNROLL = 8


def _lru_kernel(x_ref, cw_ref, cb_ref, wa_ref, ba_ref, wx_ref, bx_ref, lam_ref, h0_ref, hs_ref, *rest,
                t, glen, nseg, segmented):
    if segmented:
        xp_sc, a_sc, b_sc, hs_sc, e_sc, p_sc, s_sc = rest
    else:
        stout_ref, xp_sc, a_sc, b_sc, hs_sc = rest
    r, w = x_ref.shape
    rb = LRU_TB * NPS
    pad = CONV_LEFT * NPS
    perm = _perm_matrix(NPS, LRU_TB)
    perm_back = _perm_matrix(LRU_TB, NPS)

    for g in range(t // glen + 1):
        xp_sc[g * (glen * NPS + pad):g * (glen * NPS + pad) + pad, :] = jnp.zeros((pad, w), F32)

    def xp_row(i):
        return pl.multiple_of(pad * (1 + (i * LRU_TB) // glen) + i * rb, NPS)

    nbt = t // LRU_TB

    def load_tm(i):
        xin = jnp.concatenate([x_ref[pl.ds(pl.multiple_of(b * t + i * LRU_TB, LRU_TB), LRU_TB), :]
                               for b in range(NPS)], axis=0)
        xp_sc[pl.ds(xp_row(i), rb), :] = _permute_rows_f32(perm, xin)

    c2 = [(-0.25 * LRU_C) * _softplus(-lam_ref[d]) for d in range(2)]

    def conv_gates(i):
        xc = jnp.zeros((rb, w), F32) + cb_ref[...]
        for kk in range(CONV_W):
            xs = xp_sc[pl.ds(pl.multiple_of(xp_row(i) + (kk - CONV_LEFT) * NPS, NPS), rb), :]
            xc = xc + xs * cw_ref[kk:kk + 1, :]
        rows = pl.ds(pl.multiple_of(i * rb, rb), rb)
        xb = xc.astype(BF16)
        for d in range(2):
            tr = jnp.tanh(_dot(xb, wa_ref[d]) + ba_ref[d])
            ti = jnp.tanh(_dot(xb, wx_ref[d]) + bx_ref[d])
            th = jnp.tanh(c2[d] + c2[d] * tr)
            rcp = 1.0 / (1.0 - th)
            a_sc[d, rows, :] = (1.0 + th) * rcp
            b_sc[d, rows, :] = rcp * jnp.sqrt(-th) * ((1.0 + ti) * xc)

    load_tm(0)
    load_tm(min(1, nbt - 1))

    def pre(i, c):
        conv_gates(i)
        load_tm(jnp.minimum(i + 2, nbt - 1))
        return c
    lax.fori_loop(0, nbt, pre, 0)

    def store_bm(i):
        hb = _permute_rows_f32(perm_back, hs_sc[pl.ds(pl.multiple_of(i * rb, rb), rb), :])
        for b in range(NPS):
            hs_ref[pl.ds(pl.multiple_of(b * t + i * LRU_TB, LRU_TB), LRU_TB), :] = hb[b * LRU_TB:(b + 1) * LRU_TB]

    for d in range(2):
        def trow(s):
            return pl.ds(pl.multiple_of((s if d == 0 else t - 1 - s) * NPS, NPS), NPS)

        if segmented:
            def sweep(s, carry):
                h, p = carry
                rows = trow(s)
                a = a_sc[d, rows, :]
                return a * h + b_sc[d, rows, :], a * p
            e, p = lax.fori_loop(0, t, sweep, (jnp.zeros((NPS, w), F32), jnp.ones((NPS, w), F32)),
                                 unroll=LRU_UNROLL)
            e_sc[...] = e
            p_sc[...] = p
            nb = NPS // nseg
            for b in range(nb):
                prev = h0_ref[d, b:b + 1, :]
                for k in (range(nseg) if d == 0 else range(nseg - 1, -1, -1)):
                    j = b * nseg + k
                    s_sc[j:j + 1, :] = prev
                    prev = p_sc[j:j + 1, :] * prev + e_sc[j:j + 1, :]
            h_init = s_sc[...]
        else:
            h_init = h0_ref[d]

        def scan(s, h):
            rows = trow(s)
            h = a_sc[d, rows, :] * h + b_sc[d, rows, :]
            if d == 0:
                hs_sc[rows, :] = h
            else:
                hs_sc[rows, :] += h
            return h
        if d == 0:
            h_fin = lax.fori_loop(0, t, scan, h_init, unroll=LRU_UNROLL)
        else:
            def sweep_block(k, h):
                store_bm(jnp.minimum(nbt - k, nbt - 1))
                for s in range(LRU_TB):
                    h = scan(k * LRU_TB + s, h)
                return h
            h_fin = lax.fori_loop(0, nbt, sweep_block, h_init)
            store_bm(0)
        if not segmented:
            stout_ref[d] = h_fin


def _lru(proj, x_col, width, conv_w, conv_b, wa, ba, wx, bx, lam, h0, *, t, glen, nseg):
    segmented = nseg > 1
    r = proj.shape[0]
    heads, blk = wa.shape[1], wa.shape[2]
    col = lambda hd: (0, hd)
    col3 = lambda hd: (0, 0, hd)
    out_shape = [jax.ShapeDtypeStruct((r, width), F32)]
    out_specs = [pl.BlockSpec((r, blk), col)]
    scratch = [pltpu.VMEM((r + (t // glen + 1) * CONV_LEFT * NPS, blk), F32),
               pltpu.VMEM((2, r, blk), F32), pltpu.VMEM((2, r, blk), F32), pltpu.VMEM((r, blk), F32)]
    if segmented:
        scratch += [pltpu.VMEM((NPS, blk), F32)] * 3
    else:
        out_shape.append(jax.ShapeDtypeStruct((2, NPS, width), F32))
        out_specs.append(pl.BlockSpec((2, NPS, blk), col3))
    res = pl.pallas_call(
        functools.partial(_lru_kernel, t=t, glen=glen, nseg=nseg, segmented=segmented),
        grid=(heads,),
        in_specs=[pl.BlockSpec((r, blk), lambda hd: (0, x_col // blk + hd)),
                  pl.BlockSpec((CONV_W, blk), col),
                  pl.BlockSpec((1, blk), col),
                  pl.BlockSpec((2, None, blk, blk), lambda hd: (0, hd, 0, 0)),
                  pl.BlockSpec((2, 1, blk), col3),
                  pl.BlockSpec((2, None, blk, blk), lambda hd: (0, hd, 0, 0)),
                  pl.BlockSpec((2, 1, blk), col3),
                  pl.BlockSpec((2, 1, blk), col3),
                  pl.BlockSpec((2, h0.shape[1], blk), col3)],
        out_specs=out_specs, out_shape=out_shape, scratch_shapes=scratch,
        compiler_params=_cparams("arbitrary"), name="lru",
    )(proj, conv_w, conv_b, wa, ba, wx, bx, lam, h0)
    return (res[0], res[1]) if not segmented else (res[0], None)


def _odd_out_kernel(hs_ref, gate_ref, x_ref, mod_ref, wo_ref, ng_ref, out_ref, h_ref):
    y = _dot((hs_ref[...] * _gelu(gate_ref[...])).astype(BF16), wo_ref[...])
    _mixer_residual(x_ref, mod_ref, ng_ref, y, out_ref, h_ref)


def _odd_out(hs, proj, x, mod_l, w_out, norm_next, *, t, row_base, pseq_per_cond):
    m, d = x.shape
    w = hs.shape[1]
    return pl.pallas_call(
        _odd_out_kernel, grid=(NPS,),
        in_specs=[pl.BlockSpec((t, w), lambda i: (i, 0)),
                  pl.BlockSpec((t, w), lambda i: (i, 0)),
                  pl.BlockSpec((t, d), lambda i: (i, 0)),
                  pl.BlockSpec((None, N_MOD, d), lambda i: (row_base + i // pseq_per_cond, 0, 0)),
                  pl.BlockSpec(w_out.shape, lambda i: (0, 0)),
                  pl.BlockSpec((1, d), lambda i: (0, 0))],
        out_specs=[pl.BlockSpec((t, d), lambda i: (i, 0)), pl.BlockSpec((t, d), lambda i: (i, 0))],
        out_shape=[jax.ShapeDtypeStruct((m, d), F32), jax.ShapeDtypeStruct((m, d), BF16)],
        compiler_params=_cparams("arbitrary"), name="odd_out",
    )(hs, proj, x, mod_l, w_out, norm_next)


def kernel(x_prompt, x_sample, state_s5_re, state_s5_im, state_gla, state_lru, c, c_ctx, norm_g, ada_w, ada_b, ffn_w_in, ffn_w_out, final_norm_g, ev_w_in, ev_w_out, s5_lam_re, s5_lam_im, s5_log_step, s5_b_re, s5_b_im, s5_c_re, s5_c_im, s5_d, s5_glu_w, s5_glu_b, gla_gate_w2, gla_gate_b, gla_norm_g, od_w_in, od_w_out, lru_conv_w, lru_conv_b, lru_wa, lru_ba, lru_wx, lru_bx, lru_lam):
    nbc, seq, d = x_prompt.shape
    nbl, dseq, _ = x_sample.shape
    depth = norm_g.shape[0]
    assert nbc == NPS and NPS % nbl == 0
    nseg = NPS // nbl
    tl = dseq // nseg
    grid_w = 64
    assert tl % grid_w == 0 and tl % LRU_TB == 0 and seq % LRU_TB == 0 and seq % GLA_RB == 0 and dseq % GLA_RB == 0
    sw = s5_d.shape[1]
    qk = gla_gate_w2.shape[-1]
    heads = state_gla.shape[3]
    rank = gla_gate_w2.shape[2]
    g5, p5 = s5_lam_re.shape[2], s5_lam_re.shape[3]
    main = ev_w_in.shape[2] - 2 * rank
    assert main % PROJ_TN == 0 and (g5 // S5_GP) % S5_PPG == 0

    passes = [dict(x=x_prompt.reshape(nbc * seq, d), base=0, rpc=nbc * seq, ppc=NPS, t=seq, nseg=1,
                   glen=seq, nb=nbc, seq=seq),
              dict(x=x_sample.reshape(nbl * dseq, d), base=1, rpc=dseq, ppc=nseg, t=tl, nseg=nseg,
                   glen=grid_w, nb=nbl, seq=dseq)]

    cond8 = jnp.concatenate([c_ctx[None, :], c, jnp.zeros((8 - 1 - nbl, d), F32)], axis=0)
    ada_b3 = ada_b.reshape(depth, 1, -1)
    nmc = ada_w.shape[2]
    head = 3 * d
    mod_head = _ada(cond8, ada_w, ada_b3, head)
    mod_l = jnp.concatenate([mod_head, jnp.zeros((8, nmc - head), F32)], axis=1).reshape(8, N_MOD, d)
    fg = final_norm_g.reshape(1, d)

    new_s5_re, new_s5_im, new_gla, new_lru = [], [], [], []
    for l in range(depth):
        ng = lambda s: norm_g[l, s].reshape(1, d)
        mod_rest = None
        for pi, ps in enumerate(passes):
            side = None
            if l == 0 and pi == 0:
                side = (cond8, ada_w, ada_b3, 0, head // ADA_TS, (nmc - head) // ADA_TS)
            res = _ffn(ps['x'], mod_l, ng(0), fg, ffn_w_in, ffn_w_out, l, 0, mi=0,
                       row_base=ps['base'], rows_per_cond=ps['rpc'], final=False, side=side)
            if side is not None:
                res, mod_rest = res
            ps['x'] = res
        if mod_rest is not None:
            mod_l = jnp.concatenate([mod_head, mod_rest], axis=1).reshape(8, N_MOD, d)
        if l % 2 == 0:
            e = l // 2
            ev_w_in_t = jnp.swapaxes(ev_w_in, 1, 2)
            w_o = ev_w_out[e].astype(BF16)
            glu_w = s5_glu_w[e].astype(BF16)
            dkh = qk // heads
            w2p = jnp.stack([jnp.pad(gla_gate_w2[e, dd], ((dd * rank, (1 - dd) * rank), (0, 0))) for dd in range(2)])
            w2p = w2p.reshape(2, 2 * rank, heads, dkh).transpose(1, 2, 0, 3).reshape(2 * rank, 2 * qk).astype(BF16)
            gate_b = gla_gate_b[e].reshape(2, heads, dkh).transpose(1, 0, 2).reshape(1, 2 * qk)
            w1, w2t, w3, a_bc, a_ts = _s5_params(s5_lam_re[e], s5_lam_im[e], s5_log_step[e], s5_b_re[e], s5_b_im[e],
                                                 s5_c_re[e], s5_c_im[e], [ps['t'] for ps in passes])
            for pi, ps in enumerate(passes):
                t = ps['t']
                proj, glr = _norm_proj(ps['x'], mod_l, ng(1), ev_w_in_t, e, 2 * rank, mi=3,
                                       row_base=ps['base'], rows_per_cond=ps['rpc'], w_rows_out=True)
                if pi == 0:
                    h0 = jnp.zeros((2, NPS, a_bc.shape[-1]), F32)
                    s0 = None
                else:
                    h0 = _s5_state_to_cols(state_s5_re[:, e], state_s5_im[:, e])
                    s0 = state_gla[:, e]
                y_s5, s5_fin = _s5(proj, sw, w1, w2t, w3, a_bc, a_ts[pi], h0, t=t, nseg=ps['nseg'])
                o_gla, gla_fin = _gla(proj, sw, glr, w2p, gate_b, gla_norm_g[e].reshape(1, -1), s0,
                                      nb=ps['nb'], seq=ps['seq'], heads=heads, want_state=(pi == 0))
                if pi == 0:
                    sr, si = _s5_cols_to_state(s5_fin, g5, p5)
                    new_s5_re.append(sr)
                    new_s5_im.append(si)
                    new_gla.append(gla_fin)
                ps['x'], ps['h'] = _even_out(y_s5, proj, o_gla, ps['x'], mod_l, s5_d[e].reshape(1, sw), glu_w,
                                             s5_glu_b[e].reshape(1, sw), w_o, ng(2), t=t, row_base=ps['base'],
                                             pseq_per_cond=ps['ppc'])
        else:
            o = l // 2
            lw = od_w_in.shape[2] // 2
            w_o = od_w_out[o].astype(BF16)
            wa16 = (0.5 * lru_wa[o]).astype(BF16)
            wx16 = (0.5 * lru_wx[o]).astype(BF16)
            for pi, ps in enumerate(passes):
                t = ps['t']
                proj, _ = _norm_proj(ps['x'], mod_l, ng(1), od_w_in, o, 0, mi=3,
                                     row_base=ps['base'], rows_per_cond=ps['rpc'], w_rows_out=False)
                if pi == 0:
                    h0 = jnp.zeros((2, NPS, lw), F32)
                else:
                    h0 = state_lru[:, o].transpose(1, 0, 2)
                hs, lru_fin = _lru(proj, lw, lw, lru_conv_w[o], lru_conv_b[o].reshape(1, lw), wa16,
                                   0.5 * lru_ba[o].reshape(2, 1, lw), wx16, 0.5 * lru_bx[o].reshape(2, 1, lw),
                                   lru_lam[o].reshape(2, 1, lw), h0, t=t, glen=ps['glen'], nseg=ps['nseg'])
                if pi == 0:
                    new_lru.append(lru_fin.transpose(1, 0, 2))
                ps['x'], ps['h'] = _odd_out(hs, proj, ps['x'], mod_l, w_o, ng(2), t=t, row_base=ps['base'],
                                            pseq_per_cond=ps['ppc'])
        mod_next = None
        for pi, ps in enumerate(passes):
            side = (cond8, ada_w, ada_b3, l + 1, 0, nmc // ADA_TS) if (pi == 0 and l + 1 < depth) else None
            res = _ffn(ps['x'], mod_l, ng(2), fg, ffn_w_in, ffn_w_out, l, 1, mi=6, row_base=ps['base'],
                       rows_per_cond=ps['rpc'], final=(l == depth - 1), h_pre=ps['h'], side=side)
            if side is not None:
                res, mod_next = res[0], res[1].reshape(8, N_MOD, d)
            ps['x'] = res
        mod_l = mod_next

    y_prompt = passes[0]['x'].reshape(nbc, seq, d)
    y_sample = passes[1]['x'].reshape(nbl, dseq, d)
    return (y_prompt, y_sample, jnp.stack(new_s5_re, 1), jnp.stack(new_s5_im, 1),
            jnp.stack(new_gla, 1), jnp.stack(new_lru, 1))
```

```python
import functools
import math

import jax
import jax.numpy as jnp
from jax import lax
from jax.experimental import pallas as pl
from jax.experimental.pallas import tpu as pltpu

F32 = jnp.float32
BF16 = jnp.bfloat16
EPS = 1e-6
NPS = 16
N_MOD = 9
GLA_CHUNK = 64
GLA_TAU = 16.0
LRU_C = 8.0
VMEM_LIMIT = 58 * 1024 * 1024
ROW_CHUNK = 64
NORM_CHUNK = 64
NORM_UNROLL = 2


def _cparams(*sem):
    return pltpu.CompilerParams(dimension_semantics=sem, vmem_limit_bytes=VMEM_LIMIT)


def _dot(a, b):
    return jnp.dot(a, b, preferred_element_type=F32)


def _dot_nt(a, b):
    return lax.dot_general(a, b, (((1,), (1,)), ((), ())), preferred_element_type=F32)


def _dot_tn(a, b):
    return lax.dot_general(a, b, (((0,), (0,)), ((), ())), preferred_element_type=F32)


def _sigmoid(x):
    return 0.5 * (1.0 + jnp.tanh(0.5 * x))


def _silu(x):
    return x * _sigmoid(x)


def _gelu(x):
    return 0.5 * x * (1.0 + jnp.tanh(math.sqrt(2.0 / math.pi) * (x + 0.044715 * (x * x * x))))


def _softplus(x):
    return jnp.maximum(x, 0.0) + jnp.log1p(jnp.exp(-jnp.abs(x)))


def _rms_mod(x, g, scale, shift):
    return x * lax.rsqrt(jnp.mean(x * x, axis=-1, keepdims=True) + EPS) * (g * (1.0 + scale)) + shift


def _perm_matrix(n_outer, n_inner):
    n = n_outer * n_inner
    ro = lax.broadcasted_iota(jnp.int32, (n, n), 0)
    ci = lax.broadcasted_iota(jnp.int32, (n, n), 1)
    return jnp.where(ci == (ro % n_outer) * n_inner + ro // n_outer, 1.0, 0.0).astype(BF16)


def _permute_rows_f32(perm, x):
    hi = x.astype(BF16)
    r1 = x - hi.astype(F32)
    mid = r1.astype(BF16)
    lo = (r1 - mid.astype(F32)).astype(BF16)
    return _dot(perm, hi) + _dot(perm, mid) + _dot(perm, lo)


def _norm_rows(x_ref, mod_ref, g_ref, h_sc, mi):
    def body(r, c):
        rows = pl.ds(pl.multiple_of(r * NORM_CHUNK, NORM_CHUNK), NORM_CHUNK)
        h = _rms_mod(x_ref[rows, :], g_ref[...], mod_ref[mi + 1:mi + 2, :], mod_ref[mi:mi + 1, :])
        h_sc[rows, :] = h.astype(BF16)
        return c
    lax.fori_loop(0, x_ref.shape[0] // NORM_CHUNK, body, 0, unroll=NORM_UNROLL)


def _ada_kernel(c_ref, w_ref, b_ref, o_ref):
    ca = _silu(c_ref[...])
    o_ref[...] = _dot(ca.astype(BF16), w_ref[...].astype(BF16)) + b_ref[...]


def _ada(cond8, ada_w, ada_b3, ncols):
    _, d, n = ada_w.shape
    tn = 1024
    assert ncols % tn == 0
    return pl.pallas_call(
        _ada_kernel, grid=(ncols // tn,),
        in_specs=[pl.BlockSpec((8, d), lambda j: (0, 0)),
                  pl.BlockSpec((None, d, tn), lambda j: (0, 0, j)),
                  pl.BlockSpec((None, 1, tn), lambda j: (0, 0, j))],
        out_specs=pl.BlockSpec((8, tn), lambda j: (0, j)),
        out_shape=jax.ShapeDtypeStruct((8, ncols), F32),
        compiler_params=_cparams("arbitrary"), name="ada",
    )(cond8, ada_w, ada_b3)


FFN_TM = 1024
FFN_TF = 256
FFN_NC = 512


def _ffn_kernel(x_ref, mod_ref, g_ref, fg_ref, wa_ref, wb_ref, wo_ref, *rest, mi, nf, final, has_h, has_side):
    rest = list(rest)
    h_ref = rest.pop(0) if has_h else None
    side_in = [rest.pop(0) for _ in range(3)] if has_side else None
    o_ref = rest.pop(0)
    side_ref = rest.pop(0) if has_side else None
    if not has_h:
        h_ref = rest.pop(0)
    j = pl.program_id(1)
    tm, d = x_ref.shape

    @pl.when(j == 0)
    def _():
        if not has_h:
            _norm_rows(x_ref, mod_ref, g_ref, h_ref, mi)
        o_ref[...] = jnp.zeros_like(o_ref)

    if has_side:
        c_ref, aw_ref, ab_ref = side_in
        side_ref[...] = _dot(_silu(c_ref[...]).astype(BF16), aw_ref[...].astype(BF16)) + ab_ref[...]

    h = h_ref[...]
    a = _dot(h, wa_ref[...].astype(BF16))
    b = _dot(h, wb_ref[...].astype(BF16))
    act = (_silu(a) * b).astype(BF16)
    for n in range(d // FFN_NC):
        cols = slice(n * FFN_NC, (n + 1) * FFN_NC)
        o_ref[:, cols] += _dot(act, wo_ref[:, cols].astype(BF16))

    @pl.when(j == nf - 1)
    def _():
        def body(r, c):
            rows = pl.ds(pl.multiple_of(r * ROW_CHUNK, ROW_CHUNK), ROW_CHUNK)
            y = x_ref[rows, :] + 0.5 * mod_ref[mi + 2:mi + 3, :] * o_ref[rows, :]
            if final:
                y = y * lax.rsqrt(jnp.mean(y * y, axis=-1, keepdims=True) + EPS) * fg_ref[...]
            o_ref[rows, :] = y
            return c
        lax.fori_loop(0, tm // ROW_CHUNK, body, 0)


ADA_TS = 256


def _ffn(x, mod_l, norm_g, final_g, w_in, w_out, l, k, *, mi, row_base, rows_per_cond, final, h_pre=None, side=None):
    m, d = x.shape
    f = w_out.shape[2]
    tm = min(FFN_TM, rows_per_cond)
    tf = FFN_TF
    nf = f // tf
    row = lambda i, j: (row_base + (i * tm) // rows_per_cond, 0, 0)
    once = dict(pipeline_mode=pl.Buffered(1))
    in_specs = [pl.BlockSpec((tm, d), lambda i, j: (i, 0), **once),
                pl.BlockSpec((None, N_MOD, d), row),
                pl.BlockSpec((1, d), lambda i, j: (0, 0)),
                pl.BlockSpec((1, d), lambda i, j: (0, 0)),
                pl.BlockSpec((None, None, d, tf), lambda i, j: (l, k, 0, j)),
                pl.BlockSpec((None, None, d, tf), lambda i, j: (l, k, 0, j + nf)),
                pl.BlockSpec((None, None, tf, d), lambda i, j: (l, k, j, 0))]
    args = [x, mod_l, norm_g, final_g, w_in, w_in, w_out]
    if h_pre is not None:
        in_specs.append(pl.BlockSpec((tm, d), lambda i, j: (i, 0), **once))
        args.append(h_pre)
    out_specs = [pl.BlockSpec((tm, d), lambda i, j: (i, 0))]
    out_shape = [jax.ShapeDtypeStruct((m, d), F32)]
    if side is not None:
        cond8, ada_w, ada_b3, ls, tile0, nts = side
        assert (m // tm) * nf >= nts
        tile = lambda i, j: jnp.minimum(i * nf + j, nts - 1)
        in_specs += [pl.BlockSpec((8, d), lambda i, j: (0, 0)),
                     pl.BlockSpec((None, d, ADA_TS), lambda i, j: (ls, 0, tile0 + tile(i, j))),
                     pl.BlockSpec((None, 1, ADA_TS), lambda i, j: (ls, 0, tile0 + tile(i, j)))]
        args += [cond8, ada_w, ada_b3]
        out_specs.append(pl.BlockSpec((8, ADA_TS), lambda i, j: (0, tile(i, j))))
        out_shape.append(jax.ShapeDtypeStruct((8, nts * ADA_TS), F32))
    res = pl.pallas_call(
        functools.partial(_ffn_kernel, mi=mi, nf=nf, final=final, has_h=h_pre is not None,
                          has_side=side is not None),
        grid=(m // tm, nf), in_specs=in_specs, out_specs=out_specs, out_shape=out_shape,
        scratch_shapes=[] if h_pre is not None else [pltpu.VMEM((tm, d), BF16)],
        compiler_params=_cparams("arbitrary", "arbitrary"), name="ffn",
    )(*args)
    return res if side is not None else res[0]


PROJ_TM = 1024
PROJ_TN = 1024


def _norm_proj_kernel(x_ref, mod_ref, g_ref, w_ref, *rest, mi, nmain, w_rows_out):
    if len(rest) == 4:
        wx_ref, o_ref, ox_ref, h_sc = rest
    else:
        (o_ref, h_sc), wx_ref, ox_ref = rest, None, None
    j = pl.program_id(1)
    mm = _dot_nt if w_rows_out else _dot

    @pl.when(j == 0)
    def _():
        _norm_rows(x_ref, mod_ref, g_ref, h_sc, mi)

    @pl.when(j < nmain)
    def _():
        o_ref[...] = mm(h_sc[...], w_ref[...].astype(BF16))

    if wx_ref is not None:
        @pl.when(j == nmain)
        def _():
            ox_ref[...] = mm(h_sc[...], wx_ref[...].astype(BF16))


def _norm_proj(x, mod_l, norm_g, w, e, n_extra, *, mi, row_base, rows_per_cond, w_rows_out):
    m, d = x.shape
    tm = min(PROJ_TM, rows_per_cond)
    tn = PROJ_TN
    nout = w.shape[1] if w_rows_out else w.shape[2]
    nmain = nout // tn
    assert nout == nmain * tn + n_extra and (n_extra == 0 or (w_rows_out and (nmain * tn) % n_extra == 0))
    nj = nmain + (1 if n_extra else 0)
    jm = lambda j: jnp.minimum(j, nmain - 1)
    wblk = (lambda rows, idx: pl.BlockSpec((None, rows, d), lambda i, j: (e, idx(j), 0))) if w_rows_out else \
           (lambda cols, idx: pl.BlockSpec((None, d, cols), lambda i, j: (e, 0, idx(j))))
    in_specs = [pl.BlockSpec((tm, d), lambda i, j: (i, 0)),
                pl.BlockSpec((None, N_MOD, d), lambda i, j: (row_base + (i * tm) // rows_per_cond, 0, 0)),
                pl.BlockSpec((1, d), lambda i, j: (0, 0)),
                wblk(tn, jm)]
    args = [x, mod_l, norm_g, w]
    out_shape = [jax.ShapeDtypeStruct((m, nmain * tn), F32)]
    out_specs = [pl.BlockSpec((tm, tn), lambda i, j: (i, jm(j)))]
    if n_extra:
        in_specs.append(wblk(n_extra, lambda j: (nmain * tn) // n_extra))
        args.append(w)
        out_shape.append(jax.ShapeDtypeStruct((m, n_extra), F32))
        out_specs.append(pl.BlockSpec((tm, n_extra), lambda i, j: (i, 0)))
    res = pl.pallas_call(
        functools.partial(_norm_proj_kernel, mi=mi, nmain=nmain, w_rows_out=w_rows_out), grid=(m // tm, nj),
        in_specs=in_specs, out_specs=out_specs, out_shape=out_shape,
        scratch_shapes=[pltpu.VMEM((tm, d), BF16)],
        compiler_params=_cparams("arbitrary", "arbitrary"), name="norm_proj",
    )(*args)
    return res if n_extra else (res[0], None)


S5_S = 8
S5_GP = 2
S5_PPG = 4
S5_UNROLL = 4


def _s5_kernel(u_ref, pc_ref, w1_ref, w2t_ref, w3_ref, a_ref, at_ref, h0_ref, y_ref, *rest, t, nseg, segmented):
    if segmented:
        g_sc, v_sc, st_sc = rest
    else:
        stout_ref, g_sc, v_sc, st_sc = rest
    nblk = t // S5_S
    npair = w1_ref.shape[1]
    cw = w1_ref.shape[2]
    hw = cw // 2
    lt = u_ref.shape[1]

    def gather(k, c):
        r = pl.ds(pl.multiple_of(k * NPS, NPS), NPS)
        for j in range(S5_S):
            g_sc[r, j * lt:(j + 1) * lt] = u_ref[pl.ds(k * S5_S + j, NPS, stride=t), :]
        return c
    lax.fori_loop(0, nblk, gather, 0)
    u = _dot(g_sc[...].astype(BF16), pc_ref[...]).astype(BF16)

    for d in range(2):
        for q in range(npair):
            v_sc[d, :, q * cw:(q + 1) * cw] = _dot(u[:, q * cw:(q + 1) * cw], w1_ref[d, q])

        def advance(state, v):
            outs = []
            for q in range(npair):
                re = slice(q * cw, q * cw + hw)
                im = slice(q * cw + hw, (q + 1) * cw)
                ar, ai = a_ref[d, :, re], a_ref[d, :, im]
                hr, hi = state[:, re], state[:, im]
                outs.append(ar * hr - ai * hi + v[:, re])
                outs.append(ar * hi + ai * hr + v[:, im])
            return jnp.concatenate(outs, axis=1)

        def rows(kk):
            k = kk if d == 0 else nblk - 1 - kk
            return pl.ds(pl.multiple_of(k * NPS, NPS), NPS)

        if segmented:
            st_sc[...] = lax.fori_loop(0, nblk, lambda kk, s: advance(s, v_sc[d, rows(kk), :]),
                                       jnp.zeros(st_sc.shape, F32), unroll=S5_UNROLL)
            nb = NPS // nseg
            for b in range(nb):
                for q in range(npair):
                    re = slice(q * cw, q * cw + hw)
                    im = slice(q * cw + hw, (q + 1) * cw)
                    pr, pi = h0_ref[d, b:b + 1, re], h0_ref[d, b:b + 1, im]
                    ar, ai = at_ref[d, :, re], at_ref[d, :, im]
                    for k in (range(nseg) if d == 0 else range(nseg - 1, -1, -1)):
                        j = b * nseg + k
                        er, ei = st_sc[j:j + 1, re], st_sc[j:j + 1, im]
                        st_sc[j:j + 1, re] = pr
                        st_sc[j:j + 1, im] = pi
                        pr, pi = ar * pr - ai * pi + er, ar * pi + ai * pr + ei
            init = st_sc[...]
        else:
            init = h0_ref[d]

        def body(kk, s):
            r = rows(kk)
            v = v_sc[d, r, :]
            v_sc[d, r, :] = s
            return advance(s, v)
        fin = lax.fori_loop(0, nblk, body, init, unroll=S5_UNROLL)
        if not segmented:
            stout_ref[d] = fin

    for q in range(npair):
        cols = slice(q * cw, (q + 1) * cw)
        hcat = jnp.concatenate([v_sc[0, :, cols], v_sc[1, :, cols]], axis=1).astype(BF16)
        g_sc[:, cols] = _dot_nt(hcat, w2t_ref[q]) + _dot(u[:, cols], w3_ref[q])

    y = g_sc[...]
    hi = y.astype(BF16)
    lo = (y - hi.astype(F32)).astype(BF16)
    g_sc[...] = _dot_nt(hi, pc_ref[...]) + _dot_nt(lo, pc_ref[...])

    def scatter(k, c):
        r = pl.ds(pl.multiple_of(k * NPS, NPS), NPS)
        for j in range(S5_S):
            y_ref[pl.ds(k * S5_S + j, NPS, stride=t), :] = g_sc[r, j * lt:(j + 1) * lt]
        return c
    lax.fori_loop(0, nblk, scatter, 0)


def _s5_col_perm(lt, gw):
    r = jnp.arange(S5_S * lt)
    j, lane = r // lt, r % lt
    dst = (lane // gw) * (S5_S * gw) + j * gw + lane % gw
    return (dst[:, None] == jnp.arange(S5_S * lt)[None, :]).astype(BF16)


def _s5(proj, width, w1, w2t, w3, a_bc, a_t, h0, *, t, nseg):
    segmented = nseg > 1
    m = proj.shape[0]
    npair, cw = w1.shape[1], w1.shape[2]
    sl = S5_PPG * cw
    lt = sl // S5_S
    r = (t // S5_S) * NPS
    pc = _s5_col_perm(lt, cw // S5_S)
    col2 = lambda g: (0, g)
    col3 = lambda g: (0, 0, g)
    out_shape = [jax.ShapeDtypeStruct((m, width), F32)]
    out_specs = [pl.BlockSpec((m, lt), col2)]
    if not segmented:
        out_shape.append(jax.ShapeDtypeStruct((2, NPS, npair * cw), F32))
        out_specs.append(pl.BlockSpec((2, NPS, sl), col3))
    res = pl.pallas_call(
        functools.partial(_s5_kernel, t=t, nseg=nseg, segmented=segmented),
        grid=(npair // S5_PPG,),
        in_specs=[pl.BlockSpec((m, lt), col2),
                  pl.BlockSpec(pc.shape, lambda g: (0, 0)),
                  pl.BlockSpec((2, S5_PPG, cw, cw), lambda g: (0, g, 0, 0)),
                  pl.BlockSpec((S5_PPG, cw, 2 * cw), lambda g: (g, 0, 0)),
                  pl.BlockSpec((S5_PPG, cw, cw), lambda g: (g, 0, 0)),
                  pl.BlockSpec((2, NPS, sl), col3),
                  pl.BlockSpec((2, 1, sl), col3),
                  pl.BlockSpec((2, h0.shape[1], sl), col3)],
        out_specs=out_specs, out_shape=out_shape,
        scratch_shapes=[pltpu.VMEM((r, sl), F32), pltpu.VMEM((2, r, sl), F32), pltpu.VMEM((NPS, sl), F32)],
        compiler_params=_cparams("arbitrary"), name="s5",
    )(proj, pc, w1, w2t, w3, a_bc, a_t, h0)
    return (res[0], res[1]) if not segmented else (res[0], None)


def _s5_params(lam_re, lam_im, log_step, b_re, b_im, c_re, c_im, t_segs):
    nd, g, p = lam_re.shape
    n = b_re.shape[-1]
    npair = g // S5_GP
    hi = lax.Precision.HIGHEST
    dt = jnp.exp(log_step)[..., None]
    z_re, z_im = lam_re * dt, lam_im * dt
    mag = jnp.exp(z_re)
    ab_re, ab_im = mag * jnp.cos(z_im), mag * jnp.sin(z_im)
    den = lam_re * lam_re + lam_im * lam_im
    n_re = ab_re - 1.0
    f_re = (n_re * lam_re + ab_im * lam_im) / den
    f_im = (ab_im * lam_re - n_re * lam_im) / den
    bb_re = f_re[..., None] * b_re - f_im[..., None] * b_im
    bb_im = f_re[..., None] * b_im + f_im[..., None] * b_re

    cw = 2 * S5_GP * p
    col = jnp.arange(cw)
    is_re = col < cw // 2

    def dup(x):
        return jnp.concatenate([x, x], axis=-1)

    zr = dup(z_re.reshape(nd, npair, S5_GP * p))
    zi = dup(z_im.reshape(nd, npair, S5_GP * p))

    def power(e):
        m = jnp.exp(zr[:, :, None, :] * e[:, None, :, None])
        ang = zi[:, :, None, :] * e[:, None, :, None]
        return m * jnp.cos(ang), m * jnp.sin(ang)

    steps = jnp.arange(S5_S, dtype=F32)
    p1r, p1i = power(jnp.stack([S5_S - 1 - steps, steps]))
    p2r, p2i = power(jnp.stack([steps + 1, S5_S - steps]))
    br, bi = [dup(x.reshape(nd, npair, S5_GP, p, n).transpose(0, 1, 4, 2, 3).reshape(nd, npair, n, S5_GP * p))
              for x in (bb_re, bb_im)]
    cr, ci = [dup(x.reshape(nd, npair, S5_GP, n, p).transpose(0, 1, 3, 2, 4).reshape(nd, npair, n, S5_GP * p))
              for x in (c_re, c_im)]
    powers = jnp.stack([p1r, p1i, p2r, p2i], axis=2)
    bc = jnp.stack([br, bi, cr, ci], axis=2)

    mag = jnp.exp(z_re[..., None] * steps)
    lr, li = mag * jnp.cos(z_im[..., None] * steps), mag * jnp.sin(z_im[..., None] * steps)
    abt_re = lr[:, :, :, None, :] * bb_re[..., None] - li[:, :, :, None, :] * bb_im[..., None]
    abt_im = lr[:, :, :, None, :] * bb_im[..., None] + li[:, :, :, None, :] * bb_re[..., None]
    kt = (jnp.einsum('dgnp,dgpmt->dgtmn', c_re, abt_re, precision=hi)
          - jnp.einsum('dgnp,dgpmt->dgtmn', c_im, abt_im, precision=hi))
    kx = kt.reshape(nd, npair, S5_GP, S5_S, n, n).transpose(0, 3, 1, 2, 4, 5).reshape(nd, S5_S, npair, S5_GP * n, n)

    def state_cols(e):
        er, ei = power(jnp.full((nd, 1), e, F32))
        return jnp.where(is_re, er, ei).reshape(nd, npair * cw)

    a_s = state_cols(float(S5_S))
    a_bc = jnp.broadcast_to(a_s[:, None, :], (nd, NPS, a_s.shape[-1]))
    a_ts = [state_cols(float(ts))[:, None, :] for ts in t_segs]
    w1, w2t, w3 = _s5_maps(powers, bc, kx)
    return w1, w2t, w3, a_bc, a_ts


def _s5_maps_kernel(p_ref, bc_ref, kx_ref, w1_ref, w2t_ref, w3_ref):
    nd, ppg = p_ref.shape[0], p_ref.shape[1]
    n, cw = bc_ref.shape[3], bc_ref.shape[4]
    sw = cw // S5_S
    col = lax.broadcasted_iota(jnp.int32, (n, cw), 1)
    is_re = col < cw // 2
    sgrp = (col // (cw // (2 * S5_GP))) % S5_GP
    ocol = lax.broadcasted_iota(jnp.int32, (sw, cw), 1)
    jcol = ocol // sw
    same_grp = (ocol // n) % S5_GP == lax.broadcasted_iota(jnp.int32, (sw, cw), 0) // n
    tile = jnp.where(lax.broadcasted_iota(jnp.int32, (n, cw), 1) % n == lax.broadcasted_iota(jnp.int32, (n, cw), 0),
                     1.0, 0.0).astype(BF16)
    for q in range(ppg):
        for d in range(nd):
            br, bi, cr, ci = [bc_ref[d, q, k] for k in range(4)]
            for j in range(S5_S):
                p1r, p1i, p2r, p2i = [p_ref[d, q, k, j:j + 1, :] for k in range(4)]
                m1 = jnp.where(is_re, p1r * br - p1i * bi, p1r * bi + p1i * br)
                m2 = jnp.where(is_re, p2r * cr - p2i * ci, -(p2r * ci + p2i * cr))
                for g in range(S5_GP):
                    rows = slice((j * S5_GP + g) * n, (j * S5_GP + g + 1) * n)
                    w1_ref[d, q, rows, :] = jnp.where(sgrp == g, m1, 0.0).astype(BF16)
                    w2t_ref[q, rows, d * cw:(d + 1) * cw] = jnp.where(sgrp == g, m2, 0.0).astype(BF16)
        taps = [[jnp.where(same_grp, _dot(kx_ref[d, lag, q].astype(BF16), tile), 0.0) for lag in range(S5_S)]
                for d in range(nd)]
        for i in range(S5_S):
            acc = jnp.zeros((sw, cw), F32)
            for lag in range(S5_S - i):
                acc = acc + jnp.where(jcol == i + lag, taps[0][lag], 0.0)
            for lag in range(i + 1):
                acc = acc + jnp.where(jcol == i - lag, taps[1][lag], 0.0)
            w3_ref[q, i * sw:(i + 1) * sw, :] = acc.astype(BF16)


def _s5_maps(powers, bc, kx):
    nd, npair, _, _, cw = powers.shape
    n = bc.shape[3]
    rows = S5_S * S5_GP * n
    blk5 = lambda a: pl.BlockSpec((nd, S5_PPG) + a.shape[2:], lambda g: (0, g, 0, 0, 0))
    return pl.pallas_call(
        _s5_maps_kernel, grid=(npair // S5_PPG,),
        in_specs=[blk5(powers), blk5(bc),
                  pl.BlockSpec((nd, S5_S, S5_PPG) + kx.shape[3:], lambda g: (0, 0, g, 0, 0))],
        out_specs=[pl.BlockSpec((nd, S5_PPG, rows, cw), lambda g: (0, g, 0, 0)),
                   pl.BlockSpec((S5_PPG, rows, nd * cw), lambda g: (g, 0, 0)),
                   pl.BlockSpec((S5_PPG, rows, cw), lambda g: (g, 0, 0))],
        out_shape=[jax.ShapeDtypeStruct((nd, npair, rows, cw), BF16),
                   jax.ShapeDtypeStruct((npair, rows, nd * cw), BF16),
                   jax.ShapeDtypeStruct((npair, rows, cw), BF16)],
        compiler_params=_cparams("arbitrary"), name="s5_maps",
    )(powers, bc, kx)


def _s5_state_to_cols(s_re, s_im):
    b, nd, g, p = s_re.shape
    npair = g // S5_GP
    st = jnp.stack([s_re.reshape(b, nd, npair, S5_GP * p), s_im.reshape(b, nd, npair, S5_GP * p)], axis=3)
    return st.reshape(b, nd, -1).transpose(1, 0, 2)


def _s5_cols_to_state(st, g, p):
    nd, b, _ = st.shape
    npair = g // S5_GP
    st = st.reshape(nd, b, npair, 2, S5_GP, p).transpose(3, 1, 0, 2, 4, 5).reshape(2, b, nd, g, p)
    return st[0], st[1]


GLA_RB = 256


GLA_HPS = 4


def _gla_kernel(q_ref, k_ref, v_ref, g_ref, glr_ref, w2_ref, gb_ref, ng_ref, *rest, seq, zero_init, want_state):
    rest = list(rest)
    s0_ref = None if zero_init else rest.pop(0)
    o_ref = rest.pop(0)
    sfin_ref = rest.pop(0) if want_state else None
    dk = q_ref.shape[1] // GLA_HPS
    dv = v_ref.shape[1] // GLA_HPS
    for hh in range(GLA_HPS):
        lk, lv, l2 = pl.ds(hh * dk, dk), pl.ds(hh * dv, dv), pl.ds(hh * 2 * dk, 2 * dk)
        args = [q_ref.at[:, lk], k_ref.at[:, lk], v_ref.at[:, lv], g_ref.at[:, lv], glr_ref,
                w2_ref.at[:, l2], gb_ref.at[:, l2], ng_ref]
        if not zero_init:
            args.append(s0_ref.at[:, hh])
        args.append(o_ref.at[:, lv])
        if want_state:
            args.append(sfin_ref.at[:, hh])
        _gla_head(*args, *[sc.at[hh] for sc in rest], seq=seq, zero_init=zero_init, want_state=want_state)


def _gla_head(q_ref, k_ref, v_ref, g_ref, glr_ref, w2_ref, gb_ref, ng_ref, *rest, seq, zero_init, want_state):
    rest = list(rest)
    s0_ref = None if zero_init else rest.pop(0)
    o_ref = rest.pop(0)
    sfin_ref = rest.pop(0) if want_state else None
    qt_sc, ke_sc, dec_sc, osum_sc = rest
    dk = q_ref.shape[1]
    dv = v_ref.shape[1]
    n = seq // GLA_CHUNK
    scale = dk ** -0.5
    ri = lax.broadcasted_iota(jnp.int32, (GLA_RB, GLA_RB), 0)
    ci = lax.broadcasted_iota(jnp.int32, (GLA_RB, GLA_RB), 1)
    same = (ri // GLA_CHUNK) == (ci // GLA_CHUNK)
    lower = same & (ci <= ri)
    upper = same & (ci >= ri)
    ones_blk = jnp.where(same, 1.0, 0.0).astype(BF16)
    tri = jnp.where(lower, 1.0, 0.0).astype(BF16)

    for rb in range(seq // GLA_RB):
        rows = slice(rb * GLA_RB, (rb + 1) * GLA_RB)
        x = _dot(glr_ref[rows, :].astype(BF16), w2_ref[...]) + gb_ref[...]
        la = (jnp.minimum(x, 0.0) - jnp.log1p(jnp.exp(-jnp.abs(x)))) * (1.0 / GLA_TAU)
        la_hi = la.astype(BF16)
        la_lo = (la - la_hi.astype(F32)).astype(BF16)
        pre = _dot(tri, la_hi) + _dot(tri, la_lo)
        tot = _dot(ones_blk, la_hi) + _dot(ones_blk, la_lo)
        cum = (pre[:, :dk], tot[:, dk:] - pre[:, dk:] + la[:, dk:])
        qf = q_ref[rows, :] * scale
        kf = k_ref[rows, :]
        att = None
        for d in range(2):
            td = tot[:, d * dk:(d + 1) * dk]
            q_t = (qf * jnp.exp(cum[d])).astype(BF16)
            k_t = (kf * jnp.exp(-cum[d])).astype(BF16)
            qt_sc[d, rows, :] = q_t
            ke_sc[d, rows, :] = (kf * jnp.exp(td - cum[d])).astype(BF16)
            dec_sc[d, rows, :] = jnp.exp(td)
            a_d = jnp.where(lower if d == 0 else upper, _dot_nt(q_t, k_t), 0.0)
            att = a_d if att is None else att + a_d
        osum_sc[rows, :] = _dot(att.astype(BF16), v_ref[rows, :].astype(BF16))

    st = [jnp.zeros((dv, dk), F32) if zero_init else s0_ref[d].T for d in range(2)]
    for c in range(n):
        for d in range(2):
            cc = c if d == 0 else n - 1 - c
            rows = slice(cc * GLA_CHUNK, (cc + 1) * GLA_CHUNK)
            osum_sc[rows, :] += _dot_nt(qt_sc[d, rows, :], st[d].astype(BF16))
            kv = _dot_tn(v_ref[rows, :].astype(BF16), ke_sc[d, rows, :])
            st[d] = st[d] * dec_sc[d, cc * GLA_CHUNK:cc * GLA_CHUNK + 1, :] + kv
    if want_state:
        for d in range(2):
            sfin_ref[d] = st[d].T

    for rb in range(seq // GLA_RB):
        rows = slice(rb * GLA_RB, (rb + 1) * GLA_RB)
        o = osum_sc[rows, :]
        o = o * lax.rsqrt(jnp.mean(o * o, axis=-1, keepdims=True) + EPS) * ng_ref[...]
        o_ref[rows, :] = (o * _silu(g_ref[rows, :])).astype(BF16)


def _gla(proj, q_col, glr, w2p, gate_b, norm_g, s0, *, nb, seq, heads, want_state):
    qk = w2p.shape[-1] // 2
    dk = qk // heads
    dv = norm_g.shape[1]
    vdim = dv * heads
    zero_init = s0 is None
    hp = GLA_HPS
    bk, bv = hp * dk, hp * dv
    assert heads % hp == 0 and q_col % bk == 0 and qk % bk == 0 and (q_col + 2 * qk) % bv == 0 and vdim % bv == 0
    in_specs = [pl.BlockSpec((seq, bk), lambda b, h: (b, q_col // bk + h)),
                pl.BlockSpec((seq, bk), lambda b, h: (b, (q_col + qk) // bk + h)),
                pl.BlockSpec((seq, bv), lambda b, h: (b, (q_col + 2 * qk) // bv + h)),
                pl.BlockSpec((seq, bv), lambda b, h: (b, (q_col + 2 * qk + vdim) // bv + h)),
                pl.BlockSpec((seq, glr.shape[1]), lambda b, h: (b, 0)),
                pl.BlockSpec((w2p.shape[0], 2 * bk), lambda b, h: (0, h)),
                pl.BlockSpec((1, 2 * bk), lambda b, h: (0, h)),
                pl.BlockSpec((1, dv), lambda b, h: (0, 0))]
    args = [proj, proj, proj, proj, glr, w2p, gate_b, norm_g]
    if not zero_init:
        in_specs.append(pl.BlockSpec((None, 2, hp, dk, dv), lambda b, h: (b, 0, h, 0, 0)))
        args.append(s0)
    out_shape = [jax.ShapeDtypeStruct((nb * seq, vdim), BF16)]
    out_specs = [pl.BlockSpec((seq, bv), lambda b, h: (b, h))]
    if want_state:
        out_shape.append(jax.ShapeDtypeStruct((nb, 2, heads, dk, dv), F32))
        out_specs.append(pl.BlockSpec((None, 2, hp, dk, dv), lambda b, h: (b, 0, h, 0, 0)))
    res = pl.pallas_call(
        functools.partial(_gla_kernel, seq=seq, zero_init=zero_init, want_state=want_state),
        grid=(nb, heads // hp), in_specs=in_specs, out_specs=out_specs, out_shape=out_shape,
        scratch_shapes=[pltpu.VMEM((hp, 2, seq, dk), BF16), pltpu.VMEM((hp, 2, seq, dk), BF16),
                        pltpu.VMEM((hp, 2, seq, dk), F32), pltpu.VMEM((hp, seq, dv), F32)],
        compiler_params=_cparams("arbitrary", "arbitrary"), name="gla",
    )(*args)
    return (res[0], res[1]) if want_state else (res[0], None)


OUT_PPS = 2


def _mixer_residual(x_ref, mod_ref, ng_ref, y, out_ref, h_ref):
    xn = x_ref[...] + mod_ref[5:6, :] * y
    out_ref[...] = xn
    h_ref[...] = _rms_mod(xn, ng_ref[...], mod_ref[7:8, :], mod_ref[6:7, :]).astype(BF16)


def _even_out_kernel(y_ref, u_ref, o_ref, x_ref, mod_ref, sd_ref, gw_ref, gb_ref, wo_ref, ng_ref, out_ref, h_ref):
    sw = u_ref.shape[1]
    ys = _gelu(y_ref[...] + sd_ref[...] * u_ref[...])
    ys = ys * _sigmoid(_dot(ys.astype(BF16), gw_ref[...]) + gb_ref[...])
    y = _dot(ys.astype(BF16), wo_ref[:sw, :]) + _dot(o_ref[...], wo_ref[sw:, :])
    _mixer_residual(x_ref, mod_ref, ng_ref, y, out_ref, h_ref)


def _even_out(y_s5, proj, o_gla, x, mod_l, s5_d, glu_w, glu_b, w_out, norm_next, *, t, row_base, pseq_per_cond):
    m, d = x.shape
    sw = y_s5.shape[1]
    t = OUT_PPS * t
    assert pseq_per_cond % OUT_PPS == 0
    once = dict(pipeline_mode=pl.Buffered(1))
    return pl.pallas_call(
        _even_out_kernel, grid=(NPS // OUT_PPS,),
        in_specs=[pl.BlockSpec((t, sw), lambda i: (i, 0)),
                  pl.BlockSpec((t, sw), lambda i: (i, 0)),
                  pl.BlockSpec((t, o_gla.shape[1]), lambda i: (i, 0)),
                  pl.BlockSpec((t, d), lambda i: (i, 0)),
                  pl.BlockSpec((None, N_MOD, d), lambda i: (row_base + (i * OUT_PPS) // pseq_per_cond, 0, 0)),
                  pl.BlockSpec((1, sw), lambda i: (0, 0)),
                  pl.BlockSpec((sw, sw), lambda i: (0, 0), **once),
                  pl.BlockSpec((1, sw), lambda i: (0, 0)),
                  pl.BlockSpec(w_out.shape, lambda i: (0, 0), **once),
                  pl.BlockSpec((1, d), lambda i: (0, 0))],
        out_specs=[pl.BlockSpec((t, d), lambda i: (i, 0)), pl.BlockSpec((t, d), lambda i: (i, 0))],
        out_shape=[jax.ShapeDtypeStruct((m, d), F32), jax.ShapeDtypeStruct((m, d), BF16)],
        compiler_params=_cparams("arbitrary"), name="even_out",
    )(y_s5, proj, o_gla, x, mod_l, s5_d, glu_w, glu_b, w_out, norm_next)


LRU_TB = 16
CONV_W = 4
CONV_LEFT = 2
LRU_UNROLL = 8


def _lru_kernel(x_ref, cw_ref, cb_ref, wa_ref, ba_ref, wx_ref, bx_ref, lam_ref, h0_ref, hs_ref, *rest,
                t, glen, nseg, segmented):
    if segmented:
        xp_sc, a_sc, b_sc, hs_sc, e_sc, p_sc, s_sc = rest
    else:
        stout_ref, xp_sc, a_sc, b_sc, hs_sc = rest
    r, w = x_ref.shape
    rb = LRU_TB * NPS
    pad = CONV_LEFT * NPS
    perm = _perm_matrix(NPS, LRU_TB)
    perm_back = _perm_matrix(LRU_TB, NPS)

    for g in range(t // glen + 1):
        xp_sc[g * (glen * NPS + pad):g * (glen * NPS + pad) + pad, :] = jnp.zeros((pad, w), F32)

    def xp_row(i):
        return pl.multiple_of(pad * (1 + (i * LRU_TB) // glen) + i * rb, NPS)

    nbt = t // LRU_TB

    def load_tm(i):
        xin = jnp.concatenate([x_ref[pl.ds(pl.multiple_of(b * t + i * LRU_TB, LRU_TB), LRU_TB), :]
                               for b in range(NPS)], axis=0)
        xp_sc[pl.ds(xp_row(i), rb), :] = _permute_rows_f32(perm, xin)

    c2 = [(-0.25 * LRU_C) * _softplus(-lam_ref[d]) for d in range(2)]

    def conv_gates(i):
        xc = jnp.zeros((rb, w), F32) + cb_ref[...]
        for kk in range(CONV_W):
            xs = xp_sc[pl.ds(pl.multiple_of(xp_row(i) + (kk - CONV_LEFT) * NPS, NPS), rb), :]
            xc = xc + xs * cw_ref[kk:kk + 1, :]
        rows = pl.ds(pl.multiple_of(i * rb, rb), rb)
        xb = xc.astype(BF16)
        for d in range(2):
            tr = jnp.tanh(_dot(xb, wa_ref[d]) + ba_ref[d])
            ti = jnp.tanh(_dot(xb, wx_ref[d]) + bx_ref[d])
            th = jnp.tanh(c2[d] + c2[d] * tr)
            rcp = 1.0 / (1.0 - th)
            a_sc[d, rows, :] = (1.0 + th) * rcp
            b_sc[d, rows, :] = rcp * jnp.sqrt(-th) * ((1.0 + ti) * xc)

    load_tm(0)
    load_tm(min(1, nbt - 1))

    def pre(i, c):
        conv_gates(i)
        load_tm(jnp.minimum(i + 2, nbt - 1))
        return c
    lax.fori_loop(0, nbt, pre, 0)

    def store_bm(i):
        hb = _permute_rows_f32(perm_back, hs_sc[pl.ds(pl.multiple_of(i * rb, rb), rb), :])
        for b in range(NPS):
            hs_ref[pl.ds(pl.multiple_of(b * t + i * LRU_TB, LRU_TB), LRU_TB), :] = hb[b * LRU_TB:(b + 1) * LRU_TB]

    for d in range(2):
        def trow(s):
            return pl.ds(pl.multiple_of((s if d == 0 else t - 1 - s) * NPS, NPS), NPS)

        if segmented:
            def sweep(s, carry):
                h, p = carry
                rows = trow(s)
                a = a_sc[d, rows, :]
                return a * h + b_sc[d, rows, :], a * p
            e, p = lax.fori_loop(0, t, sweep, (jnp.zeros((NPS, w), F32), jnp.ones((NPS, w), F32)),
                                 unroll=LRU_UNROLL)
            e_sc[...] = e
            p_sc[...] = p
            nb = NPS // nseg
            for b in range(nb):
                prev = h0_ref[d, b:b + 1, :]
                for k in (range(nseg) if d == 0 else range(nseg - 1, -1, -1)):
                    j = b * nseg + k
                    s_sc[j:j + 1, :] = prev
                    prev = p_sc[j:j + 1, :] * prev + e_sc[j:j + 1, :]
            h_init = s_sc[...]
        else:
            h_init = h0_ref[d]

        def scan(s, h):
            rows = trow(s)
            h = a_sc[d, rows, :] * h + b_sc[d, rows, :]
            if d == 0:
                hs_sc[rows, :] = h
            else:
                hs_sc[rows, :] += h
            return h
        if d == 0:
            h_fin = lax.fori_loop(0, t, scan, h_init, unroll=LRU_UNROLL)
        else:
            def sweep_block(k, h):
                store_bm(jnp.minimum(nbt - k, nbt - 1))
                for s in range(LRU_TB):
                    h = scan(k * LRU_TB + s, h)
                return h
            h_fin = lax.fori_loop(0, nbt, sweep_block, h_init)
            store_bm(0)
        if not segmented:
            stout_ref[d] = h_fin


def _lru(proj, x_col, width, conv_w, conv_b, wa, ba, wx, bx, lam, h0, *, t, glen, nseg):
    segmented = nseg > 1
    r = proj.shape[0]
    heads, blk = wa.shape[1], wa.shape[2]
    col = lambda hd: (0, hd)
    col3 = lambda hd: (0, 0, hd)
    out_shape = [jax.ShapeDtypeStruct((r, width), F32)]
    out_specs = [pl.BlockSpec((r, blk), col)]
    scratch = [pltpu.VMEM((r + (t // glen + 1) * CONV_LEFT * NPS, blk), F32),
               pltpu.VMEM((2, r, blk), F32), pltpu.VMEM((2, r, blk), F32), pltpu.VMEM((r, blk), F32)]
    if segmented:
        scratch += [pltpu.VMEM((NPS, blk), F32)] * 3
    else:
        out_shape.append(jax.ShapeDtypeStruct((2, NPS, width), F32))
        out_specs.append(pl.BlockSpec((2, NPS, blk), col3))
    res = pl.pallas_call(
        functools.partial(_lru_kernel, t=t, glen=glen, nseg=nseg, segmented=segmented),
        grid=(heads,),
        in_specs=[pl.BlockSpec((r, blk), lambda hd: (0, x_col // blk + hd)),
                  pl.BlockSpec((CONV_W, blk), col),
                  pl.BlockSpec((1, blk), col),
                  pl.BlockSpec((2, None, blk, blk), lambda hd: (0, hd, 0, 0)),
                  pl.BlockSpec((2, 1, blk), col3),
                  pl.BlockSpec((2, None, blk, blk), lambda hd: (0, hd, 0, 0)),
                  pl.BlockSpec((2, 1, blk), col3),
                  pl.BlockSpec((2, 1, blk), col3),
                  pl.BlockSpec((2, h0.shape[1], blk), col3)],
        out_specs=out_specs, out_shape=out_shape, scratch_shapes=scratch,
        compiler_params=_cparams("arbitrary"), name="lru",
    )(proj, conv_w, conv_b, wa, ba, wx, bx, lam, h0)
    return (res[0], res[1]) if not segmented else (res[0], None)


def _odd_out_kernel(hs_ref, gate_ref, x_ref, mod_ref, wo_ref, ng_ref, out_ref, h_ref):
    y = _dot((hs_ref[...] * _gelu(gate_ref[...])).astype(BF16), wo_ref[...])
    _mixer_residual(x_ref, mod_ref, ng_ref, y, out_ref, h_ref)


def _odd_out(hs, proj, x, mod_l, w_out, norm_next, *, t, row_base, pseq_per_cond):
    m, d = x.shape
    w = hs.shape[1]
    t = OUT_PPS * t
    assert pseq_per_cond % OUT_PPS == 0
    return pl.pallas_call(
        _odd_out_kernel, grid=(NPS // OUT_PPS,),
        in_specs=[pl.BlockSpec((t, w), lambda i: (i, 0)),
                  pl.BlockSpec((t, w), lambda i: (i, 0)),
                  pl.BlockSpec((t, d), lambda i: (i, 0)),
                  pl.BlockSpec((None, N_MOD, d), lambda i: (row_base + (i * OUT_PPS) // pseq_per_cond, 0, 0)),
                  pl.BlockSpec(w_out.shape, lambda i: (0, 0), pipeline_mode=pl.Buffered(1)),
                  pl.BlockSpec((1, d), lambda i: (0, 0))],
        out_specs=[pl.BlockSpec((t, d), lambda i: (i, 0)), pl.BlockSpec((t, d), lambda i: (i, 0))],
        out_shape=[jax.ShapeDtypeStruct((m, d), F32), jax.ShapeDtypeStruct((m, d), BF16)],
        compiler_params=_cparams("arbitrary"), name="odd_out",
    )(hs, proj, x, mod_l, w_out, norm_next)


def kernel(x_prompt, x_sample, state_s5_re, state_s5_im, state_gla, state_lru, c, c_ctx, norm_g, ada_w, ada_b, ffn_w_in, ffn_w_out, final_norm_g, ev_w_in, ev_w_out, s5_lam_re, s5_lam_im, s5_log_step, s5_b_re, s5_b_im, s5_c_re, s5_c_im, s5_d, s5_glu_w, s5_glu_b, gla_gate_w2, gla_gate_b, gla_norm_g, od_w_in, od_w_out, lru_conv_w, lru_conv_b, lru_wa, lru_ba, lru_wx, lru_bx, lru_lam):
    nbc, seq, d = x_prompt.shape
    nbl, dseq, _ = x_sample.shape
    depth = norm_g.shape[0]
    assert nbc == NPS and NPS % nbl == 0
    nseg = NPS // nbl
    tl = dseq // nseg
    grid_w = 64
    assert tl % grid_w == 0 and tl % LRU_TB == 0 and seq % LRU_TB == 0 and seq % GLA_RB == 0 and dseq % GLA_RB == 0
    sw = s5_d.shape[1]
    qk = gla_gate_w2.shape[-1]
    heads = state_gla.shape[3]
    rank = gla_gate_w2.shape[2]
    g5, p5 = s5_lam_re.shape[2], s5_lam_re.shape[3]
    main = ev_w_in.shape[2] - 2 * rank
    assert main % PROJ_TN == 0 and (g5 // S5_GP) % S5_PPG == 0

    passes = [dict(x=x_prompt.reshape(nbc * seq, d), base=0, rpc=nbc * seq, ppc=NPS, t=seq, nseg=1,
                   glen=seq, nb=nbc, seq=seq),
              dict(x=x_sample.reshape(nbl * dseq, d), base=1, rpc=dseq, ppc=nseg, t=tl, nseg=nseg,
                   glen=grid_w, nb=nbl, seq=dseq)]

    cond8 = jnp.concatenate([c_ctx[None, :], c, jnp.zeros((8 - 1 - nbl, d), F32)], axis=0)
    ada_b3 = ada_b.reshape(depth, 1, -1)
    nmc = ada_w.shape[2]
    head = 3 * d
    mod_head = _ada(cond8, ada_w, ada_b3, head)
    mod_l = jnp.concatenate([mod_head, jnp.zeros((8, nmc - head), F32)], axis=1).reshape(8, N_MOD, d)
    fg = final_norm_g.reshape(1, d)

    new_s5_re, new_s5_im, new_gla, new_lru = [], [], [], []
    for l in range(depth):
        ng = lambda s: norm_g[l, s].reshape(1, d)
        mod_rest = None
        for pi, ps in enumerate(passes):
            side = None
            if l == 0 and pi == 0:
                side = (cond8, ada_w, ada_b3, 0, head // ADA_TS, (nmc - head) // ADA_TS)
            res = _ffn(ps['x'], mod_l, ng(0), fg, ffn_w_in, ffn_w_out, l, 0, mi=0,
                       row_base=ps['base'], rows_per_cond=ps['rpc'], final=False, side=side)
            if side is not None:
                res, mod_rest = res
            ps['x'] = res
        if mod_rest is not None:
            mod_l = jnp.concatenate([mod_head, mod_rest], axis=1).reshape(8, N_MOD, d)
        if l % 2 == 0:
            e = l // 2
            ev_w_in_t = jnp.swapaxes(ev_w_in, 1, 2)
            w_o = ev_w_out[e].astype(BF16)
            glu_w = s5_glu_w[e].astype(BF16)
            dkh = qk // heads
            w2p = jnp.stack([jnp.pad(gla_gate_w2[e, dd], ((dd * rank, (1 - dd) * rank), (0, 0))) for dd in range(2)])
            w2p = w2p.reshape(2, 2 * rank, heads, dkh).transpose(1, 2, 0, 3).reshape(2 * rank, 2 * qk).astype(BF16)
            gate_b = gla_gate_b[e].reshape(2, heads, dkh).transpose(1, 0, 2).reshape(1, 2 * qk)
            w1, w2t, w3, a_bc, a_ts = _s5_params(s5_lam_re[e], s5_lam_im[e], s5_log_step[e], s5_b_re[e], s5_b_im[e],
                                                 s5_c_re[e], s5_c_im[e], [ps['t'] for ps in passes])
            for pi, ps in enumerate(passes):
                t = ps['t']
                proj, glr = _norm_proj(ps['x'], mod_l, ng(1), ev_w_in_t, e, 2 * rank, mi=3,
                                       row_base=ps['base'], rows_per_cond=ps['rpc'], w_rows_out=True)
                if pi == 0:
                    h0 = jnp.zeros((2, NPS, a_bc.shape[-1]), F32)
                    s0 = None
                else:
                    h0 = _s5_state_to_cols(state_s5_re[:, e], state_s5_im[:, e])
                    s0 = state_gla[:, e]
                y_s5, s5_fin = _s5(proj, sw, w1, w2t, w3, a_bc, a_ts[pi], h0, t=t, nseg=ps['nseg'])
                o_gla, gla_fin = _gla(proj, sw, glr, w2p, gate_b, gla_norm_g[e].reshape(1, -1), s0,
                                      nb=ps['nb'], seq=ps['seq'], heads=heads, want_state=(pi == 0))
                if pi == 0:
                    sr, si = _s5_cols_to_state(s5_fin, g5, p5)
                    new_s5_re.append(sr)
                    new_s5_im.append(si)
                    new_gla.append(gla_fin)
                ps['x'], ps['h'] = _even_out(y_s5, proj, o_gla, ps['x'], mod_l, s5_d[e].reshape(1, sw), glu_w,
                                             s5_glu_b[e].reshape(1, sw), w_o, ng(2), t=t, row_base=ps['base'],
                                             pseq_per_cond=ps['ppc'])
        else:
            o = l // 2
            lw = od_w_in.shape[2] // 2
            w_o = od_w_out[o].astype(BF16)
            wa16 = (0.5 * lru_wa[o]).astype(BF16)
            wx16 = (0.5 * lru_wx[o]).astype(BF16)
            for pi, ps in enumerate(passes):
                t = ps['t']
                proj, _ = _norm_proj(ps['x'], mod_l, ng(1), od_w_in, o, 0, mi=3,
                                     row_base=ps['base'], rows_per_cond=ps['rpc'], w_rows_out=False)
                if pi == 0:
                    h0 = jnp.zeros((2, NPS, lw), F32)
                else:
                    h0 = state_lru[:, o].transpose(1, 0, 2)
                hs, lru_fin = _lru(proj, lw, lw, lru_conv_w[o], lru_conv_b[o].reshape(1, lw), wa16,
                                   0.5 * lru_ba[o].reshape(2, 1, lw), wx16, 0.5 * lru_bx[o].reshape(2, 1, lw),
                                   lru_lam[o].reshape(2, 1, lw), h0, t=t, glen=ps['glen'], nseg=ps['nseg'])
                if pi == 0:
                    new_lru.append(lru_fin.transpose(1, 0, 2))
                ps['x'], ps['h'] = _odd_out(hs, proj, ps['x'], mod_l, w_o, ng(2), t=t, row_base=ps['base'],
                                            pseq_per_cond=ps['ppc'])
        mod_next = None
        for pi, ps in enumerate(passes):
            side = (cond8, ada_w, ada_b3, l + 1, 0, nmc // ADA_TS) if (pi == 0 and l + 1 < depth) else None
            res = _ffn(ps['x'], mod_l, ng(2), fg, ffn_w_in, ffn_w_out, l, 1, mi=6, row_base=ps['base'],
                       rows_per_cond=ps['rpc'], final=(l == depth - 1), h_pre=ps['h'], side=side)
            if side is not None:
                res, mod_next = res[0], res[1].reshape(8, N_MOD, d)
            ps['x'] = res
        mod_l = mod_next

    y_prompt = passes[0]['x'].reshape(nbc, seq, d)
    y_sample = passes[1]['x'].reshape(nbl, dseq, d)
    return (y_prompt, y_sample, jnp.stack(new_s5_re, 1), jnp.stack(new_s5_im, 1),
            jnp.stack(new_gla, 1), jnp.stack(new_lru, 1))
```
